```python
import jax
import jax.numpy as jnp
from jax import lax
import numpy as np

D_MODEL = 1024
BATCH = 16
SEQ = 2048
DEPTH = 4

CTX_LEN = 256
GRID_W = 64
N_MOD = 6
EPS = 1e-6
MLA_W = D_MODEL // 2
V_HEAD = 64
N_MLA_HEADS = MLA_W // V_HEAD
QK_NOPE = 64
QK_ROPE = 32
QK_DIM = QK_NOPE + QK_ROPE
Q_LORA = 3 * D_MODEL // 8
KV_LORA = D_MODEL // 4
ROPE_BASE = 10000.0
Q_BLOCK = 128
HGRN_W = D_MODEL // 2
HGRN_K = 128
N_HGRN_HEADS = HGRN_W // HGRN_K
HGRN_V = HGRN_W // N_HGRN_HEADS
CHUNK = 64
IN_SPLITS = (Q_LORA, KV_LORA, QK_ROPE, HGRN_W, HGRN_W, HGRN_W, HGRN_W, HGRN_W)
IN_W = Q_LORA + KV_LORA + QK_ROPE + 5 * HGRN_W
MIX_W = MLA_W + HGRN_W
POOL_WINDOWS = (2, 4, 8, 16)
N_POOL = 4
POOL_C = D_MODEL // N_POOL
D_FF = ((8 * D_MODEL // 3 + 255) // 256) * 256
N_EXPERTS = 8
TOP_K = 2
D_FF_EXPERT = 7 * D_MODEL // 2
N_AB = (DEPTH + 1) // 2
N_C = DEPTH // 2

kernel_name = "hybrid_mla_hgrn2_pool_moe_dit"

F32 = jnp.float32


def rmsnorm(x, g):
    xf = x.astype(F32)
    y = xf * lax.rsqrt(jnp.mean(xf * xf, axis=-1, keepdims=True) + EPS) * g.astype(F32)
    return y.astype(x.dtype)


def modulate(x, shift, scale):
    return x * (1.0 + scale) + shift


def split_cols(p):
    out, start = [], 0
    for w in IN_SPLITS:
        out.append(p[..., start:start + w])
        start += w
    return out


def rope_tables(n_tok):
    rows = n_tok // GRID_W
    row = jnp.repeat(jnp.arange(rows, dtype=F32), GRID_W)
    col = jnp.tile(jnp.arange(GRID_W, dtype=F32), rows)
    half = QK_ROPE // 2
    inv = 1.0 / (ROPE_BASE ** (jnp.arange(0, half, 2, dtype=F32) / half))
    ar = row[:, None] * inv[None, :]
    ac = col[:, None] * inv[None, :]
    ang = jnp.concatenate([ar, ar, ac, ac], axis=-1)
    return jnp.cos(ang)[:, None, :], jnp.sin(ang)[:, None, :]


def rope2d(x, cos, sin):
    xs = x.reshape(x.shape[:-1] + (2, 2, QK_ROPE // 4))
    rot = jnp.stack([-xs[..., 1, :], xs[..., 0, :]], axis=-2).reshape(x.shape)
    return x * cos.astype(x.dtype) + rot * sin.astype(x.dtype)


def mla_q(cq, g_cq, w_uq, cos, sin):
    b, t, _ = cq.shape
    q = (rmsnorm(cq, g_cq) @ w_uq).reshape(b, t, N_MLA_HEADS, QK_DIM)
    q_nope, q_rope = q[..., :QK_NOPE], q[..., QK_NOPE:]
    if cos is not None:
        q_rope = rope2d(q_rope, cos, sin)
    return jnp.concatenate([q_nope, q_rope], axis=-1)


def mla_kv(ckv, kr, g_ckv, w_ukv, cos, sin):
    b, t, _ = ckv.shape
    kv = (rmsnorm(ckv, g_ckv) @ w_ukv).reshape(b, t, N_MLA_HEADS, QK_NOPE + V_HEAD)
    k_nope, v = kv[..., :QK_NOPE], kv[..., QK_NOPE:]
    kr = kr[:, :, None, :]
    if cos is not None:
        kr = rope2d(kr, cos, sin)
    k = jnp.concatenate([k_nope, jnp.broadcast_to(kr, (b, t, N_MLA_HEADS, QK_ROPE))], axis=-1)
    return k, v


def softmax_attend(q, k, v):
    s = jnp.einsum('bqhd,bkhd->bhqk', q, k).astype(F32) * (QK_DIM ** -0.5)
    p = jax.nn.softmax(s, axis=-1).astype(v.dtype)
    return jnp.einsum('bhqk,bkhd->bqhd', p, v)


def blocked_attend(q, k, v):
    b, t, h, dk = q.shape
    nb = t // Q_BLOCK
    qb = q.reshape(b, nb, Q_BLOCK, h, dk).transpose(1, 0, 2, 3, 4)
    o = lax.map(lambda qi: softmax_attend(qi, k, v), qb)
    return o.transpose(1, 0, 2, 3, 4).reshape(b, t, h, v.shape[-1])


def hgrn_heads(a):
    return a.reshape(a.shape[0], a.shape[1], N_HGRN_HEADS, -1)


def hgrn_gates(hf, lb):
    hf = hf.astype(F32)
    k_in = (1.0 - lb) * jax.nn.sigmoid(-hf)
    log_f = jnp.log1p(-k_in)
    return k_in, log_f


def gla_chunked(q, k, v, log_f, s0):
    b, t, h, dk = q.shape
    dv = v.shape[-1]
    n = t // CHUNK

    def to_chunks(a):
        return a.astype(F32).reshape(b, n, CHUNK, h, a.shape[-1]).transpose(1, 0, 3, 2, 4)

    qc, kc, vc, gc = to_chunks(q), to_chunks(k), to_chunks(v), to_chunks(log_f)
    lower = jnp.tril(jnp.ones((CHUNK, CHUNK), dtype=bool))[:, :, None]

    def step(state, inp):
        qi, ki, vi, gi = inp
        g_cum = jnp.cumsum(gi, axis=2)
        o_inter = jnp.einsum('bhlk,bhkv->bhlv', qi * jnp.exp(g_cum), state)
        diff = g_cum[:, :, :, None, :] - g_cum[:, :, None, :, :]
        decay = jnp.where(lower, jnp.exp(jnp.where(lower, diff, 0.0)), 0.0)
        attn = jnp.einsum('bhtk,bhsk,bhtsk->bhts', qi, ki, decay)
        o = o_inter + jnp.einsum('bhts,bhsv->bhtv', attn, vi)
        g_last = g_cum[:, :, -1:, :]
        new_state = (jnp.exp(g_last[:, :, 0, :])[..., None] * state
                     + jnp.einsum('bhlk,bhlv->bhkv', ki * jnp.exp(g_last - g_cum), vi))
        return new_state, o

    s_final, o = lax.scan(step, s0, (qc, kc, vc, gc))
    return o.transpose(1, 0, 3, 2, 4).reshape(b, t, h, dv), s_final


def final_state(k, v, log_f):
    g_cum = jnp.cumsum(log_f, axis=1)
    dec = jnp.exp(g_cum[:, -1:] - g_cum)
    return jnp.einsum('bthk,bthv->bhkv', k * dec, v.astype(F32))


def flip(a):
    return jnp.flip(a, axis=1)


def hgrn_readout(o, hg, g_norm):
    b, t = o.shape[0], o.shape[1]
    y = rmsnorm(o, g_norm) * jax.nn.silu(hgrn_heads(hg).astype(F32))
    return y.reshape(b, t, HGRN_W).astype(hg.dtype)


def mixer_ab(u, uc, w_in, g_cq, g_ckv, w_uq, w_ukv, lb, g_norm, w_out, cos, sin, need_ctx_out):
    b, s, _ = u.shape
    cq, ckv, kr, hq, hff, hfb, hi, hg = split_cols(u @ w_in)
    cq_c, ckv_c, kr_c, hq_c, hff_c, hfb_c, hi_c, hg_c = split_cols(uc @ w_in)

    q = mla_q(cq, g_cq, w_uq, cos, sin)
    k, v = mla_kv(ckv, kr, g_ckv, w_ukv, cos, sin)
    k_c, v_c = mla_kv(ckv_c, kr_c, g_ckv, w_ukv, None, None)
    a_lat = blocked_attend(q, jnp.concatenate([k, k_c], axis=1),
                           jnp.concatenate([v, v_c], axis=1)).reshape(b, s, MLA_W)

    qh = jax.nn.silu(hgrn_heads(hq).astype(F32))
    vh = hgrn_heads(hi).astype(F32)
    kf, lgf = hgrn_gates(hgrn_heads(hff), lb[0])
    kb, lgb = hgrn_gates(hgrn_heads(hfb), lb[1])
    qh_c = jax.nn.silu(hgrn_heads(hq_c).astype(F32))
    vh_c = hgrn_heads(hi_c).astype(F32)
    kf_c, lgf_c = hgrn_gates(hgrn_heads(hff_c), lb[0])
    kb_c, lgb_c = hgrn_gates(hgrn_heads(hfb_c), lb[1])
    zero = jnp.zeros((b, N_HGRN_HEADS, HGRN_K, HGRN_V), F32)
    if need_ctx_out:
        oc_f, sc_f = gla_chunked(qh_c, kf_c, vh_c, lgf_c, zero)
        oc_b, sc_b = gla_chunked(flip(qh_c), flip(kb_c), flip(vh_c), flip(lgb_c), zero)
        oc = oc_f + flip(oc_b)
    else:
        sc_f = final_state(kf_c, vh_c, lgf_c)
        sc_b = final_state(flip(kb_c), flip(vh_c), flip(lgb_c))
    ol_f, _ = gla_chunked(qh, kf, vh, lgf, sc_f)
    ol_b, _ = gla_chunked(flip(qh), flip(kb), flip(vh), flip(lgb), sc_b)
    b_lat = hgrn_readout(ol_f + flip(ol_b), hg, g_norm)

    y = jnp.concatenate([a_lat, b_lat], axis=-1) @ w_out
    yc = None
    if need_ctx_out:
        qc = mla_q(cq_c, g_cq, w_uq, None, None)
        a_ctx = softmax_attend(qc, k_c, v_c).reshape(uc.shape[0], uc.shape[1], MLA_W)
        b_ctx = hgrn_readout(oc, hg_c, g_norm)
        yc = jnp.concatenate([a_ctx, b_ctx], axis=-1) @ w_out
    return y, yc


def pool_mix(u, w, bias, scale):
    b, t, _ = u.shape
    uf = u.astype(F32)
    cs = jnp.pad(jnp.cumsum(uf, axis=1), ((0, 0), (1, 0), (0, 0))).reshape(b, t + 1, N_POOL, POOL_C)
    pos = jnp.arange(t)[:, None]
    win = jnp.array(POOL_WINDOWS, dtype=jnp.int32)[None, :]
    lo = jnp.clip(pos - win // 2, 0, t)
    hi = jnp.clip(pos - win // 2 + win, 0, t)
    grp = jnp.arange(N_POOL)[None, :]
    win_sum = cs[:, hi, grp] - cs[:, lo, grp]
    cnt = (hi - lo).astype(F32)[None, :, :, None]
    pooled = (win_sum / cnt - uf.reshape(b, t, N_POOL, POOL_C)).astype(u.dtype)
    y = jnp.einsum('btgc,gcd->btgd', pooled, w) + bias
    return y.reshape(b, t, D_MODEL) * scale


def swiglu(h, wg, wu, wd):
    return (jax.nn.silu(h @ wg) * (h @ wu)) @ wd


def moe(h, router, wg, wu, wd):
    logits = (h @ router).astype(F32)
    top_v, top_i = lax.top_k(logits, TOP_K)
    top_w = jax.nn.softmax(top_v, axis=-1)
    gates = jnp.sum(jax.nn.one_hot(top_i, N_EXPERTS, dtype=F32) * top_w[..., None], axis=-2)
    y = jnp.zeros_like(h)
    for e in range(N_EXPERTS):
        y = y + gates[..., e:e + 1].astype(h.dtype) * swiglu(h, wg[e], wu[e], wd[e])
    return y


def setup_inputs(seed: int = 0) -> dict:
    key = jax.random.key(seed)
    keys = iter(jax.random.split(key, 40))

    def nrm(shape, scale):
        return jax.random.normal(next(keys), shape, F32) * scale

    d = D_MODEL
    return {
        "x": nrm((BATCH, SEQ, d), 1.0),
        "c": nrm((BATCH, d), 1.0),
        "ctx": nrm((BATCH, CTX_LEN, d), 1.0),
        "c_ctx": nrm((d,), 1.0),
        "w_mod": nrm((DEPTH, d, N_MOD * d), 0.5 * d ** -0.5),
        "b_mod": nrm((DEPTH, N_MOD * d), 0.02),
        "norm_g": 1.0 + nrm((DEPTH, 2, d), 0.1),
        "final_g": 1.0 + nrm((d,), 0.1),
        "ab_w_in": nrm((N_AB, d, IN_W), d ** -0.5),
        "ab_g_cq": 1.0 + nrm((N_AB, Q_LORA), 0.1),
        "ab_g_ckv": 1.0 + nrm((N_AB, KV_LORA), 0.1),
        "ab_w_uq": nrm((N_AB, Q_LORA, N_MLA_HEADS * QK_DIM), Q_LORA ** -0.5),
        "ab_w_ukv": nrm((N_AB, KV_LORA, N_MLA_HEADS * (QK_NOPE + V_HEAD)), KV_LORA ** -0.5),
        "hgrn_lb_logits": nrm((N_AB, 2, HGRN_W), 0.5),
        "hgrn_g_norm": 1.0 + nrm((N_AB, HGRN_V), 0.1),
        "ab_w_out": nrm((N_AB, MIX_W, d), MIX_W ** -0.5),
        "ffn_w_gate": nrm((N_AB, d, D_FF), d ** -0.5),
        "ffn_w_up": nrm((N_AB, d, D_FF), d ** -0.5),
        "ffn_w_down": nrm((N_AB, D_FF, d), D_FF ** -0.5),
        "pool_w": nrm((N_C, N_POOL, POOL_C, POOL_C), POOL_C ** -0.5),
        "pool_b": nrm((N_C, N_POOL, POOL_C), 0.02),
        "pool_scale": 1.0 + nrm((N_C, d), 0.1),
        "moe_router": nrm((N_C, d, N_EXPERTS), d ** -0.5),
        "moe_w_gate": nrm((N_C, N_EXPERTS, d, D_FF_EXPERT), d ** -0.5),
        "moe_w_up": nrm((N_C, N_EXPERTS, d, D_FF_EXPERT), d ** -0.5),
        "moe_w_down": nrm((N_C, N_EXPERTS, D_FF_EXPERT, d), D_FF_EXPERT ** -0.5),
    }


def reference(x, c, ctx, c_ctx, w_mod, b_mod, norm_g, final_g, ab_w_in, ab_g_cq, ab_g_ckv,
              ab_w_uq, ab_w_ukv, hgrn_lb_logits, hgrn_g_norm, ab_w_out, ffn_w_gate, ffn_w_up,
              ffn_w_down, pool_w, pool_b, pool_scale, moe_router, moe_w_gate, moe_w_up, moe_w_down):
    seq = x.shape[1]
    cos, sin = rope_tables(seq)
    lb_p = jax.nn.softmax(hgrn_lb_logits.astype(F32), axis=0)
    lb_all = (jnp.cumsum(lb_p, axis=0) - lb_p[:1]).reshape(N_AB, 2, N_HGRN_HEADS, HGRN_K)
    silu_c = jax.nn.silu(c)
    silu_cc = jax.nn.silu(c_ctx)
    h, hc = x, ctx
    for l in range(DEPTH):
        j = l // 2
        even = l % 2 == 0
        ctx_later = any(m % 2 == 0 for m in range(l + 1, DEPTH))
        sh1, sc1, g1, sh2, sc2, g2 = jnp.split((silu_c @ w_mod[l] + b_mod[l])[:, None, :], N_MOD, axis=-1)
        u = modulate(rmsnorm(h, norm_g[l, 0]), sh1, sc1)
        uc = None
        if even or ctx_later:
            csh1, csc1, cg1, csh2, csc2, cg2 = jnp.split(
                (silu_cc @ w_mod[l] + b_mod[l])[None, None, :], N_MOD, axis=-1)
            uc = modulate(rmsnorm(hc, norm_g[l, 0]), csh1, csc1)
        if even:
            y, yc = mixer_ab(u, uc, ab_w_in[j], ab_g_cq[j], ab_g_ckv[j], ab_w_uq[j], ab_w_ukv[j],
                             lb_all[j], hgrn_g_norm[j], ab_w_out[j], cos, sin, ctx_later)
        else:
            y = pool_mix(u, pool_w[j], pool_b[j], pool_scale[j])
            yc = pool_mix(uc, pool_w[j], pool_b[j], pool_scale[j]) if ctx_later else None
        h = h + g1 * y
        v = modulate(rmsnorm(h, norm_g[l, 1]), sh2, sc2)
        if even:
            h = h + g2 * swiglu(v, ffn_w_gate[j], ffn_w_up[j], ffn_w_down[j])
        else:
            h = h + g2 * moe(v, moe_router[j], moe_w_gate[j], moe_w_up[j], moe_w_down[j])
        if ctx_later:
            hc = hc + cg1 * yc
            vc = modulate(rmsnorm(hc, norm_g[l, 1]), csh2, csc2)
            if even:
                hc = hc + cg2 * swiglu(vc, ffn_w_gate[j], ffn_w_up[j], ffn_w_down[j])
            else:
                hc = hc + cg2 * moe(vc, moe_router[j], moe_w_gate[j], moe_w_up[j], moe_w_down[j])
    return rmsnorm(h, final_g)
```

```python
import functools

import jax
import jax.numpy as jnp
from jax import lax
from jax.experimental import pallas as pl
from jax.experimental.pallas import tpu as pltpu

F32 = jnp.float32
BF16 = jnp.bfloat16

D_MODEL = 1024
GRID_W = 64
N_MOD = 6
EPS = 1e-6
N_HEADS = 8
V_HEAD = 64
QK_NOPE = 64
QK_ROPE = 32
QK_DIM = QK_NOPE + QK_ROPE
Q_LORA = 384
KV_LORA = 256
ROPE_BASE = 10000.0
MLA_W = 512
HGRN_W = 512
HGRN_K = 128
N_HGRN_HEADS = 4
POOL_WINDOWS = (2, 4, 8, 16)
POOL_C = 256
N_EXPERTS = 8

LANES = 128
SUBLANES = 8
VMEM_LIMIT_BYTES = 56 * 1024 * 1024

HEAD_PAD = LANES
QKV_W = N_HEADS * HEAD_PAD
KR_OFF = Q_LORA + KV_LORA
KRR_OFF = KR_OFF + LANES
H_OFF = KRR_OFF + LANES
IN_COLS = H_OFF + 5 * HGRN_W
GLA_CHUNK = 64
GLA_BLOCK = 256
ROW_TILE = 256
MOE_TILE = 512
MOE_FF_TILE = 512
EXP_CAP = 80.0


def _params(*sem):
    return pltpu.CompilerParams(dimension_semantics=sem, vmem_limit_bytes=VMEM_LIMIT_BYTES)


def _dot(a, b):
    return jnp.dot(a, b, preferred_element_type=F32)


def _dot_nt(a, b):
    return lax.dot_general(a, b, (((1,), (1,)), ((), ())), preferred_element_type=F32)


def _dot_tn(a, b):
    return lax.dot_general(a, b, (((0,), (0,)), ((), ())), preferred_element_type=F32)


def _rms(x, g):
    return x * lax.rsqrt(jnp.mean(x * x, axis=-1, keepdims=True) + EPS) * g


def _silu(x):
    return x * jax.nn.sigmoid(x)


def _norm_mod(h, g, shift, scale):
    return _rms(h, g) * (1.0 + scale) + shift


def _mod_kernel(c_ref, w_ref, b_ref, o_ref):
    c = c_ref[...]
    o_ref[...] = jnp.dot(_silu(c), w_ref[...], preferred_element_type=F32,
                         precision=lax.Precision.HIGHEST) + b_ref[...]


def _mod_table(cvec, w_mod, b_mod):
    depth, d, nd = w_mod.shape
    g = cvec.shape[0]
    tn = 1536
    return pl.pallas_call(
        _mod_kernel,
        grid=(depth, nd // tn),
        in_specs=[pl.BlockSpec((g, d), lambda l, j: (0, 0)),
                  pl.BlockSpec((None, d, tn), lambda l, j: (l, 0, j)),
                  pl.BlockSpec((None, 1, tn), lambda l, j: (l, 0, j))],
        out_specs=pl.BlockSpec((None, g, tn), lambda l, j: (l, 0, j)),
        out_shape=jax.ShapeDtypeStruct((depth, g, nd), F32),
        compiler_params=_params("parallel", "parallel"),
        name="mod_table",
    )(cvec, w_mod, b_mod.reshape(depth, 1, nd))


def _inproj_kernel(h_ref, mod_ref, ng_ref, win_ref, gcq_ref, gckv_ref, wq_ref, wkv_ref,
                   lb_ref, cq_tab, sq_tab, ck_tab, sk_tab,
                   q_ref, k_ref, v_ref, qh_ref, kf_ref, lgf_ref, kb_ref, lgb_ref, vh_ref, hg_ref):
    u = _norm_mod(h_ref[...], ng_ref[...], mod_ref[0:1, :], mod_ref[1:2, :]).astype(BF16)
    p = _dot(u, win_ref[...])
    cqn = _rms(p[:, 0:Q_LORA], gcq_ref[...]).astype(BF16)
    a = _dot(cqn, wq_ref[...])
    cq = jnp.tile(cq_tab[...], (1, N_HEADS))
    sq = jnp.tile(sq_tab[...], (1, N_HEADS))
    q_ref[...] = (a[:, :QKV_W] * cq + a[:, QKV_W:] * sq).astype(BF16)
    ckvn = _rms(p[:, Q_LORA:KR_OFF], gckv_ref[...]).astype(BF16)
    kr = (p[:, KR_OFF:KRR_OFF] * ck_tab[...] + p[:, KRR_OFF:H_OFF] * sk_tab[...]).astype(BF16)
    kv = _dot(jnp.concatenate([ckvn, kr], axis=-1), wkv_ref[...])
    k_ref[...] = kv[:, :QKV_W].astype(BF16)
    lane = lax.broadcasted_iota(jnp.int32, (1, QKV_W), 1)
    ones_col = (lane % HEAD_PAD == V_HEAD).astype(F32)
    v_ref[...] = (kv[:, QKV_W:] + ones_col).astype(BF16)
    o = H_OFF
    qh_ref[...] = _silu(p[:, o:o + HGRN_W])
    kf = (1.0 - lb_ref[0:1, :]) * jax.nn.sigmoid(-p[:, o + HGRN_W:o + 2 * HGRN_W])
    kf_ref[...] = kf
    lgf_ref[...] = jnp.log1p(-kf)
    kb = (1.0 - lb_ref[1:2, :]) * jax.nn.sigmoid(-p[:, o + 2 * HGRN_W:o + 3 * HGRN_W])
    kb_ref[...] = kb
    lgb_ref[...] = jnp.log1p(-kb)
    vh_ref[...] = p[:, o + 3 * HGRN_W:o + 4 * HGRN_W]
    hg_ref[...] = p[:, o + 4 * HGRN_W:o + 5 * HGRN_W]


def _inproj(h, mod, ng, win, gcq, gckv, wq, wkv, lb, tabs, dims):
    n, d = h.shape
    b, s, ctx = dims
    tm = ROW_TILE
    n_lat_t = b * s // tm
    pos_blocks = s // tm

    def grp(i):
        return jnp.minimum(i * tm // s, b)

    def pos(i):
        return jnp.where(i < n_lat_t, i % pos_blocks, pos_blocks)

    row = lambda i: (i, 0)
    const = lambda i: (0, 0)
    tab_spec = pl.BlockSpec((tm, LANES), lambda i: (pos(i), 0))
    wide = jax.ShapeDtypeStruct((n, QKV_W), BF16)
    hg = jax.ShapeDtypeStruct((n, HGRN_W), F32)
    return pl.pallas_call(
        _inproj_kernel,
        grid=(n // tm,),
        in_specs=[pl.BlockSpec((tm, d), row),
                  pl.BlockSpec((None, N_MOD, d), lambda i: (grp(i), 0, 0)),
                  pl.BlockSpec((1, d), const),
                  pl.BlockSpec(win.shape, const),
                  pl.BlockSpec((1, Q_LORA), const),
                  pl.BlockSpec((1, KV_LORA), const),
                  pl.BlockSpec(wq.shape, const),
                  pl.BlockSpec(wkv.shape, const),
                  pl.BlockSpec((2, HGRN_W), const),
                  tab_spec, tab_spec, tab_spec, tab_spec],
        out_specs=[pl.BlockSpec((tm, QKV_W), row)] * 3 + [pl.BlockSpec((tm, HGRN_W), row)] * 7,
        out_shape=[wide] * 3 + [hg] * 7,
        compiler_params=_params("parallel"),
        name="inproj",
    )(h, mod, ng, win, gcq, gckv, wq, wkv, lb, *tabs)


def _attn_kernel(*refs, n_kv):
    q_ref = refs[0]
    k_refs = refs[1:1 + 2 * n_kv:2]
    v_refs = refs[2:2 + 2 * n_kv:2]
    o_ref = refs[-1]
    for hd in range(N_HEADS):
        sl = slice(hd * HEAD_PAD, (hd + 1) * HEAD_PAD)
        q = q_ref[:, sl]
        scores = [_dot_nt(q, k_ref[:, sl]) for k_ref in k_refs]
        m = functools.reduce(jnp.maximum, [jnp.max(sc, axis=-1, keepdims=True) for sc in scores])
        acc = None
        for sc, v_ref in zip(scores, v_refs):
            part = _dot(jnp.exp(sc - m).astype(BF16), v_ref[:, sl])
            acc = part if acc is None else acc + part
        o = acc[:, :V_HEAD] / acc[:, V_HEAD:V_HEAD + 1]
        o_ref[:, hd * V_HEAD:(hd + 1) * V_HEAD] = o.astype(o_ref.dtype)


def _attention_latent(q, k, v, dims):
    n = q.shape[0]
    b, s, ctx = dims
    tq = ROW_TILE
    nq = s // tq
    lat = lambda bi, j: (bi, 0)
    cx = lambda bi, j: (b * s // ctx + bi, 0)
    return pl.pallas_call(
        functools.partial(_attn_kernel, n_kv=2),
        grid=(b, nq),
        in_specs=[pl.BlockSpec((tq, QKV_W), lambda bi, j: (bi * nq + j, 0)),
                  pl.BlockSpec((s, QKV_W), lat), pl.BlockSpec((s, QKV_W), lat),
                  pl.BlockSpec((ctx, QKV_W), cx), pl.BlockSpec((ctx, QKV_W), cx)],
        out_specs=pl.BlockSpec((tq, MLA_W), lambda bi, j: (bi * nq + j, 0)),
        out_shape=jax.ShapeDtypeStruct((n, MLA_W), BF16),
        compiler_params=_params("parallel", "arbitrary"),
        name="attn_latent",
    )(q, k, v, k, v)


def _attention_ctx(a_lat, q, k, v, dims):
    b, s, ctx = dims
    cx = lambda bi: (b * s // ctx + bi, 0)
    return pl.pallas_call(
        functools.partial(_attn_kernel, n_kv=1),
        grid=(b,),
        in_specs=[pl.BlockSpec((ctx, QKV_W), cx), pl.BlockSpec((ctx, QKV_W), cx),
                  pl.BlockSpec((ctx, QKV_W), cx), pl.BlockSpec(memory_space=pl.ANY)],
        out_specs=pl.BlockSpec((ctx, MLA_W), cx),
        out_shape=jax.ShapeDtypeStruct(a_lat.shape, a_lat.dtype),
        input_output_aliases={3: 0},
        compiler_params=_params("parallel"),
        name="attn_ctx",
    )(q, k, v, a_lat)


def _bcast_block_row(g, bs, row):
    if bs == GLA_CHUNK:
        return g[row:row + 1, :]
    g3 = g.reshape(GLA_CHUNK // bs, bs, g.shape[-1])
    return jnp.broadcast_to(g3[:, row:row + 1, :], g3.shape).reshape(g.shape)


def _gla_direction(q_ref, k_ref, lg_ref, v_ref, o_ref, st_ref, st_base, reverse):
    c = GLA_CHUNK
    t_idx = lax.broadcasted_iota(jnp.int32, (c, c), 0)
    s_idx = lax.broadcasted_iota(jnp.int32, (c, c), 1)
    pt = (c - 1 - t_idx) if reverse else t_idx
    ps = (c - 1 - s_idx) if reverse else s_idx
    tri = (ps <= pt).astype(BF16)
    levels = []
    for half in (32, 16, 8):
        bs = 2 * half
        mask = (t_idx // bs == s_idx // bs) & (pt % bs >= half) & (ps % bs < half)
        levels.append((bs, half if reverse else half - 1, 0.0, mask))
    dmask = (t_idx // SUBLANES == s_idx // SUBLANES) & (ps <= pt)
    levels.append((SUBLANES, 4 if reverse else 3, EXP_CAP, dmask))
    end_row = 0 if reverse else c - 1

    n_chunks = GLA_BLOCK // c
    order = range(n_chunks - 1, -1, -1) if reverse else range(n_chunks)
    for ci in order:
        rows = slice(ci * c, (ci + 1) * c)
        lg = lg_ref[rows, :]
        hi = lg.astype(BF16)
        r1 = lg - hi.astype(F32)
        mid = r1.astype(BF16)
        lo = (r1 - mid.astype(F32)).astype(BF16)
        gc = _dot(tri, hi) + _dot(tri, mid) + _dot(tri, lo)
        for hd in range(N_HGRN_HEADS):
            hs = slice(hd * HGRN_K, (hd + 1) * HGRN_K)
            g = gc[:, hs]
            qh = q_ref[rows, hs]
            kh = k_ref[rows, hs]
            vh = v_ref[rows, hs].astype(BF16)
            st = st_ref[st_base + hd]
            g_end = g[end_row:end_row + 1, :]
            attn = jnp.zeros((c, c), F32)
            for bs, row, cap, mask in levels:
                ref = _bcast_block_row(g, bs, row)
                qf = (qh * jnp.exp(jnp.minimum(g - ref, cap))).astype(BF16)
                kf = (kh * jnp.exp(jnp.minimum(ref - g, cap))).astype(BF16)
                attn = jnp.where(mask, _dot_nt(qf, kf), attn)
            qe = (qh * jnp.exp(g)).astype(BF16)
            ke = (kh * jnp.exp(g_end - g)).astype(BF16)
            o = _dot_nt(qe, st.astype(BF16)) + _dot(attn.astype(BF16), vh)
            o_ref[rows, hs] = o
            st_ref[st_base + hd] = st * jnp.exp(g_end) + _dot_tn(vh, ke)


def _gla_kernel(qf_ref, kf_ref, lgf_ref, vf_ref, qb_ref, kb_ref, lgb_ref, vb_ref,
                of_ref, ob_ref, st_ref):
    @pl.when(pl.program_id(1) == 0)
    def _():
        st_ref[...] = jnp.zeros_like(st_ref)

    _gla_direction(qf_ref, kf_ref, lgf_ref, vf_ref, of_ref, st_ref, 0, False)
    _gla_direction(qb_ref, kb_ref, lgb_ref, vb_ref, ob_ref, st_ref, N_HGRN_HEADS, True)


def _gla(qh, kf, lgf, kb, lgb, vh, dims):
    n = qh.shape[0]
    b, s, ctx = dims
    blk = GLA_BLOCK
    ncb, nsb, nlat = ctx // blk, s // blk, b * s // blk

    def fwd(bi, j):
        return (jnp.where(j < ncb, nlat + bi * ncb + j, bi * nsb + (j - ncb)), 0)

    def bwd(bi, j):
        return (jnp.where(j < ncb, nlat + bi * ncb + (ncb - 1 - j), bi * nsb + (nsb - 1 - (j - ncb))), 0)

    fs = pl.BlockSpec((blk, HGRN_W), fwd)
    bs = pl.BlockSpec((blk, HGRN_W), bwd)
    out = jax.ShapeDtypeStruct((n, HGRN_W), F32)
    return pl.pallas_call(
        _gla_kernel,
        grid=(b, ncb + nsb),
        in_specs=[fs, fs, fs, fs, bs, bs, bs, bs],
        out_specs=[fs, bs],
        out_shape=[out, out],
        scratch_shapes=[pltpu.VMEM((2 * N_HGRN_HEADS, HGRN_K, HGRN_K), F32)],
        compiler_params=_params("parallel", "arbitrary"),
        name="gla",
    )(qh, kf, lgf, vh, qh, kb, lgb, vh)


def _outproj_kernel(a_ref, of_ref, ob_ref, hg_ref, gn_ref, w_ref, h_ref, mod_ref, o_ref):
    o = of_ref[...] + ob_ref[...]
    gate = _silu(hg_ref[...])
    ys = []
    for hd in range(N_HGRN_HEADS):
        hs = slice(hd * HGRN_K, (hd + 1) * HGRN_K)
        ys.append(_rms(o[:, hs], gn_ref[...]) * gate[:, hs])
    y = jnp.concatenate(ys, axis=-1).astype(BF16)
    mix = _dot(a_ref[...], w_ref[0:MLA_W, :]) + _dot(y, w_ref[MLA_W:, :])
    o_ref[...] = h_ref[...] + mod_ref[2:3, :] * mix


def _outproj(a, of, ob, hg, gn, w, h, mod, dims, n_rows):
    n, d = h.shape
    b, s, ctx = dims
    tm = ROW_TILE
    row = lambda i: (i, 0)
    const = lambda i: (0, 0)
    half = pl.BlockSpec((tm, HGRN_W), row)
    return pl.pallas_call(
        _outproj_kernel,
        grid=(n_rows // tm,),
        in_specs=[pl.BlockSpec((tm, MLA_W), row), half, half, half,
                  pl.BlockSpec((1, HGRN_K), const), pl.BlockSpec(w.shape, const),
                  pl.BlockSpec((tm, d), row),
                  pl.BlockSpec((None, N_MOD, d), lambda i: (jnp.minimum(i * tm // s, b), 0, 0))],
        out_specs=pl.BlockSpec((tm, d), row),
        out_shape=jax.ShapeDtypeStruct((n, d), F32),
        input_output_aliases={6: 0},
        compiler_params=_params("parallel"),
        name="outproj",
    )(a, of, ob, hg, gn, w, h, mod)


def _ffn_kernel(h_ref, mod_ref, ng_ref, wg_ref, wu_ref, wd_ref, o_ref):
    h = h_ref[...]
    v = _norm_mod(h, ng_ref[...], mod_ref[3:4, :], mod_ref[4:5, :]).astype(BF16)
    hid = (_silu(_dot(v, wg_ref[...])) * _dot(v, wu_ref[...])).astype(BF16)
    o_ref[...] = h + mod_ref[5:6, :] * _dot(hid, wd_ref[...])


def _ffn(h, mod, ng, wg, wu, wd, dims, n_rows):
    n, d = h.shape
    b, s, ctx = dims
    tm = ROW_TILE
    row = lambda i: (i, 0)
    const = lambda i: (0, 0)
    return pl.pallas_call(
        _ffn_kernel,
        grid=(n_rows // tm,),
        in_specs=[pl.BlockSpec((tm, d), row),
                  pl.BlockSpec((None, N_MOD, d), lambda i: (jnp.minimum(i * tm // s, b), 0, 0)),
                  pl.BlockSpec((1, d), const),
                  pl.BlockSpec(wg.shape, const, pipeline_mode=pl.Buffered(1)),
                  pl.BlockSpec(wu.shape, const, pipeline_mode=pl.Buffered(1)),
                  pl.BlockSpec(wd.shape, const, pipeline_mode=pl.Buffered(1))],
        out_specs=pl.BlockSpec((tm, d), row),
        out_shape=jax.ShapeDtypeStruct((n, d), F32),
        input_output_aliases={0: 0},
        compiler_params=_params("parallel"),
        name="ffn",
    )(h, mod, ng, wg, wu, wd)


def _pool_kernel(h_ref, prev_ref, next_ref, mod_ref, ng_ref, w_ref, b_ref, sc_ref, o_ref, ext_ref,
                 *, tm, seq_tiles_lat, n_lat_tiles, seq_tiles_ctx):
    i = pl.program_id(0)
    halo = SUBLANES
    shift, scale = mod_ref[0:1, :], mod_ref[1:2, :]
    in_lat = i < n_lat_tiles
    j = jnp.where(in_lat, i % seq_tiles_lat, (i - n_lat_tiles) % seq_tiles_ctx)
    n_seq_tiles = jnp.where(in_lat, seq_tiles_lat, seq_tiles_ctx)
    h = h_ref[...]
    u = _norm_mod(h, ng_ref[...], shift, scale)
    up = _norm_mod(prev_ref[...], ng_ref[...], shift, scale)
    un = _norm_mod(next_ref[...], ng_ref[...], shift, scale)
    ext_ref[0:halo, :] = jnp.where(j > 0, up, 0.0)
    ext_ref[halo:halo + tm, :] = u
    ext_ref[halo + tm:, :] = jnp.where(j < n_seq_tiles - 1, un, 0.0)
    pos = j * tm + lax.broadcasted_iota(jnp.int32, (tm, 1), 0)
    t_len = n_seq_tiles * tm
    outs = []
    for g, win in enumerate(POOL_WINDOWS):
        cs = slice(g * POOL_C, (g + 1) * POOL_C)
        acc = None
        for dlt in range(-(win // 2), win // 2):
            part = ext_ref[halo + dlt:halo + dlt + tm, cs]
            acc = part if acc is None else acc + part
        lo = jnp.maximum(pos - win // 2, 0)
        hi = jnp.minimum(pos - win // 2 + win, t_len)
        cnt = (hi - lo).astype(F32)
        pooled = (acc / cnt - u[:, cs]).astype(BF16)
        outs.append(_dot(pooled, w_ref[g]) + b_ref[g:g + 1, :])
    y = jnp.concatenate(outs, axis=-1) * sc_ref[...]
    o_ref[...] = h + mod_ref[2:3, :] * y


def _pool(h, mod, ng, w, bias, scale, dims, n_rows):
    n, d = h.shape
    b, s, ctx = dims
    tm = ROW_TILE
    hb = tm // SUBLANES
    n_blk8 = n // SUBLANES
    row = lambda i: (i, 0)
    const = lambda i: (0, 0)
    kern = functools.partial(_pool_kernel, tm=tm, seq_tiles_lat=s // tm, n_lat_tiles=b * s // tm,
                             seq_tiles_ctx=ctx // tm)
    return pl.pallas_call(
        kern,
        grid=(n_rows // tm,),
        in_specs=[pl.BlockSpec((tm, d), row),
                  pl.BlockSpec((SUBLANES, d), lambda i: (jnp.maximum(i * hb - 1, 0), 0)),
                  pl.BlockSpec((SUBLANES, d), lambda i: (jnp.minimum((i + 1) * hb, n_blk8 - 1), 0)),
                  pl.BlockSpec((None, N_MOD, d), lambda i: (jnp.minimum(i * tm // s, b), 0, 0)),
                  pl.BlockSpec((1, d), const),
                  pl.BlockSpec(w.shape, lambda i: (0, 0, 0)),
                  pl.BlockSpec(bias.shape, const),
                  pl.BlockSpec((1, d), const)],
        out_specs=pl.BlockSpec((tm, d), row),
        out_shape=jax.ShapeDtypeStruct((n, d), F32),
        scratch_shapes=[pltpu.VMEM((tm + 2 * SUBLANES, d), F32)],
        compiler_params=_params("parallel"),
        name="pool",
    )(h, h, h, mod, ng, w, bias, scale)


def _route_kernel(h_ref, mod_ref, ng_ref, r_ref, v_ref, idx_ref, wt_ref):
    v = _norm_mod(h_ref[...], ng_ref[...], mod_ref[3:4, :], mod_ref[4:5, :])
    v_ref[...] = v.astype(BF16)
    logits = jnp.dot(v, r_ref[...], preferred_element_type=F32, precision=lax.Precision.HIGHEST)
    lane = lax.broadcasted_iota(jnp.int32, logits.shape, 1)
    neg = jnp.float32(-jnp.inf)
    logits = jnp.where(lane < N_EXPERTS, logits, neg)
    m1 = jnp.max(logits, axis=-1, keepdims=True)
    i1 = jnp.min(jnp.where(logits == m1, lane, LANES), axis=-1, keepdims=True)
    rest = jnp.where(lane == i1, neg, logits)
    m2 = jnp.max(rest, axis=-1, keepdims=True)
    i2 = jnp.min(jnp.where(rest == m2, lane, LANES), axis=-1, keepdims=True)
    e2 = jnp.exp(m2 - m1)
    w1 = 1.0 / (1.0 + e2)
    w2 = e2 / (1.0 + e2)
    idx_ref[...] = jnp.where(lane == 0, i1, jnp.where(lane == 1, i2, 0))
    wt_ref[...] = jnp.where(lane == 0, w1, jnp.where(lane == 1, w2, 0.0))


def _route(h, mod, ng, router, dims, n_rows):
    n, d = h.shape
    b, s, ctx = dims
    tm = ROW_TILE
    row = lambda i: (i, 0)
    const = lambda i: (0, 0)
    return pl.pallas_call(
        _route_kernel,
        grid=(n_rows // tm,),
        in_specs=[pl.BlockSpec((tm, d), row),
                  pl.BlockSpec((None, N_MOD, d), lambda i: (jnp.minimum(i * tm // s, b), 0, 0)),
                  pl.BlockSpec((1, d), const),
                  pl.BlockSpec((d, LANES), const)],
        out_specs=[pl.BlockSpec((tm, d), row), pl.BlockSpec((tm, LANES), row),
                   pl.BlockSpec((tm, LANES), row)],
        out_shape=[jax.ShapeDtypeStruct((n_rows, d), BF16),
                   jax.ShapeDtypeStruct((n_rows, LANES), jnp.int32),
                   jax.ShapeDtypeStruct((n_rows, LANES), F32)],
        compiler_params=_params("parallel"),
        name="route",
    )(h, mod, ng, router)


def _moe_kernel(te_ref, nu_ref, x_ref, gw_ref, wg_ref, wu_ref, wd_ref, o_ref):
    i = pl.program_id(0)
    f = pl.program_id(1)

    @pl.when(i < nu_ref[0])
    def _():
        x = x_ref[...]
        hid = (_silu(_dot(x, wg_ref[...])) * _dot(x, wu_ref[...])).astype(BF16)
        part = _dot(hid, wd_ref[...])

        @pl.when(f == 0)
        def _():
            o_ref[...] = part

        @pl.when(f > 0)
        def _():
            o_ref[...] += part

        @pl.when(f == pl.num_programs(1) - 1)
        def _():
            o_ref[...] = o_ref[...] * gw_ref[...]


def _moe_experts(x_sorted, gate_sorted, tile_expert, n_used, wg, wu, wd):
    npad, d = x_sorted.shape
    tm, tf = MOE_TILE, MOE_FF_TILE
    dff = wg.shape[-1]
    grid_spec = pltpu.PrefetchScalarGridSpec(
        num_scalar_prefetch=2,
        grid=(npad // tm, dff // tf),
        in_specs=[pl.BlockSpec((tm, d), lambda i, f, te, nu: (i, 0)),
                  pl.BlockSpec((tm, 1), lambda i, f, te, nu: (i, 0)),
                  pl.BlockSpec((None, d, tf), lambda i, f, te, nu: (te[i], 0, f)),
                  pl.BlockSpec((None, d, tf), lambda i, f, te, nu: (te[i], 0, f)),
                  pl.BlockSpec((None, tf, d), lambda i, f, te, nu: (te[i], f, 0))],
        out_specs=pl.BlockSpec((tm, d), lambda i, f, te, nu: (i, 0)),
    )
    return pl.pallas_call(
        _moe_kernel,
        grid_spec=grid_spec,
        out_shape=jax.ShapeDtypeStruct((npad, d), F32),
        compiler_params=_params("parallel", "arbitrary"),
        name="moe_experts",
    )(tile_expert, n_used, x_sorted, gate_sorted, wg, wu, wd)


def _combine_kernel(h_ref, y0_ref, y1_ref, mod_ref, o_ref):
    o_ref[...] = h_ref[...] + mod_ref[5:6, :] * (y0_ref[...] + y1_ref[...])


def _combine(h, y0, y1, mod, dims, n_rows):
    n, d = h.shape
    b, s, ctx = dims
    tm = ROW_TILE
    row = lambda i: (i, 0)
    return pl.pallas_call(
        _combine_kernel,
        grid=(n_rows // tm,),
        in_specs=[pl.BlockSpec((tm, d), row), pl.BlockSpec((tm, d), row), pl.BlockSpec((tm, d), row),
                  pl.BlockSpec((None, N_MOD, d), lambda i: (jnp.minimum(i * tm // s, b), 0, 0))],
        out_specs=pl.BlockSpec((tm, d), row),
        out_shape=jax.ShapeDtypeStruct((n, d), F32),
        input_output_aliases={0: 0},
        compiler_params=_params("parallel"),
        name="moe_combine",
    )(h, y0, y1, mod)


def _moe(h, mod, ng, router, wg, wu, wd, dims, n_rows):
    tm = MOE_TILE
    v, idx, wt = _route(h, mod, ng, router, dims, n_rows)
    e = idx[:, :2].reshape(-1)
    gate = wt[:, :2].reshape(-1)
    onehot = (e[:, None] == jnp.arange(N_EXPERTS)[None, :]).astype(jnp.int32)
    rank = jnp.take_along_axis(jnp.cumsum(onehot, axis=0), e[:, None], axis=1)[:, 0] - 1
    counts = jnp.sum(onehot, axis=0)
    tiles_per = (counts + tm - 1) // tm
    tile_end = jnp.cumsum(tiles_per)
    group_start = (tile_end - tiles_per) * tm
    slot = group_start[e] + rank
    n_tiles = 2 * n_rows // tm + N_EXPERTS
    npad = n_tiles * tm
    src = jnp.zeros((npad,), jnp.int32).at[slot].set(jnp.arange(2 * n_rows, dtype=jnp.int32) // 2)
    gate_sorted = jnp.zeros((npad,), F32).at[slot].set(gate)
    tile_expert = jnp.minimum(jnp.searchsorted(tile_end, jnp.arange(n_tiles), side="right"),
                              N_EXPERTS - 1).astype(jnp.int32)
    n_used = tile_end[-1:].astype(jnp.int32)
    x_sorted = jnp.take(v, src, axis=0)
    out_sorted = _moe_experts(x_sorted, gate_sorted[:, None], tile_expert, n_used, wg, wu, wd)
    slot2 = slot.reshape(-1, 2)
    y0 = jnp.take(out_sorted, slot2[:, 0], axis=0)
    y1 = jnp.take(out_sorted, slot2[:, 1], axis=0)
    return _combine(h, y0, y1, mod, dims, n_rows)


def _final_kernel(h_ref, g_ref, o_ref):
    o_ref[...] = _rms(h_ref[...], g_ref[...])


def _final_norm(h, g, n_rows):
    n, d = h.shape
    tm = ROW_TILE
    row = lambda i: (i, 0)
    return pl.pallas_call(
        _final_kernel,
        grid=(n_rows // tm,),
        in_specs=[pl.BlockSpec((tm, d), row), pl.BlockSpec((1, d), lambda i: (0, 0))],
        out_specs=pl.BlockSpec((tm, d), row),
        out_shape=jax.ShapeDtypeStruct((n_rows, d), F32),
        compiler_params=_params("parallel"),
        name="final_norm",
    )(h, g)


def _rot_cols(w):
    ws = w.reshape(w.shape[:-1] + (2, 2, QK_ROPE // 4))
    return jnp.stack([-ws[..., 1, :], ws[..., 0, :]], axis=-2).reshape(w.shape)


def _pad_cols(w, width):
    return jnp.pad(w, ((0, 0), (0, width - w.shape[-1])))


def _pack_in(w_in):
    cq = w_in[:, :Q_LORA]
    ckv = w_in[:, Q_LORA:KR_OFF]
    kr = w_in[:, KR_OFF:KR_OFF + QK_ROPE]
    rest = w_in[:, KR_OFF + QK_ROPE:]
    return jnp.concatenate([cq, ckv, _pad_cols(kr, LANES), _pad_cols(_rot_cols(kr), LANES), rest],
                           axis=-1).astype(BF16)


def _pack_q(w_uq):
    w = w_uq.reshape(Q_LORA, N_HEADS, QK_DIM)
    nope, rope = w[..., :QK_NOPE], w[..., QK_NOPE:]
    z = jnp.zeros((Q_LORA, N_HEADS, HEAD_PAD - QK_DIM), w.dtype)
    plain = jnp.concatenate([nope, rope, z], axis=-1).reshape(Q_LORA, QKV_W)
    rot = jnp.concatenate([jnp.zeros_like(nope), _rot_cols(rope), z], axis=-1).reshape(Q_LORA, QKV_W)
    return jnp.concatenate([plain, rot], axis=-1).astype(BF16)


def _pack_kv(w_ukv):
    w = w_ukv.reshape(KV_LORA, N_HEADS, QK_NOPE + V_HEAD)
    z = jnp.zeros((KV_LORA, N_HEADS, HEAD_PAD - QK_NOPE), w.dtype)
    k_top = jnp.concatenate([w[..., :QK_NOPE], z], axis=-1).reshape(KV_LORA, QKV_W)
    v_top = jnp.concatenate([w[..., QK_NOPE:], z], axis=-1).reshape(KV_LORA, QKV_W)
    place = jnp.zeros((LANES, N_HEADS, HEAD_PAD), w.dtype)
    ii = jnp.arange(QK_ROPE)
    place = place.at[ii, :, QK_NOPE + ii].set(1.0).reshape(LANES, QKV_W)
    k_all = jnp.concatenate([k_top, place], axis=0)
    v_all = jnp.concatenate([v_top, jnp.zeros((LANES, QKV_W), w.dtype)], axis=0)
    return jnp.concatenate([k_all, v_all], axis=-1).astype(BF16)


def _rope_tabs(s, tm):
    rows = s // GRID_W
    row = jnp.repeat(jnp.arange(rows, dtype=F32), GRID_W)
    col = jnp.tile(jnp.arange(GRID_W, dtype=F32), rows)
    half = QK_ROPE // 2
    inv = 1.0 / (ROPE_BASE ** (jnp.arange(0, half, 2, dtype=F32) / half))
    ar = row[:, None] * inv[None, :]
    ac = col[:, None] * inv[None, :]
    ang = jnp.concatenate([ar, ar, ac, ac], axis=-1)
    cos = jnp.concatenate([jnp.cos(ang), jnp.ones((tm, QK_ROPE), F32)], axis=0)
    sin = jnp.concatenate([jnp.sin(ang), jnp.zeros((tm, QK_ROPE), F32)], axis=0)
    t = cos.shape[0]
    scale = QK_DIM ** -0.5
    cq = jnp.concatenate([jnp.full((t, QK_NOPE), scale, F32), cos * scale,
                          jnp.zeros((t, HEAD_PAD - QK_DIM), F32)], axis=-1)
    sq = jnp.concatenate([jnp.zeros((t, QK_NOPE), F32), sin * scale,
                          jnp.zeros((t, HEAD_PAD - QK_DIM), F32)], axis=-1)
    ck = _pad_cols(cos, LANES)
    sk = _pad_cols(sin, LANES)
    return cq, sq, ck, sk


def kernel(x, c, ctx, c_ctx, w_mod, b_mod, norm_g, final_g, ab_w_in, ab_g_cq, ab_g_ckv, ab_w_uq, ab_w_ukv, hgrn_lb_logits, hgrn_g_norm, ab_w_out, ffn_w_gate, ffn_w_up, ffn_w_down, pool_w, pool_b, pool_scale, moe_router, moe_w_gate, moe_w_up, moe_w_down):
    b, s, d = x.shape
    ctx_len = ctx.shape[1]
    depth = w_mod.shape[0]
    dims = (b, s, ctx_len)
    n_lat, n_ctx = b * s, b * ctx_len
    assert d == D_MODEL and s % ROW_TILE == 0 and ctx_len % ROW_TILE == 0
    assert ctx_len % GLA_BLOCK == 0 and s % GLA_BLOCK == 0 and n_lat % ctx_len == 0

    h = jnp.concatenate([x.reshape(n_lat, d), ctx.reshape(n_ctx, d)], axis=0)
    n_groups = ((b + 1 + SUBLANES - 1) // SUBLANES) * SUBLANES
    cvec = jnp.concatenate([c, c_ctx[None, :], jnp.zeros((n_groups - b - 1, d), F32)], axis=0)
    mod_all = _mod_table(cvec, w_mod, b_mod).reshape(depth, n_groups, N_MOD, d)

    lb_p = jax.nn.softmax(hgrn_lb_logits.astype(F32), axis=0)
    lb_all = jnp.cumsum(lb_p, axis=0) - lb_p[:1]
    tabs = _rope_tabs(s, ROW_TILE)

    for l in range(depth):
        j = l // 2
        even = l % 2 == 0
        ctx_later = any(m % 2 == 0 for m in range(l + 1, depth))
        n_rows = n_lat + n_ctx if ctx_later else n_lat
        mod = mod_all[l]
        ng1, ng2 = norm_g[l, 0][None, :], norm_g[l, 1][None, :]
        if even:
            (q, k, v, qh, kf, lgf, kb, lgb, vh, hg) = _inproj(
                h, mod, ng1, _pack_in(ab_w_in[j]), ab_g_cq[j][None, :], ab_g_ckv[j][None, :],
                _pack_q(ab_w_uq[j]), _pack_kv(ab_w_ukv[j]), lb_all[j], tabs, dims)
            a = _attention_latent(q, k, v, dims)
            if ctx_later:
                a = _attention_ctx(a, q, k, v, dims)
            o_f, o_b = _gla(qh, kf, lgf, kb, lgb, vh, dims)
            h = _outproj(a, o_f, o_b, hg, hgrn_g_norm[j][None, :], ab_w_out[j].astype(BF16), h, mod,
                         dims, n_rows)
            h = _ffn(h, mod, ng2, ffn_w_gate[j].astype(BF16), ffn_w_up[j].astype(BF16),
                     ffn_w_down[j].astype(BF16), dims, n_rows)
        else:
            h = _pool(h, mod, ng1, pool_w[j].astype(BF16), pool_b[j], pool_scale[j][None, :], dims, n_rows)
            router = _pad_cols(moe_router[j], LANES)
            h = _moe(h, mod, ng2, router, moe_w_gate[j].astype(BF16), moe_w_up[j].astype(BF16),
                     moe_w_down[j].astype(BF16), dims, n_rows)
    return _final_norm(h, final_g[None, :], n_lat).reshape(b, s, d)
```

```python
import functools

import jax
import jax.numpy as jnp
from jax import lax
from jax.experimental import pallas as pl
from jax.experimental.pallas import tpu as pltpu

F32 = jnp.float32
BF16 = jnp.bfloat16

D_MODEL = 1024
GRID_W = 64
N_MOD = 6
EPS = 1e-6
N_HEADS = 8
V_HEAD = 64
QK_NOPE = 64
QK_ROPE = 32
QK_DIM = QK_NOPE + QK_ROPE
Q_LORA = 384
KV_LORA = 256
ROPE_BASE = 10000.0
MLA_W = 512
HGRN_W = 512
HGRN_K = 128
N_HGRN_HEADS = 4
POOL_WINDOWS = (2, 4, 8, 16)
POOL_C = 256
N_EXPERTS = 8

LANES = 128
SUBLANES = 8
VMEM_LIMIT_BYTES = 56 * 1024 * 1024

HEAD_PAD = LANES
QKV_W = N_HEADS * HEAD_PAD
KR_OFF = Q_LORA + KV_LORA
KRR_OFF = KR_OFF + LANES
H_OFF = KRR_OFF + LANES
IN_COLS = H_OFF + 5 * HGRN_W
GLA_CHUNK = 64
GLA_BLOCK = 256
ROW_TILE = 256
MOE_TILE = 1024
MOE_FF_TILE = 512
EXP_CAP = 80.0


def _params(*sem):
    return pltpu.CompilerParams(dimension_semantics=sem, vmem_limit_bytes=VMEM_LIMIT_BYTES)


def _dot(a, b):
    return jnp.dot(a, b, preferred_element_type=F32)


def _dot_nt(a, b):
    return lax.dot_general(a, b, (((1,), (1,)), ((), ())), preferred_element_type=F32)


def _dot_tn(a, b):
    return lax.dot_general(a, b, (((0,), (0,)), ((), ())), preferred_element_type=F32)


def _rms(x, g):
    return x * lax.rsqrt(jnp.mean(x * x, axis=-1, keepdims=True) + EPS) * g


def _silu(x):
    return x * jax.nn.sigmoid(x)


def _norm_mod(h, g, shift, scale):
    return _rms(h, g) * (1.0 + scale) + shift


def _mod_kernel(c_ref, w_ref, b_ref, o_ref):
    c = c_ref[...]
    o_ref[...] = jnp.dot(_silu(c), w_ref[...], preferred_element_type=F32,
                         precision=lax.Precision.HIGHEST) + b_ref[...]


def _mod_table(cvec, w_mod, b_mod):
    depth, d, nd = w_mod.shape
    g = cvec.shape[0]
    tn = 1536
    return pl.pallas_call(
        _mod_kernel,
        grid=(depth, nd // tn),
        in_specs=[pl.BlockSpec((g, d), lambda l, j: (0, 0)),
                  pl.BlockSpec((None, d, tn), lambda l, j: (l, 0, j)),
                  pl.BlockSpec((None, 1, tn), lambda l, j: (l, 0, j))],
        out_specs=pl.BlockSpec((None, g, tn), lambda l, j: (l, 0, j)),
        out_shape=jax.ShapeDtypeStruct((depth, g, nd), F32),
        compiler_params=_params("parallel", "parallel"),
        name="mod_table",
    )(cvec, w_mod, b_mod.reshape(depth, 1, nd))


def _inproj_kernel(h_ref, mod_ref, ng_ref, win_ref, gcq_ref, gckv_ref, wq_ref, wkv_ref,
                   lb_ref, cq_tab, sq_tab, ck_tab, sk_tab,
                   q_ref, k_ref, v_ref, qh_ref, kf_ref, lgf_ref, kb_ref, lgb_ref, vh_ref, hg_ref):
    u = _norm_mod(h_ref[...], ng_ref[...], mod_ref[0:1, :], mod_ref[1:2, :]).astype(BF16)
    p = _dot(u, win_ref[...])
    cqn = _rms(p[:, 0:Q_LORA], gcq_ref[...]).astype(BF16)
    a = _dot(cqn, wq_ref[...])
    cq = jnp.tile(cq_tab[...], (1, N_HEADS))
    sq = jnp.tile(sq_tab[...], (1, N_HEADS))
    q_ref[...] = (a[:, :QKV_W] * cq + a[:, QKV_W:] * sq).astype(BF16)
    ckvn = _rms(p[:, Q_LORA:KR_OFF], gckv_ref[...]).astype(BF16)
    kr = (p[:, KR_OFF:KRR_OFF] * ck_tab[...] + p[:, KRR_OFF:H_OFF] * sk_tab[...]).astype(BF16)
    kv = _dot(jnp.concatenate([ckvn, kr], axis=-1), wkv_ref[...])
    k_ref[...] = kv[:, :QKV_W].astype(BF16)
    lane = lax.broadcasted_iota(jnp.int32, (1, QKV_W), 1)
    ones_col = (lane % HEAD_PAD == V_HEAD).astype(F32)
    v_ref[...] = (kv[:, QKV_W:] + ones_col).astype(BF16)
    o = H_OFF
    qh_ref[...] = _silu(p[:, o:o + HGRN_W])
    kf = (1.0 - lb_ref[0:1, :]) * jax.nn.sigmoid(-p[:, o + HGRN_W:o + 2 * HGRN_W])
    kf_ref[...] = kf
    lgf_ref[...] = jnp.log1p(-kf)
    kb = (1.0 - lb_ref[1:2, :]) * jax.nn.sigmoid(-p[:, o + 2 * HGRN_W:o + 3 * HGRN_W])
    kb_ref[...] = kb
    lgb_ref[...] = jnp.log1p(-kb)
    vh_ref[...] = p[:, o + 3 * HGRN_W:o + 4 * HGRN_W]
    hg_ref[...] = p[:, o + 4 * HGRN_W:o + 5 * HGRN_W]


def _inproj(h, mod, ng, win, gcq, gckv, wq, wkv, lb, tabs, dims):
    n, d = h.shape
    b, s, ctx = dims
    tm = ROW_TILE
    n_lat_t = b * s // tm
    pos_blocks = s // tm

    def grp(i):
        return jnp.minimum(i * tm // s, b)

    def pos(i):
        return jnp.where(i < n_lat_t, i % pos_blocks, pos_blocks)

    row = lambda i: (i, 0)
    const = lambda i: (0, 0)
    tab_spec = pl.BlockSpec((tm, LANES), lambda i: (pos(i), 0))
    wide = jax.ShapeDtypeStruct((n, QKV_W), BF16)
    hg = jax.ShapeDtypeStruct((n, HGRN_W), F32)
    return pl.pallas_call(
        _inproj_kernel,
        grid=(n // tm,),
        in_specs=[pl.BlockSpec((tm, d), row),
                  pl.BlockSpec((None, N_MOD, d), lambda i: (grp(i), 0, 0)),
                  pl.BlockSpec((1, d), const),
                  pl.BlockSpec(win.shape, const),
                  pl.BlockSpec((1, Q_LORA), const),
                  pl.BlockSpec((1, KV_LORA), const),
                  pl.BlockSpec(wq.shape, const),
                  pl.BlockSpec(wkv.shape, const),
                  pl.BlockSpec((2, HGRN_W), const),
                  tab_spec, tab_spec, tab_spec, tab_spec],
        out_specs=[pl.BlockSpec((tm, QKV_W), row)] * 3 + [pl.BlockSpec((tm, HGRN_W), row)] * 7,
        out_shape=[wide] * 3 + [hg] * 7,
        compiler_params=_params("parallel"),
        name="inproj",
    )(h, mod, ng, win, gcq, gckv, wq, wkv, lb, *tabs)


def _attn_kernel(*refs, n_kv):
    q_ref = refs[0]
    k_refs = refs[1:1 + 2 * n_kv:2]
    v_refs = refs[2:2 + 2 * n_kv:2]
    o_ref = refs[-1]
    for hd in range(N_HEADS):
        sl = slice(hd * HEAD_PAD, (hd + 1) * HEAD_PAD)
        q = q_ref[:, sl]
        scores = [_dot_nt(q, k_ref[:, sl]) for k_ref in k_refs]
        m = functools.reduce(jnp.maximum, [jnp.max(sc, axis=-1, keepdims=True) for sc in scores])
        acc = None
        for sc, v_ref in zip(scores, v_refs):
            part = _dot(jnp.exp(sc - m).astype(BF16), v_ref[:, sl])
            acc = part if acc is None else acc + part
        o = acc[:, :V_HEAD] / acc[:, V_HEAD:V_HEAD + 1]
        o_ref[:, hd * V_HEAD:(hd + 1) * V_HEAD] = o.astype(o_ref.dtype)


def _attention_latent(q, k, v, dims):
    n = q.shape[0]
    b, s, ctx = dims
    tq = ROW_TILE
    nq = s // tq
    lat = lambda bi, j: (bi, 0)
    cx = lambda bi, j: (b * s // ctx + bi, 0)
    return pl.pallas_call(
        functools.partial(_attn_kernel, n_kv=2),
        grid=(b, nq),
        in_specs=[pl.BlockSpec((tq, QKV_W), lambda bi, j: (bi * nq + j, 0)),
                  pl.BlockSpec((s, QKV_W), lat), pl.BlockSpec((s, QKV_W), lat),
                  pl.BlockSpec((ctx, QKV_W), cx), pl.BlockSpec((ctx, QKV_W), cx)],
        out_specs=pl.BlockSpec((tq, MLA_W), lambda bi, j: (bi * nq + j, 0)),
        out_shape=jax.ShapeDtypeStruct((n, MLA_W), BF16),
        compiler_params=_params("parallel", "arbitrary"),
        name="attn_latent",
    )(q, k, v, k, v)


def _attention_ctx(a_lat, q, k, v, dims):
    b, s, ctx = dims
    cx = lambda bi: (b * s // ctx + bi, 0)
    return pl.pallas_call(
        functools.partial(_attn_kernel, n_kv=1),
        grid=(b,),
        in_specs=[pl.BlockSpec((ctx, QKV_W), cx), pl.BlockSpec((ctx, QKV_W), cx),
                  pl.BlockSpec((ctx, QKV_W), cx), pl.BlockSpec(memory_space=pl.ANY)],
        out_specs=pl.BlockSpec((ctx, MLA_W), cx),
        out_shape=jax.ShapeDtypeStruct(a_lat.shape, a_lat.dtype),
        input_output_aliases={3: 0},
        compiler_params=_params("parallel"),
        name="attn_ctx",
    )(q, k, v, a_lat)


def _bcast_block_row(g, bs, row):
    if bs == GLA_CHUNK:
        return g[row:row + 1, :]
    g3 = g.reshape(GLA_CHUNK // bs, bs, g.shape[-1])
    return jnp.broadcast_to(g3[:, row:row + 1, :], g3.shape).reshape(g.shape)


def _gla_direction(q_ref, k_ref, lg_ref, v_ref, o_ref, st_ref, reverse):
    c = GLA_CHUNK
    t_idx = lax.broadcasted_iota(jnp.int32, (c, c), 0)
    s_idx = lax.broadcasted_iota(jnp.int32, (c, c), 1)
    pt = (c - 1 - t_idx) if reverse else t_idx
    ps = (c - 1 - s_idx) if reverse else s_idx
    tri = (ps <= pt).astype(BF16)
    levels = []
    for half in (32, 16, 8):
        bs = 2 * half
        mask = (t_idx // bs == s_idx // bs) & (pt % bs >= half) & (ps % bs < half)
        levels.append((bs, half if reverse else half - 1, 0.0, mask))
    dmask = (t_idx // SUBLANES == s_idx // SUBLANES) & (ps <= pt)
    levels.append((SUBLANES, 4 if reverse else 3, EXP_CAP, dmask))
    end_row = 0 if reverse else c - 1

    n_chunks = GLA_BLOCK // c
    order = range(n_chunks - 1, -1, -1) if reverse else range(n_chunks)
    states = [st_ref[hd] for hd in range(N_HGRN_HEADS)]
    for ci in order:
        rows = slice(ci * c, (ci + 1) * c)
        lg = lg_ref[rows, :]
        hi = lg.astype(BF16)
        r1 = lg - hi.astype(F32)
        mid = r1.astype(BF16)
        lo = (r1 - mid.astype(F32)).astype(BF16)
        gc = _dot(tri, hi) + _dot(tri, mid) + _dot(tri, lo)
        for hd in range(N_HGRN_HEADS):
            hs = slice(hd * HGRN_K, (hd + 1) * HGRN_K)
            g = gc[:, hs]
            qh = q_ref[rows, hs]
            kh = k_ref[rows, hs]
            vh = v_ref[rows, hs].astype(BF16)
            st = states[hd]
            g_end = g[end_row:end_row + 1, :]
            attn = jnp.zeros((c, c), F32)
            for bs, row, cap, mask in levels:
                ref = _bcast_block_row(g, bs, row)
                qf = (qh * jnp.exp(jnp.minimum(g - ref, cap))).astype(BF16)
                kf = (kh * jnp.exp(jnp.minimum(ref - g, cap))).astype(BF16)
                attn = jnp.where(mask, _dot_nt(qf, kf), attn)
            qe = (qh * jnp.exp(g)).astype(BF16)
            ke = (kh * jnp.exp(g_end - g)).astype(BF16)
            o = _dot_nt(qe, st.astype(BF16)) + _dot(attn.astype(BF16), vh)
            o_ref[rows, hs] = o
            states[hd] = st * jnp.exp(g_end) + _dot_tn(vh, ke)
    for hd in range(N_HGRN_HEADS):
        st_ref[hd] = states[hd]


def _gla_kernel(qf_ref, kf_ref, lgf_ref, vf_ref, qb_ref, kb_ref, lgb_ref, vb_ref,
                of_ref, ob_ref, stf_ref, stb_ref):
    @pl.when(pl.program_id(1) == 0)
    def _():
        stf_ref[...] = jnp.zeros_like(stf_ref)
        stb_ref[...] = jnp.zeros_like(stb_ref)

    _gla_direction(qf_ref, kf_ref, lgf_ref, vf_ref, of_ref, stf_ref, False)
    _gla_direction(qb_ref, kb_ref, lgb_ref, vb_ref, ob_ref, stb_ref, True)


def _gla(qh, kf, lgf, kb, lgb, vh, dims):
    n = qh.shape[0]
    b, s, ctx = dims
    blk = GLA_BLOCK
    ncb, nsb, nlat = ctx // blk, s // blk, b * s // blk

    def fwd(bi, j):
        return (jnp.where(j < ncb, nlat + bi * ncb + j, bi * nsb + (j - ncb)), 0)

    def bwd(bi, j):
        return (jnp.where(j < ncb, nlat + bi * ncb + (ncb - 1 - j), bi * nsb + (nsb - 1 - (j - ncb))), 0)

    fs = pl.BlockSpec((blk, HGRN_W), fwd)
    bs = pl.BlockSpec((blk, HGRN_W), bwd)
    out = jax.ShapeDtypeStruct((n, HGRN_W), F32)
    return pl.pallas_call(
        _gla_kernel,
        grid=(b, ncb + nsb),
        in_specs=[fs, fs, fs, fs, bs, bs, bs, bs],
        out_specs=[fs, bs],
        out_shape=[out, out],
        scratch_shapes=[pltpu.VMEM((N_HGRN_HEADS, HGRN_K, HGRN_K), F32)] * 2,
        compiler_params=_params("parallel", "arbitrary"),
        name="gla",
    )(qh, kf, lgf, vh, qh, kb, lgb, vh)


def _outproj_kernel(a_ref, of_ref, ob_ref, hg_ref, gn_ref, w_ref, h_ref, mod_ref, o_ref):
    o = of_ref[...] + ob_ref[...]
    gate = _silu(hg_ref[...])
    ys = []
    for hd in range(N_HGRN_HEADS):
        hs = slice(hd * HGRN_K, (hd + 1) * HGRN_K)
        ys.append(_rms(o[:, hs], gn_ref[...]) * gate[:, hs])
    y = jnp.concatenate(ys, axis=-1).astype(BF16)
    mix = _dot(a_ref[...], w_ref[0:MLA_W, :]) + _dot(y, w_ref[MLA_W:, :])
    o_ref[...] = h_ref[...] + mod_ref[2:3, :] * mix


def _outproj(a, of, ob, hg, gn, w, h, mod, dims, n_rows):
    n, d = h.shape
    b, s, ctx = dims
    tm = ROW_TILE
    row = lambda i: (i, 0)
    const = lambda i: (0, 0)
    half = pl.BlockSpec((tm, HGRN_W), row)
    return pl.pallas_call(
        _outproj_kernel,
        grid=(n_rows // tm,),
        in_specs=[pl.BlockSpec((tm, MLA_W), row), half, half, half,
                  pl.BlockSpec((1, HGRN_K), const), pl.BlockSpec(w.shape, const),
                  pl.BlockSpec((tm, d), row),
                  pl.BlockSpec((None, N_MOD, d), lambda i: (jnp.minimum(i * tm // s, b), 0, 0))],
        out_specs=pl.BlockSpec((tm, d), row),
        out_shape=jax.ShapeDtypeStruct((n, d), F32),
        input_output_aliases={6: 0},
        compiler_params=_params("parallel"),
        name="outproj",
    )(a, of, ob, hg, gn, w, h, mod)


def _ffn_kernel(h_ref, mod_ref, ng_ref, wg_ref, wu_ref, wd_ref, o_ref):
    h = h_ref[...]
    v = _norm_mod(h, ng_ref[...], mod_ref[3:4, :], mod_ref[4:5, :]).astype(BF16)
    hid = (_silu(_dot(v, wg_ref[...])) * _dot(v, wu_ref[...])).astype(BF16)
    o_ref[...] = h + mod_ref[5:6, :] * _dot(hid, wd_ref[...])


def _ffn(h, mod, ng, wg, wu, wd, dims, n_rows):
    n, d = h.shape
    b, s, ctx = dims
    tm = ROW_TILE
    row = lambda i: (i, 0)
    const = lambda i: (0, 0)
    return pl.pallas_call(
        _ffn_kernel,
        grid=(n_rows // tm,),
        in_specs=[pl.BlockSpec((tm, d), row),
                  pl.BlockSpec((None, N_MOD, d), lambda i: (jnp.minimum(i * tm // s, b), 0, 0)),
                  pl.BlockSpec((1, d), const),
                  pl.BlockSpec(wg.shape, const, pipeline_mode=pl.Buffered(1)),
                  pl.BlockSpec(wu.shape, const, pipeline_mode=pl.Buffered(1)),
                  pl.BlockSpec(wd.shape, const, pipeline_mode=pl.Buffered(1))],
        out_specs=pl.BlockSpec((tm, d), row),
        out_shape=jax.ShapeDtypeStruct((n, d), F32),
        input_output_aliases={0: 0},
        compiler_params=_params("parallel"),
        name="ffn",
    )(h, mod, ng, wg, wu, wd)


def _pool_kernel(h_ref, prev_ref, next_ref, mod_ref, ng_ref, w_ref, b_ref, sc_ref, o_ref, ext_ref,
                 *, tm, seq_tiles_lat, n_lat_tiles, seq_tiles_ctx):
    i = pl.program_id(0)
    halo = SUBLANES
    shift, scale = mod_ref[0:1, :], mod_ref[1:2, :]
    in_lat = i < n_lat_tiles
    j = jnp.where(in_lat, i % seq_tiles_lat, (i - n_lat_tiles) % seq_tiles_ctx)
    n_seq_tiles = jnp.where(in_lat, seq_tiles_lat, seq_tiles_ctx)
    h = h_ref[...]
    u = _norm_mod(h, ng_ref[...], shift, scale)
    up = _norm_mod(prev_ref[...], ng_ref[...], shift, scale)
    un = _norm_mod(next_ref[...], ng_ref[...], shift, scale)
    ext_ref[0:halo, :] = jnp.where(j > 0, up, 0.0)
    ext_ref[halo:halo + tm, :] = u
    ext_ref[halo + tm:, :] = jnp.where(j < n_seq_tiles - 1, un, 0.0)
    pos = j * tm + lax.broadcasted_iota(jnp.int32, (tm, 1), 0)
    t_len = n_seq_tiles * tm
    outs = []
    for g, win in enumerate(POOL_WINDOWS):
        cs = slice(g * POOL_C, (g + 1) * POOL_C)
        acc = None
        for dlt in range(-(win // 2), win // 2):
            part = ext_ref[halo + dlt:halo + dlt + tm, cs]
            acc = part if acc is None else acc + part
        lo = jnp.maximum(pos - win // 2, 0)
        hi = jnp.minimum(pos - win // 2 + win, t_len)
        cnt = (hi - lo).astype(F32)
        pooled = (acc / cnt - u[:, cs]).astype(BF16)
        outs.append(_dot(pooled, w_ref[g]) + b_ref[g:g + 1, :])
    y = jnp.concatenate(outs, axis=-1) * sc_ref[...]
    o_ref[...] = h + mod_ref[2:3, :] * y


def _pool(h, mod, ng, w, bias, scale, dims, n_rows):
    n, d = h.shape
    b, s, ctx = dims
    tm = ROW_TILE
    hb = tm // SUBLANES
    n_blk8 = n // SUBLANES
    row = lambda i: (i, 0)
    const = lambda i: (0, 0)
    kern = functools.partial(_pool_kernel, tm=tm, seq_tiles_lat=s // tm, n_lat_tiles=b * s // tm,
                             seq_tiles_ctx=ctx // tm)
    return pl.pallas_call(
        kern,
        grid=(n_rows // tm,),
        in_specs=[pl.BlockSpec((tm, d), row),
                  pl.BlockSpec((SUBLANES, d), lambda i: (jnp.maximum(i * hb - 1, 0), 0)),
                  pl.BlockSpec((SUBLANES, d), lambda i: (jnp.minimum((i + 1) * hb, n_blk8 - 1), 0)),
                  pl.BlockSpec((None, N_MOD, d), lambda i: (jnp.minimum(i * tm // s, b), 0, 0)),
                  pl.BlockSpec((1, d), const),
                  pl.BlockSpec(w.shape, lambda i: (0, 0, 0)),
                  pl.BlockSpec(bias.shape, const),
                  pl.BlockSpec((1, d), const)],
        out_specs=pl.BlockSpec((tm, d), row),
        out_shape=jax.ShapeDtypeStruct((n, d), F32),
        scratch_shapes=[pltpu.VMEM((tm + 2 * SUBLANES, d), F32)],
        compiler_params=_params("parallel"),
        name="pool",
    )(h, h, h, mod, ng, w, bias, scale)


def _route_kernel(h_ref, mod_ref, ng_ref, r_ref, v_ref, idx_ref, wt_ref, lst_ref, base_ref, run_ref):
    tm = h_ref.shape[0]

    @pl.when(pl.program_id(0) == 0)
    def _():
        run_ref[...] = jnp.zeros_like(run_ref)

    v = _norm_mod(h_ref[...], ng_ref[...], mod_ref[3:4, :], mod_ref[4:5, :])
    v_ref[...] = v.astype(BF16)
    logits = jnp.dot(v, r_ref[...], preferred_element_type=F32, precision=lax.Precision.HIGHEST)
    lane = lax.broadcasted_iota(jnp.int32, logits.shape, 1)
    neg = jnp.float32(-jnp.inf)
    logits = jnp.where(lane < N_EXPERTS, logits, neg)
    m1 = jnp.max(logits, axis=-1, keepdims=True)
    i1 = jnp.min(jnp.where(logits == m1, lane, LANES), axis=-1, keepdims=True)
    rest = jnp.where(lane == i1, neg, logits)
    m2 = jnp.max(rest, axis=-1, keepdims=True)
    i2 = jnp.min(jnp.where(rest == m2, lane, LANES), axis=-1, keepdims=True)
    e2 = jnp.exp(m2 - m1)
    w1 = 1.0 / (1.0 + e2)
    w2 = e2 / (1.0 + e2)
    wt_ref[...] = jnp.where(lane == 0, w1, jnp.where(lane == 1, w2, 0.0))
    chose = (lane == i1) | (lane == i2)
    oh = chose.astype(BF16)
    ti = lax.broadcasted_iota(jnp.int32, (tm, tm), 0)
    tj = lax.broadcasted_iota(jnp.int32, (tm, tm), 1)
    before = _dot((tj < ti).astype(BF16), oh)
    run = run_ref[...]
    pos = run + before
    r1 = jnp.sum(jnp.where(lane == i1, pos, 0.0), axis=-1, keepdims=True)
    r2 = jnp.sum(jnp.where(lane == i2, pos, 0.0), axis=-1, keepdims=True)
    idx_ref[...] = jnp.where(lane == 0, i1, jnp.where(lane == 1, i2, jnp.where(
        lane == 2, r1.astype(jnp.int32), jnp.where(lane == 3, r2.astype(jnp.int32), 0))))
    base_ref[...] = run.astype(jnp.int32)
    run_ref[...] = run + jnp.sum(oh.astype(F32), axis=0, keepdims=True)
    before_t = _dot_tn(oh, (ti < tj).astype(BF16))
    chose_t = _dot_tn(oh, (ti == tj).astype(BF16))
    ids = lax.broadcasted_iota(jnp.int32, (tm, LANES), 0).astype(BF16)
    lst = jnp.zeros((tm, LANES), F32)
    for e in range(N_EXPERTS):
        sel = (before_t[e:e + 1, :] == ti.astype(F32)) & (chose_t[e:e + 1, :] > 0.5)
        lst = jnp.where(lane == e, _dot(sel.astype(BF16), ids), lst)
    lst_ref[...] = lst.astype(jnp.int32)


def _route(h, mod, ng, router, dims, n_rows):
    n, d = h.shape
    b, s, ctx = dims
    tm = ROW_TILE
    assert tm <= 256
    nt = n_rows // tm
    row = lambda i: (i, 0)
    const = lambda i: (0, 0)
    return pl.pallas_call(
        _route_kernel,
        grid=(nt,),
        in_specs=[pl.BlockSpec((tm, d), row),
                  pl.BlockSpec((None, N_MOD, d), lambda i: (jnp.minimum(i * tm // s, b), 0, 0)),
                  pl.BlockSpec((1, d), const),
                  pl.BlockSpec((d, LANES), const)],
        out_specs=[pl.BlockSpec((tm, d), row), pl.BlockSpec((tm, LANES), row),
                   pl.BlockSpec((tm, LANES), row), pl.BlockSpec((tm, LANES), row),
                   pl.BlockSpec((None, 1, LANES), lambda i: (i, 0, 0))],
        out_shape=[jax.ShapeDtypeStruct((n_rows, d), BF16),
                   jax.ShapeDtypeStruct((n_rows, LANES), jnp.int32),
                   jax.ShapeDtypeStruct((n_rows, LANES), F32),
                   jax.ShapeDtypeStruct((n_rows, LANES), jnp.int32),
                   jax.ShapeDtypeStruct((nt, 1, LANES), jnp.int32)],
        scratch_shapes=[pltpu.VMEM((1, LANES), F32)],
        compiler_params=_params("arbitrary"),
        name="route",
    )(h, mod, ng, router)


def _moe_kernel(te_ref, nu_ref, x_ref, wg_ref, wu_ref, wd_ref, o_ref):
    i = pl.program_id(0)
    f = pl.program_id(1)

    @pl.when(i < nu_ref[0])
    def _():
        x = x_ref[...]
        hid = (_silu(_dot(x, wg_ref[...])) * _dot(x, wu_ref[...])).astype(BF16)
        part = _dot(hid, wd_ref[...])

        @pl.when(f == 0)
        def _():
            o_ref[...] = part

        @pl.when(f > 0)
        def _():
            o_ref[...] += part


def _moe_experts(x_sorted, tile_expert, n_used, wg, wu, wd):
    npad, d = x_sorted.shape
    tm, tf = MOE_TILE, MOE_FF_TILE
    dff = wg.shape[-1]
    grid_spec = pltpu.PrefetchScalarGridSpec(
        num_scalar_prefetch=2,
        grid=(npad // tm, dff // tf),
        in_specs=[pl.BlockSpec((tm, d), lambda i, f, te, nu: (i, 0)),
                  pl.BlockSpec((None, d, tf), lambda i, f, te, nu: (te[i], 0, f)),
                  pl.BlockSpec((None, d, tf), lambda i, f, te, nu: (te[i], 0, f)),
                  pl.BlockSpec((None, tf, d), lambda i, f, te, nu: (te[i], f, 0))],
        out_specs=pl.BlockSpec((tm, d), lambda i, f, te, nu: (i, 0)),
    )
    return pl.pallas_call(
        _moe_kernel,
        grid_spec=grid_spec,
        out_shape=jax.ShapeDtypeStruct((npad, d), F32),
        compiler_params=_params("parallel", "arbitrary"),
        name="moe_experts",
    )(tile_expert, n_used, x_sorted, wg, wu, wd)


def _combine_kernel(h_ref, y0_ref, y1_ref, wt_ref, mod_ref, o_ref):
    y = wt_ref[:, 0:1] * y0_ref[...] + wt_ref[:, 1:2] * y1_ref[...]
    o_ref[...] = h_ref[...] + mod_ref[5:6, :] * y


def _combine(h, y0, y1, wt, mod, dims, n_rows):
    n, d = h.shape
    b, s, ctx = dims
    tm = ROW_TILE
    row = lambda i: (i, 0)
    return pl.pallas_call(
        _combine_kernel,
        grid=(n_rows // tm,),
        in_specs=[pl.BlockSpec((tm, d), row), pl.BlockSpec((tm, d), row), pl.BlockSpec((tm, d), row),
                  pl.BlockSpec((tm, LANES), row),
                  pl.BlockSpec((None, N_MOD, d), lambda i: (jnp.minimum(i * tm // s, b), 0, 0))],
        out_specs=pl.BlockSpec((tm, d), row),
        out_shape=jax.ShapeDtypeStruct((n, d), F32),
        input_output_aliases={0: 0},
        compiler_params=_params("parallel"),
        name="moe_combine",
    )(h, y0, y1, wt, mod)


def _moe(h, mod, ng, router, wg, wu, wd, dims, n_rows):
    tm, rt = MOE_TILE, ROW_TILE
    v, idx, wt, lst, base = _route(h, mod, ng, router, dims, n_rows)
    base = base[:, 0, :N_EXPERTS]
    e_sel, pos = idx[:, 0:2], idx[:, 2:4]
    last = e_sel[-rt:]
    counts = base[-1] + jnp.sum(last[:, :, None] == jnp.arange(N_EXPERTS), axis=(0, 1))
    tiles_per = (counts + tm - 1) // tm
    tile_end = jnp.cumsum(tiles_per)
    group_start = (tile_end - tiles_per) * tm
    slot = group_start[e_sel] + pos
    n_tiles = 2 * n_rows // tm + N_EXPERTS
    tile_expert = jnp.minimum(jnp.sum(jnp.arange(n_tiles)[:, None] >= tile_end[None, :], axis=1),
                              N_EXPERTS - 1).astype(jnp.int32)
    n_used = tile_end[-1:].astype(jnp.int32)
    rank = (jnp.arange(n_tiles * tm, dtype=jnp.int32).reshape(n_tiles, tm)
            - group_start[tile_expert][:, None])
    base_te = base.T[tile_expert]
    rtile = jnp.sum(base_te[:, None, :] <= rank[:, :, None], axis=-1) - 1
    local = rank - jnp.take_along_axis(base_te, rtile, axis=1)
    flat = jnp.clip((rtile * rt + local) * N_EXPERTS + tile_expert[:, None], 0, n_rows * N_EXPERTS - 1)
    src = rtile * rt + jnp.take(lst[:, :N_EXPERTS].reshape(-1), flat)
    src = jnp.where(rank < counts[tile_expert][:, None], src, 0).reshape(-1)
    x_sorted = jnp.take(v, src, axis=0)
    out_sorted = _moe_experts(x_sorted, tile_expert, n_used, wg, wu, wd)
    y0 = jnp.take(out_sorted, slot[:, 0], axis=0)
    y1 = jnp.take(out_sorted, slot[:, 1], axis=0)
    return _combine(h, y0, y1, wt, mod, dims, n_rows)


def _final_kernel(h_ref, g_ref, o_ref):
    o_ref[...] = _rms(h_ref[...], g_ref[...])


def _final_norm(h, g, n_rows):
    n, d = h.shape
    tm = ROW_TILE
    row = lambda i: (i, 0)
    return pl.pallas_call(
        _final_kernel,
        grid=(n_rows // tm,),
        in_specs=[pl.BlockSpec((tm, d), row), pl.BlockSpec((1, d), lambda i: (0, 0))],
        out_specs=pl.BlockSpec((tm, d), row),
        out_shape=jax.ShapeDtypeStruct((n_rows, d), F32),
        compiler_params=_params("parallel"),
        name="final_norm",
    )(h, g)


def _rot_cols(w):
    ws = w.reshape(w.shape[:-1] + (2, 2, QK_ROPE // 4))
    return jnp.stack([-ws[..., 1, :], ws[..., 0, :]], axis=-2).reshape(w.shape)


def _pad_cols(w, width):
    return jnp.pad(w, ((0, 0), (0, width - w.shape[-1])))


def _pack_in(w_in):
    cq = w_in[:, :Q_LORA]
    ckv = w_in[:, Q_LORA:KR_OFF]
    kr = w_in[:, KR_OFF:KR_OFF + QK_ROPE]
    rest = w_in[:, KR_OFF + QK_ROPE:]
    return jnp.concatenate([cq, ckv, _pad_cols(kr, LANES), _pad_cols(_rot_cols(kr), LANES), rest],
                           axis=-1).astype(BF16)


def _pack_q(w_uq):
    w = w_uq.reshape(Q_LORA, N_HEADS, QK_DIM)
    nope, rope = w[..., :QK_NOPE], w[..., QK_NOPE:]
    z = jnp.zeros((Q_LORA, N_HEADS, HEAD_PAD - QK_DIM), w.dtype)
    plain = jnp.concatenate([nope, rope, z], axis=-1).reshape(Q_LORA, QKV_W)
    rot = jnp.concatenate([jnp.zeros_like(nope), _rot_cols(rope), z], axis=-1).reshape(Q_LORA, QKV_W)
    return jnp.concatenate([plain, rot], axis=-1).astype(BF16)


def _pack_kv(w_ukv):
    w = w_ukv.reshape(KV_LORA, N_HEADS, QK_NOPE + V_HEAD)
    z = jnp.zeros((KV_LORA, N_HEADS, HEAD_PAD - QK_NOPE), w.dtype)
    k_top = jnp.concatenate([w[..., :QK_NOPE], z], axis=-1).reshape(KV_LORA, QKV_W)
    v_top = jnp.concatenate([w[..., QK_NOPE:], z], axis=-1).reshape(KV_LORA, QKV_W)
    place = jnp.zeros((LANES, N_HEADS, HEAD_PAD), w.dtype)
    ii = jnp.arange(QK_ROPE)
    place = place.at[ii, :, QK_NOPE + ii].set(1.0).reshape(LANES, QKV_W)
    k_all = jnp.concatenate([k_top, place], axis=0)
    v_all = jnp.concatenate([v_top, jnp.zeros((LANES, QKV_W), w.dtype)], axis=0)
    return jnp.concatenate([k_all, v_all], axis=-1).astype(BF16)


def _rope_tabs(s, tm):
    rows = s // GRID_W
    row = jnp.repeat(jnp.arange(rows, dtype=F32), GRID_W)
    col = jnp.tile(jnp.arange(GRID_W, dtype=F32), rows)
    half = QK_ROPE // 2
    inv = 1.0 / (ROPE_BASE ** (jnp.arange(0, half, 2, dtype=F32) / half))
    ar = row[:, None] * inv[None, :]
    ac = col[:, None] * inv[None, :]
    ang = jnp.concatenate([ar, ar, ac, ac], axis=-1)
    cos = jnp.concatenate([jnp.cos(ang), jnp.ones((tm, QK_ROPE), F32)], axis=0)
    sin = jnp.concatenate([jnp.sin(ang), jnp.zeros((tm, QK_ROPE), F32)], axis=0)
    t = cos.shape[0]
    scale = QK_DIM ** -0.5
    cq = jnp.concatenate([jnp.full((t, QK_NOPE), scale, F32), cos * scale,
                          jnp.zeros((t, HEAD_PAD - QK_DIM), F32)], axis=-1)
    sq = jnp.concatenate([jnp.zeros((t, QK_NOPE), F32), sin * scale,
                          jnp.zeros((t, HEAD_PAD - QK_DIM), F32)], axis=-1)
    ck = _pad_cols(cos, LANES)
    sk = _pad_cols(sin, LANES)
    return cq, sq, ck, sk


def kernel(x, c, ctx, c_ctx, w_mod, b_mod, norm_g, final_g, ab_w_in, ab_g_cq, ab_g_ckv, ab_w_uq, ab_w_ukv, hgrn_lb_logits, hgrn_g_norm, ab_w_out, ffn_w_gate, ffn_w_up, ffn_w_down, pool_w, pool_b, pool_scale, moe_router, moe_w_gate, moe_w_up, moe_w_down):
    b, s, d = x.shape
    ctx_len = ctx.shape[1]
    depth = w_mod.shape[0]
    dims = (b, s, ctx_len)
    n_lat, n_ctx = b * s, b * ctx_len
    assert d == D_MODEL and s % ROW_TILE == 0 and ctx_len % ROW_TILE == 0
    assert ctx_len % GLA_BLOCK == 0 and s % GLA_BLOCK == 0 and n_lat % ctx_len == 0

    h = jnp.concatenate([x.reshape(n_lat, d), ctx.reshape(n_ctx, d)], axis=0)
    n_groups = ((b + 1 + SUBLANES - 1) // SUBLANES) * SUBLANES
    cvec = jnp.concatenate([c, c_ctx[None, :], jnp.zeros((n_groups - b - 1, d), F32)], axis=0)
    mod_all = _mod_table(cvec, w_mod, b_mod).reshape(depth, n_groups, N_MOD, d)

    lb_p = jax.nn.softmax(hgrn_lb_logits.astype(F32), axis=0)
    lb_all = jnp.cumsum(lb_p, axis=0) - lb_p[:1]
    tabs = _rope_tabs(s, ROW_TILE)

    for l in range(depth):
        j = l // 2
        even = l % 2 == 0
        ctx_later = any(m % 2 == 0 for m in range(l + 1, depth))
        n_rows = n_lat + n_ctx if ctx_later else n_lat
        mod = mod_all[l]
        ng1, ng2 = norm_g[l, 0][None, :], norm_g[l, 1][None, :]
        if even:
            (q, k, v, qh, kf, lgf, kb, lgb, vh, hg) = _inproj(
                h, mod, ng1, _pack_in(ab_w_in[j]), ab_g_cq[j][None, :], ab_g_ckv[j][None, :],
                _pack_q(ab_w_uq[j]), _pack_kv(ab_w_ukv[j]), lb_all[j], tabs, dims)
            a = _attention_latent(q, k, v, dims)
            if ctx_later:
                a = _attention_ctx(a, q, k, v, dims)
            o_f, o_b = _gla(qh, kf, lgf, kb, lgb, vh, dims)
            h = _outproj(a, o_f, o_b, hg, hgrn_g_norm[j][None, :], ab_w_out[j].astype(BF16), h, mod,
                         dims, n_rows)
            h = _ffn(h, mod, ng2, ffn_w_gate[j].astype(BF16), ffn_w_up[j].astype(BF16),
                     ffn_w_down[j].astype(BF16), dims, n_rows)
        else:
            h = _pool(h, mod, ng1, pool_w[j].astype(BF16), pool_b[j], pool_scale[j][None, :], dims, n_rows)
            router = _pad_cols(moe_router[j], LANES)
            h = _moe(h, mod, ng2, router, moe_w_gate[j].astype(BF16), moe_w_up[j].astype(BF16),
                     moe_w_down[j].astype(BF16), dims, n_rows)
    return _final_norm(h, final_g[None, :], n_lat).reshape(b, s, d)
```

```python
import functools

import jax
import jax.numpy as jnp
from jax import lax
from jax.experimental import pallas as pl
from jax.experimental.pallas import tpu as pltpu

F32 = jnp.float32
BF16 = jnp.bfloat16

D_MODEL = 1024
GRID_W = 64
N_MOD = 6
EPS = 1e-6
N_HEADS = 8
V_HEAD = 64
QK_NOPE = 64
QK_ROPE = 32
QK_DIM = QK_NOPE + QK_ROPE
Q_LORA = 384
KV_LORA = 256
ROPE_BASE = 10000.0
MLA_W = 512
HGRN_W = 512
HGRN_K = 128
N_HGRN_HEADS = 4
POOL_WINDOWS = (2, 4, 8, 16)
POOL_C = 256
N_EXPERTS = 8

LANES = 128
SUBLANES = 8
VMEM_LIMIT_BYTES = 56 * 1024 * 1024

HEAD_PAD = LANES
QKV_W = N_HEADS * HEAD_PAD
KR_OFF = Q_LORA + KV_LORA
KRR_OFF = KR_OFF + LANES
H_OFF = KRR_OFF + LANES
IN_COLS = H_OFF + 5 * HGRN_W
GLA_CHUNK = 64
GLA_BLOCK = 256
GLA_PHASE_GROUP = 2
ROW_TILE = 256
MOE_TILE = 1024
MOE_FF_TILE = 512
LOG2E = 1.4426950408889634


def _params(*sem):
    return pltpu.CompilerParams(dimension_semantics=sem, vmem_limit_bytes=VMEM_LIMIT_BYTES)


def _dot(a, b):
    return jnp.dot(a, b, preferred_element_type=F32)


def _dot_nt(a, b):
    return lax.dot_general(a, b, (((1,), (1,)), ((), ())), preferred_element_type=F32)


def _dot_tn(a, b):
    return lax.dot_general(a, b, (((0,), (0,)), ((), ())), preferred_element_type=F32)


def _rms(x, g):
    return x * lax.rsqrt(jnp.mean(x * x, axis=-1, keepdims=True) + EPS) * g


def _silu(x):
    return x * jax.nn.sigmoid(x)


def _norm_mod(h, g, shift, scale):
    return _rms(h, g) * (1.0 + scale) + shift


def _mod_kernel(c_ref, w_ref, b_ref, o_ref):
    c = c_ref[...]
    o_ref[...] = jnp.dot(_silu(c), w_ref[...], preferred_element_type=F32,
                         precision=lax.Precision.HIGHEST) + b_ref[...]


def _mod_table(cvec, w_mod, b_mod):
    depth, d, nd = w_mod.shape
    g = cvec.shape[0]
    tn = 1536
    return pl.pallas_call(
        _mod_kernel,
        grid=(depth, nd // tn),
        in_specs=[pl.BlockSpec((g, d), lambda l, j: (0, 0)),
                  pl.BlockSpec((None, d, tn), lambda l, j: (l, 0, j)),
                  pl.BlockSpec((None, 1, tn), lambda l, j: (l, 0, j))],
        out_specs=pl.BlockSpec((None, g, tn), lambda l, j: (l, 0, j)),
        out_shape=jax.ShapeDtypeStruct((depth, g, nd), F32),
        compiler_params=_params("parallel", "parallel"),
        name="mod_table",
    )(cvec, w_mod, b_mod.reshape(depth, 1, nd))


def _inproj_kernel(h_ref, mod_ref, ng_ref, win_ref, gcq_ref, gckv_ref, wq_ref, wkv_ref,
                   lb_ref, cq_tab, sq_tab, ck_tab, sk_tab,
                   q_ref, k_ref, v_ref, qh_ref, kf_ref, lgf_ref, kb_ref, lgb_ref, vh_ref, hg_ref):
    u = _norm_mod(h_ref[...], ng_ref[...], mod_ref[0:1, :], mod_ref[1:2, :]).astype(BF16)
    p = _dot(u, win_ref[...])
    cqn = _rms(p[:, 0:Q_LORA], gcq_ref[...]).astype(BF16)
    a = _dot(cqn, wq_ref[...])
    cq = jnp.tile(cq_tab[...], (1, N_HEADS))
    sq = jnp.tile(sq_tab[...], (1, N_HEADS))
    q_ref[...] = (a[:, :QKV_W] * cq + a[:, QKV_W:] * sq).astype(BF16)
    ckvn = _rms(p[:, Q_LORA:KR_OFF], gckv_ref[...]).astype(BF16)
    kr = (p[:, KR_OFF:KRR_OFF] * ck_tab[...] + p[:, KRR_OFF:H_OFF] * sk_tab[...]).astype(BF16)
    kv = _dot(jnp.concatenate([ckvn, kr], axis=-1), wkv_ref[...])
    k_ref[...] = kv[:, :QKV_W].astype(BF16)
    lane = lax.broadcasted_iota(jnp.int32, (1, QKV_W), 1)
    ones_col = (lane % HEAD_PAD == V_HEAD).astype(F32)
    v_ref[...] = (kv[:, QKV_W:] + ones_col).astype(BF16)
    o = H_OFF
    qh_ref[...] = _silu(p[:, o:o + HGRN_W])
    kf = (1.0 - lb_ref[0:1, :]) * jax.nn.sigmoid(-p[:, o + HGRN_W:o + 2 * HGRN_W])
    kf_ref[...] = kf
    lgf_ref[...] = jnp.log1p(-kf)
    kb = (1.0 - lb_ref[1:2, :]) * jax.nn.sigmoid(-p[:, o + 2 * HGRN_W:o + 3 * HGRN_W])
    kb_ref[...] = kb
    lgb_ref[...] = jnp.log1p(-kb)
    vh_ref[...] = p[:, o + 3 * HGRN_W:o + 4 * HGRN_W]
    hg_ref[...] = p[:, o + 4 * HGRN_W:o + 5 * HGRN_W]


def _inproj(h, mod, ng, win, gcq, gckv, wq, wkv, lb, tabs, dims):
    n, d = h.shape
    b, s, ctx = dims
    tm = ROW_TILE
    n_lat_t = b * s // tm
    pos_blocks = s // tm

    def grp(i):
        return jnp.minimum(i * tm // s, b)

    def pos(i):
        return jnp.where(i < n_lat_t, i % pos_blocks, pos_blocks)

    row = lambda i: (i, 0)
    const = lambda i: (0, 0)
    tab_spec = pl.BlockSpec((tm, LANES), lambda i: (pos(i), 0))
    wide = jax.ShapeDtypeStruct((n, QKV_W), BF16)
    hg = jax.ShapeDtypeStruct((n, HGRN_W), F32)
    return pl.pallas_call(
        _inproj_kernel,
        grid=(n // tm,),
        in_specs=[pl.BlockSpec((tm, d), row),
                  pl.BlockSpec((None, N_MOD, d), lambda i: (grp(i), 0, 0)),
                  pl.BlockSpec((1, d), const),
                  pl.BlockSpec(win.shape, const),
                  pl.BlockSpec((1, Q_LORA), const),
                  pl.BlockSpec((1, KV_LORA), const),
                  pl.BlockSpec(wq.shape, const),
                  pl.BlockSpec(wkv.shape, const),
                  pl.BlockSpec((2, HGRN_W), const),
                  tab_spec, tab_spec, tab_spec, tab_spec],
        out_specs=[pl.BlockSpec((tm, QKV_W), row)] * 3 + [pl.BlockSpec((tm, HGRN_W), row)] * 7,
        out_shape=[wide] * 3 + [hg] * 7,
        compiler_params=_params("parallel"),
        name="inproj",
    )(h, mod, ng, win, gcq, gckv, wq, wkv, lb, *tabs)


def _attn_kernel(*refs, n_kv):
    q_ref = refs[0]
    k_refs = refs[1:1 + 2 * n_kv:2]
    v_refs = refs[2:2 + 2 * n_kv:2]
    o_ref = refs[-1]
    for hd in range(N_HEADS):
        sl = slice(hd * HEAD_PAD, (hd + 1) * HEAD_PAD)
        q = q_ref[:, sl]
        scores = [_dot_nt(q, k_ref[:, sl]) for k_ref in k_refs]
        m = functools.reduce(jnp.maximum, [jnp.max(sc, axis=-1, keepdims=True) for sc in scores])
        acc = None
        for sc, v_ref in zip(scores, v_refs):
            part = _dot(jnp.exp2(sc - m).astype(BF16), v_ref[:, sl])
            acc = part if acc is None else acc + part
        o = acc[:, :V_HEAD] / acc[:, V_HEAD:V_HEAD + 1]
        o_ref[:, hd * V_HEAD:(hd + 1) * V_HEAD] = o.astype(o_ref.dtype)


def _attention_latent(q, k, v, dims):
    n = q.shape[0]
    b, s, ctx = dims
    tq = ROW_TILE
    nq = s // tq
    lat = lambda bi, j: (bi, 0)
    cx = lambda bi, j: (b * s // ctx + bi, 0)
    return pl.pallas_call(
        functools.partial(_attn_kernel, n_kv=2),
        grid=(b, nq),
        in_specs=[pl.BlockSpec((tq, QKV_W), lambda bi, j: (bi * nq + j, 0)),
                  pl.BlockSpec((s, QKV_W), lat), pl.BlockSpec((s, QKV_W), lat),
                  pl.BlockSpec((ctx, QKV_W), cx), pl.BlockSpec((ctx, QKV_W), cx)],
        out_specs=pl.BlockSpec((tq, MLA_W), lambda bi, j: (bi * nq + j, 0)),
        out_shape=jax.ShapeDtypeStruct((b * s, MLA_W), BF16),
        compiler_params=_params("parallel", "arbitrary"),
        name="attn_latent",
    )(q, k, v, k, v)


def _attention_ctx(q, k, v, dims):
    b, s, ctx = dims
    cx = lambda bi: (b * s // ctx + bi, 0)
    return pl.pallas_call(
        functools.partial(_attn_kernel, n_kv=1),
        grid=(b,),
        in_specs=[pl.BlockSpec((ctx, QKV_W), cx), pl.BlockSpec((ctx, QKV_W), cx),
                  pl.BlockSpec((ctx, QKV_W), cx)],
        out_specs=pl.BlockSpec((ctx, MLA_W), lambda bi: (bi, 0)),
        out_shape=jax.ShapeDtypeStruct((b * ctx, MLA_W), BF16),
        compiler_params=_params("parallel"),
        name="attn_ctx",
    )(q, k, v)


def _bcast_block_row(g, bs, row):
    if bs == GLA_CHUNK:
        return g[row:row + 1, :]
    g3 = g.reshape(GLA_CHUNK // bs, bs, g.shape[-1])
    return jnp.broadcast_to(g3[:, row:row + 1, :], g3.shape).reshape(g.shape)


def _gla_consts(reverse):
    c = GLA_CHUNK
    t_idx = lax.broadcasted_iota(jnp.int32, (c, c), 0)
    s_idx = lax.broadcasted_iota(jnp.int32, (c, c), 1)
    pt = (c - 1 - t_idx) if reverse else t_idx
    ps = (c - 1 - s_idx) if reverse else s_idx
    tri = (ps <= pt).astype(BF16)
    levels = []
    for half in (32, 16, 8):
        bs = 2 * half
        mask = (t_idx // bs == s_idx // bs) & (pt % bs >= half) & (ps % bs < half)
        levels.append((bs, half if reverse else half - 1, False, mask))
    dmask = (t_idx // SUBLANES == s_idx // SUBLANES) & (ps <= pt)
    levels.append((SUBLANES, 4 if reverse else 3, True, dmask))
    end_row = 0 if reverse else c - 1
    return tri, levels, end_row


def _gla_kernel(qf_ref, kf_ref, lgf_ref, vf_ref, qb_ref, kb_ref, lgb_ref, vb_ref,
                of_ref, ob_ref, stf_ref, stb_ref):
    @pl.when(pl.program_id(1) == 0)
    def _():
        stf_ref[...] = jnp.zeros_like(stf_ref)
        stb_ref[...] = jnp.zeros_like(stb_ref)

    c = GLA_CHUNK
    n_chunks = GLA_BLOCK // c
    dirs = [(qf_ref, kf_ref, lgf_ref, vf_ref, of_ref, stf_ref, False),
            (qb_ref, kb_ref, lgb_ref, vb_ref, ob_ref, stb_ref, True)]
    consts = [_gla_consts(False), _gla_consts(True)]
    states = [[d[5][hd] for hd in range(N_HGRN_HEADS)] for d in dirs]
    for step0 in range(0, n_chunks, GLA_PHASE_GROUP):
      units = []
      for step in range(step0, step0 + GLA_PHASE_GROUP):
        for di, (q_ref, k_ref, lg_ref, v_ref, o_ref, _, reverse) in enumerate(dirs):
            tri, levels, end_row = consts[di]
            ci = n_chunks - 1 - step if reverse else step
            rows = slice(ci * c, (ci + 1) * c)
            lg = lg_ref[rows, :]
            hi = lg.astype(BF16)
            r1 = lg - hi.astype(F32)
            mid = r1.astype(BF16)
            lo = (r1 - mid.astype(F32)).astype(BF16)
            gc = (_dot(tri, hi) + _dot(tri, mid) + _dot(tri, lo)) * LOG2E
            for hd in range(N_HGRN_HEADS):
                hs = slice(hd * HGRN_K, (hd + 1) * HGRN_K)
                units.append(dict(di=di, hd=hd, rows=rows, hs=hs, g=gc[:, hs], o_ref=o_ref,
                                  qh=q_ref[rows, hs], kh=k_ref[rows, hs],
                                  vh=v_ref[rows, hs].astype(BF16), levels=levels, end_row=end_row,
                                  attn=jnp.zeros((c, c), F32)))
      if True:
        for li in range(4):
            for u in units:
                bs, row, diag, mask = u["levels"][li]
                g, qh, kh = u["g"], u["qh"], u["kh"]
                d = g - _bcast_block_row(g, bs, row)
                if diag:
                    qf = (qh * jnp.exp2(d)).astype(BF16)
                    kf = (kh * jnp.exp2(-d)).astype(BF16)
                else:
                    e = jnp.exp2(-jnp.abs(d))
                    qf = (qh * e).astype(BF16)
                    kf = (kh * e).astype(BF16)
                u["attn"] = jnp.where(mask, _dot_nt(qf, kf), u["attn"])
        for u in units:
            g, qh, kh, vh = u["g"], u["qh"], u["kh"], u["vh"]
            st = states[u["di"]][u["hd"]]
            g_end = g[u["end_row"]:u["end_row"] + 1, :]
            qe = (qh * jnp.exp2(g)).astype(BF16)
            ke = (kh * jnp.exp2(g_end - g)).astype(BF16)
            o = _dot_nt(qe, st.astype(BF16)) + _dot(u["attn"].astype(BF16), vh)
            u["o_ref"][u["rows"], u["hs"]] = o
            states[u["di"]][u["hd"]] = st * jnp.exp2(g_end) + _dot_tn(vh, ke)
    for di, d in enumerate(dirs):
        for hd in range(N_HGRN_HEADS):
            d[5][hd] = states[di][hd]


def _gla(qh, kf, lgf, kb, lgb, vh, dims):
    n = qh.shape[0]
    b, s, ctx = dims
    blk = GLA_BLOCK
    ncb, nsb, nlat = ctx // blk, s // blk, b * s // blk

    def fwd(bi, j):
        return (jnp.where(j < ncb, nlat + bi * ncb + j, bi * nsb + (j - ncb)), 0)

    def bwd(bi, j):
        return (jnp.where(j < ncb, nlat + bi * ncb + (ncb - 1 - j), bi * nsb + (nsb - 1 - (j - ncb))), 0)

    fs = pl.BlockSpec((blk, HGRN_W), fwd)
    bs = pl.BlockSpec((blk, HGRN_W), bwd)
    out = jax.ShapeDtypeStruct((n, HGRN_W), F32)
    return pl.pallas_call(
        _gla_kernel,
        grid=(b, ncb + nsb),
        in_specs=[fs, fs, fs, fs, bs, bs, bs, bs],
        out_specs=[fs, bs],
        out_shape=[out, out],
        scratch_shapes=[pltpu.VMEM((N_HGRN_HEADS, HGRN_K, HGRN_K), F32)] * 2,
        compiler_params=_params("parallel", "arbitrary"),
        name="gla",
    )(qh, kf, lgf, vh, qh, kb, lgb, vh)


def _outproj_kernel(a_ref, of_ref, ob_ref, hg_ref, gn_ref, w_ref, h_ref, mod_ref, o_ref):
    o = of_ref[...] + ob_ref[...]
    gate = _silu(hg_ref[...])
    ys = []
    for hd in range(N_HGRN_HEADS):
        hs = slice(hd * HGRN_K, (hd + 1) * HGRN_K)
        ys.append(_rms(o[:, hs], gn_ref[...]) * gate[:, hs])
    y = jnp.concatenate(ys, axis=-1).astype(BF16)
    mix = _dot(a_ref[...], w_ref[0:MLA_W, :]) + _dot(y, w_ref[MLA_W:, :])
    o_ref[...] = h_ref[...] + mod_ref[2:3, :] * mix


def _outproj(a, of, ob, hg, gn, w, h, mod, dims, n_rows):
    n, d = h.shape
    b, s, ctx = dims
    tm = ROW_TILE
    row = lambda i: (i, 0)
    const = lambda i: (0, 0)
    half = pl.BlockSpec((tm, HGRN_W), row)
    return pl.pallas_call(
        _outproj_kernel,
        grid=(n_rows // tm,),
        in_specs=[pl.BlockSpec((tm, MLA_W), row), half, half, half,
                  pl.BlockSpec((1, HGRN_K), const), pl.BlockSpec(w.shape, const),
                  pl.BlockSpec((tm, d), row),
                  pl.BlockSpec((None, N_MOD, d), lambda i: (jnp.minimum(i * tm // s, b), 0, 0))],
        out_specs=pl.BlockSpec((tm, d), row),
        out_shape=jax.ShapeDtypeStruct((n, d), F32),
        input_output_aliases={6: 0},
        compiler_params=_params("parallel"),
        name="outproj",
    )(a, of, ob, hg, gn, w, h, mod)


def _ffn_kernel(h_ref, mod_ref, ng_ref, wg_ref, wu_ref, wd_ref, o_ref):
    h = h_ref[...]
    v = _norm_mod(h, ng_ref[...], mod_ref[3:4, :], mod_ref[4:5, :]).astype(BF16)
    hid = (_silu(_dot(v, wg_ref[...])) * _dot(v, wu_ref[...])).astype(BF16)
    o_ref[...] = h + mod_ref[5:6, :] * _dot(hid, wd_ref[...])


def _ffn(h, mod, ng, wg, wu, wd, dims, n_rows):
    n, d = h.shape
    b, s, ctx = dims
    tm = ROW_TILE
    row = lambda i: (i, 0)
    const = lambda i: (0, 0)
    return pl.pallas_call(
        _ffn_kernel,
        grid=(n_rows // tm,),
        in_specs=[pl.BlockSpec((tm, d), row),
                  pl.BlockSpec((None, N_MOD, d), lambda i: (jnp.minimum(i * tm // s, b), 0, 0)),
                  pl.BlockSpec((1, d), const),
                  pl.BlockSpec(wg.shape, const, pipeline_mode=pl.Buffered(1)),
                  pl.BlockSpec(wu.shape, const, pipeline_mode=pl.Buffered(1)),
                  pl.BlockSpec(wd.shape, const, pipeline_mode=pl.Buffered(1))],
        out_specs=pl.BlockSpec((tm, d), row),
        out_shape=jax.ShapeDtypeStruct((n, d), F32),
        input_output_aliases={0: 0},
        compiler_params=_params("parallel"),
        name="ffn",
    )(h, mod, ng, wg, wu, wd)


def _pool_kernel(h_ref, prev_ref, next_ref, mod_ref, ng_ref, w_ref, b_ref, sc_ref, o_ref, ext_ref,
                 *, tm, seq_tiles_lat, n_lat_tiles, seq_tiles_ctx):
    i = pl.program_id(0)
    halo = SUBLANES
    shift, scale = mod_ref[0:1, :], mod_ref[1:2, :]
    in_lat = i < n_lat_tiles
    j = jnp.where(in_lat, i % seq_tiles_lat, (i - n_lat_tiles) % seq_tiles_ctx)
    n_seq_tiles = jnp.where(in_lat, seq_tiles_lat, seq_tiles_ctx)
    h = h_ref[...]
    u = _norm_mod(h, ng_ref[...], shift, scale)
    up = _norm_mod(prev_ref[...], ng_ref[...], shift, scale)
    un = _norm_mod(next_ref[...], ng_ref[...], shift, scale)
    ext_ref[0:halo, :] = jnp.where(j > 0, up, 0.0)
    ext_ref[halo:halo + tm, :] = u
    ext_ref[halo + tm:, :] = jnp.where(j < n_seq_tiles - 1, un, 0.0)
    pos = j * tm + lax.broadcasted_iota(jnp.int32, (tm, 1), 0)
    t_len = n_seq_tiles * tm
    outs = []
    for g, win in enumerate(POOL_WINDOWS):
        cs = slice(g * POOL_C, (g + 1) * POOL_C)
        acc = None
        for dlt in range(-(win // 2), win // 2):
            part = ext_ref[halo + dlt:halo + dlt + tm, cs]
            acc = part if acc is None else acc + part
        lo = jnp.maximum(pos - win // 2, 0)
        hi = jnp.minimum(pos - win // 2 + win, t_len)
        cnt = (hi - lo).astype(F32)
        pooled = (acc / cnt - u[:, cs]).astype(BF16)
        outs.append(_dot(pooled, w_ref[g]) + b_ref[g:g + 1, :])
    y = jnp.concatenate(outs, axis=-1) * sc_ref[...]
    o_ref[...] = h + mod_ref[2:3, :] * y


def _pool(h, mod, ng, w, bias, scale, dims, n_rows):
    n, d = h.shape
    b, s, ctx = dims
    tm = ROW_TILE
    hb = tm // SUBLANES
    n_blk8 = n // SUBLANES
    row = lambda i: (i, 0)
    const = lambda i: (0, 0)
    kern = functools.partial(_pool_kernel, tm=tm, seq_tiles_lat=s // tm, n_lat_tiles=b * s // tm,
                             seq_tiles_ctx=ctx // tm)
    return pl.pallas_call(
        kern,
        grid=(n_rows // tm,),
        in_specs=[pl.BlockSpec((tm, d), row),
                  pl.BlockSpec((SUBLANES, d), lambda i: (jnp.maximum(i * hb - 1, 0), 0)),
                  pl.BlockSpec((SUBLANES, d), lambda i: (jnp.minimum((i + 1) * hb, n_blk8 - 1), 0)),
                  pl.BlockSpec((None, N_MOD, d), lambda i: (jnp.minimum(i * tm // s, b), 0, 0)),
                  pl.BlockSpec((1, d), const),
                  pl.BlockSpec(w.shape, lambda i: (0, 0, 0)),
                  pl.BlockSpec(bias.shape, const),
                  pl.BlockSpec((1, d), const)],
        out_specs=pl.BlockSpec((tm, d), row),
        out_shape=jax.ShapeDtypeStruct((n_rows, d), F32),
        scratch_shapes=[pltpu.VMEM((tm + 2 * SUBLANES, d), F32)],
        compiler_params=_params("parallel"),
        name="pool",
    )(h, h, h, mod, ng, w, bias, scale)


def _route_kernel(h_ref, mod_ref, ng_ref, r_ref, v_ref, idx_ref, wt_ref, lst_ref, base_ref, run_ref):
    tm = h_ref.shape[0]

    @pl.when(pl.program_id(0) == 0)
    def _():
        run_ref[...] = jnp.zeros_like(run_ref)

    v = _norm_mod(h_ref[...], ng_ref[...], mod_ref[3:4, :], mod_ref[4:5, :])
    v_ref[...] = v.astype(BF16)
    logits = jnp.dot(v, r_ref[...], preferred_element_type=F32, precision=lax.Precision.HIGHEST)
    lane = lax.broadcasted_iota(jnp.int32, logits.shape, 1)
    neg = jnp.float32(-jnp.inf)
    logits = jnp.where(lane < N_EXPERTS, logits, neg)
    m1 = jnp.max(logits, axis=-1, keepdims=True)
    i1 = jnp.min(jnp.where(logits == m1, lane, LANES), axis=-1, keepdims=True)
    rest = jnp.where(lane == i1, neg, logits)
    m2 = jnp.max(rest, axis=-1, keepdims=True)
    i2 = jnp.min(jnp.where(rest == m2, lane, LANES), axis=-1, keepdims=True)
    e2 = jnp.exp(m2 - m1)
    w1 = 1.0 / (1.0 + e2)
    w2 = e2 / (1.0 + e2)
    wt_ref[...] = jnp.where(lane == 0, w1, jnp.where(lane == 1, w2, 0.0))
    chose = (lane == i1) | (lane == i2)
    oh = chose.astype(BF16)
    ti = lax.broadcasted_iota(jnp.int32, (tm, tm), 0)
    tj = lax.broadcasted_iota(jnp.int32, (tm, tm), 1)
    before = _dot((tj < ti).astype(BF16), oh)
    run = run_ref[...]
    pos = run + before
    r1 = jnp.sum(jnp.where(lane == i1, pos, 0.0), axis=-1, keepdims=True)
    r2 = jnp.sum(jnp.where(lane == i2, pos, 0.0), axis=-1, keepdims=True)
    idx_ref[...] = jnp.where(lane == 0, i1, jnp.where(lane == 1, i2, jnp.where(
        lane == 2, r1.astype(jnp.int32), jnp.where(lane == 3, r2.astype(jnp.int32), 0))))
    base_ref[...] = run.astype(jnp.int32)
    run_ref[...] = run + jnp.sum(oh.astype(F32), axis=0, keepdims=True)
    before_t = _dot_tn(oh, (ti < tj).astype(BF16))
    chose_t = _dot_tn(oh, (ti == tj).astype(BF16))
    ids = lax.broadcasted_iota(jnp.int32, (tm, LANES), 0).astype(BF16)
    lst = jnp.zeros((tm, LANES), F32)
    for e in range(N_EXPERTS):
        sel = (before_t[e:e + 1, :] == ti.astype(F32)) & (chose_t[e:e + 1, :] > 0.5)
        lst = jnp.where(lane == e, _dot(sel.astype(BF16), ids), lst)
    lst_ref[...] = lst.astype(jnp.int32)


def _route(h, mod, ng, router, dims, n_rows):
    n, d = h.shape
    b, s, ctx = dims
    tm = ROW_TILE
    assert tm <= 256
    nt = n_rows // tm
    row = lambda i: (i, 0)
    const = lambda i: (0, 0)
    return pl.pallas_call(
        _route_kernel,
        grid=(nt,),
        in_specs=[pl.BlockSpec((tm, d), row),
                  pl.BlockSpec((None, N_MOD, d), lambda i: (jnp.minimum(i * tm // s, b), 0, 0)),
                  pl.BlockSpec((1, d), const),
                  pl.BlockSpec((d, LANES), const)],
        out_specs=[pl.BlockSpec((tm, d), row), pl.BlockSpec((tm, LANES), row),
                   pl.BlockSpec((tm, LANES), row), pl.BlockSpec((tm, LANES), row),
                   pl.BlockSpec((None, 1, LANES), lambda i: (i, 0, 0))],
        out_shape=[jax.ShapeDtypeStruct((n_rows, d), BF16),
                   jax.ShapeDtypeStruct((n_rows, LANES), jnp.int32),
                   jax.ShapeDtypeStruct((n_rows, LANES), F32),
                   jax.ShapeDtypeStruct((n_rows, LANES), jnp.int32),
                   jax.ShapeDtypeStruct((nt, 1, LANES), jnp.int32)],
        scratch_shapes=[pltpu.VMEM((1, LANES), F32)],
        compiler_params=_params("arbitrary"),
        name="route",
    )(h, mod, ng, router)


def _moe_kernel(te_ref, nu_ref, x_ref, wg_ref, wu_ref, wd_ref, o_ref):
    i = pl.program_id(0)
    f = pl.program_id(1)

    @pl.when(i < nu_ref[0])
    def _():
        x = x_ref[...]
        hid = (_silu(_dot(x, wg_ref[...].astype(BF16))) * _dot(x, wu_ref[...].astype(BF16))).astype(BF16)
        part = _dot(hid, wd_ref[...].astype(BF16))

        @pl.when(f == 0)
        def _():
            o_ref[...] = part

        @pl.when(f > 0)
        def _():
            o_ref[...] += part


def _moe_experts(x_sorted, tile_expert, n_used, wg, wu, wd, layer):
    npad, d = x_sorted.shape
    tm, tf = MOE_TILE, MOE_FF_TILE
    dff = wg.shape[-1]
    grid_spec = pltpu.PrefetchScalarGridSpec(
        num_scalar_prefetch=2,
        grid=(npad // tm, dff // tf),
        in_specs=[pl.BlockSpec((tm, d), lambda i, f, te, nu: (i, 0)),
                  pl.BlockSpec((None, None, d, tf), lambda i, f, te, nu: (layer, te[i], 0, f)),
                  pl.BlockSpec((None, None, d, tf), lambda i, f, te, nu: (layer, te[i], 0, f)),
                  pl.BlockSpec((None, None, tf, d), lambda i, f, te, nu: (layer, te[i], f, 0))],
        out_specs=pl.BlockSpec((tm, d), lambda i, f, te, nu: (i, 0)),
    )
    return pl.pallas_call(
        _moe_kernel,
        grid_spec=grid_spec,
        out_shape=jax.ShapeDtypeStruct((npad, d), F32),
        compiler_params=_params("parallel", "arbitrary"),
        name="moe_experts",
    )(tile_expert, n_used, x_sorted, wg, wu, wd)


def _combine_kernel(h_ref, y0_ref, y1_ref, wt_ref, mod_ref, o_ref):
    y = wt_ref[:, 0:1] * y0_ref[...] + wt_ref[:, 1:2] * y1_ref[...]
    o_ref[...] = h_ref[...] + mod_ref[5:6, :] * y


def _combine(h, y0, y1, wt, mod, dims, n_rows):
    n, d = h.shape
    b, s, ctx = dims
    tm = ROW_TILE
    row = lambda i: (i, 0)
    return pl.pallas_call(
        _combine_kernel,
        grid=(n_rows // tm,),
        in_specs=[pl.BlockSpec((tm, d), row), pl.BlockSpec((tm, d), row), pl.BlockSpec((tm, d), row),
                  pl.BlockSpec((tm, LANES), row),
                  pl.BlockSpec((None, N_MOD, d), lambda i: (jnp.minimum(i * tm // s, b), 0, 0))],
        out_specs=pl.BlockSpec((tm, d), row),
        out_shape=jax.ShapeDtypeStruct((n, d), F32),
        input_output_aliases={0: 0},
        compiler_params=_params("parallel"),
        name="moe_combine",
    )(h, y0, y1, wt, mod)


def _moe(h, mod, ng, router, wg, wu, wd, layer, dims, n_rows):
    tm, rt = MOE_TILE, ROW_TILE
    v, idx, wt, lst, base = _route(h, mod, ng, router, dims, n_rows)
    base = base[:, 0, :N_EXPERTS]
    e_sel, pos = idx[:, 0:2], idx[:, 2:4]
    last = e_sel[-rt:]
    counts = base[-1] + jnp.sum(last[:, :, None] == jnp.arange(N_EXPERTS), axis=(0, 1))
    tiles_per = (counts + tm - 1) // tm
    tile_end = jnp.cumsum(tiles_per)
    group_start = (tile_end - tiles_per) * tm
    slot = group_start[e_sel] + pos
    n_tiles = 2 * n_rows // tm + N_EXPERTS
    tile_expert = jnp.minimum(jnp.sum(jnp.arange(n_tiles)[:, None] >= tile_end[None, :], axis=1),
                              N_EXPERTS - 1).astype(jnp.int32)
    n_used = tile_end[-1:].astype(jnp.int32)
    rank = (jnp.arange(n_tiles * tm, dtype=jnp.int32).reshape(n_tiles, tm)
            - group_start[tile_expert][:, None])
    base_te = base.T[tile_expert]
    owns = base_te[:, None, :] <= rank[:, :, None]
    rtile = jnp.sum(owns, axis=-1) - 1
    local = rank - jnp.max(jnp.where(owns, base_te[:, None, :], 0), axis=-1)
    flat = jnp.clip((rtile * rt + local) * N_EXPERTS + tile_expert[:, None], 0, n_rows * N_EXPERTS - 1)
    src = rtile * rt + jnp.take(lst[:, :N_EXPERTS].reshape(-1), flat, mode="clip")
    src = jnp.where(rank < counts[tile_expert][:, None], src, 0).reshape(-1)
    x_sorted = jnp.take(v, src, axis=0, mode="clip")
    out_sorted = _moe_experts(x_sorted, tile_expert, n_used, wg, wu, wd, layer)
    y0 = jnp.take(out_sorted, slot[:, 0], axis=0, mode="clip")
    y1 = jnp.take(out_sorted, slot[:, 1], axis=0, mode="clip")
    return _combine(h, y0, y1, wt, mod, dims, n_rows)


def _final_kernel(h_ref, g_ref, o_ref):
    o_ref[...] = _rms(h_ref[...], g_ref[...])


def _final_norm(h, g, n_rows):
    n, d = h.shape
    tm = ROW_TILE
    row = lambda i: (i, 0)
    return pl.pallas_call(
        _final_kernel,
        grid=(n_rows // tm,),
        in_specs=[pl.BlockSpec((tm, d), row), pl.BlockSpec((1, d), lambda i: (0, 0))],
        out_specs=pl.BlockSpec((tm, d), row),
        out_shape=jax.ShapeDtypeStruct((n_rows, d), F32),
        compiler_params=_params("parallel"),
        name="final_norm",
    )(h, g)


def _rot_cols(w):
    ws = w.reshape(w.shape[:-1] + (2, 2, QK_ROPE // 4))
    return jnp.stack([-ws[..., 1, :], ws[..., 0, :]], axis=-2).reshape(w.shape)


def _pad_cols(w, width):
    return jnp.pad(w, ((0, 0), (0, width - w.shape[-1])))


def _pack_in(w_in):
    cq = w_in[:, :Q_LORA]
    ckv = w_in[:, Q_LORA:KR_OFF]
    kr = w_in[:, KR_OFF:KR_OFF + QK_ROPE]
    rest = w_in[:, KR_OFF + QK_ROPE:]
    return jnp.concatenate([cq, ckv, _pad_cols(kr, LANES), _pad_cols(_rot_cols(kr), LANES), rest],
                           axis=-1).astype(BF16)


def _pack_q(w_uq):
    w = w_uq.reshape(Q_LORA, N_HEADS, QK_DIM)
    nope, rope = w[..., :QK_NOPE], w[..., QK_NOPE:]
    z = jnp.zeros((Q_LORA, N_HEADS, HEAD_PAD - QK_DIM), w.dtype)
    plain = jnp.concatenate([nope, rope, z], axis=-1).reshape(Q_LORA, QKV_W)
    rot = jnp.concatenate([jnp.zeros_like(nope), _rot_cols(rope), z], axis=-1).reshape(Q_LORA, QKV_W)
    return jnp.concatenate([plain, rot], axis=-1).astype(BF16)


def _pack_kv(w_ukv):
    w = w_ukv.reshape(KV_LORA, N_HEADS, QK_NOPE + V_HEAD)
    z = jnp.zeros((KV_LORA, N_HEADS, HEAD_PAD - QK_NOPE), w.dtype)
    k_top = jnp.concatenate([w[..., :QK_NOPE], z], axis=-1).reshape(KV_LORA, QKV_W)
    v_top = jnp.concatenate([w[..., QK_NOPE:], z], axis=-1).reshape(KV_LORA, QKV_W)
    place = jnp.zeros((LANES, N_HEADS, HEAD_PAD), w.dtype)
    ii = jnp.arange(QK_ROPE)
    place = place.at[ii, :, QK_NOPE + ii].set(1.0).reshape(LANES, QKV_W)
    k_all = jnp.concatenate([k_top, place], axis=0)
    v_all = jnp.concatenate([v_top, jnp.zeros((LANES, QKV_W), w.dtype)], axis=0)
    return jnp.concatenate([k_all, v_all], axis=-1).astype(BF16)


def _rope_tabs(s, tm):
    rows = s // GRID_W
    row = jnp.repeat(jnp.arange(rows, dtype=F32), GRID_W)
    col = jnp.tile(jnp.arange(GRID_W, dtype=F32), rows)
    half = QK_ROPE // 2
    inv = 1.0 / (ROPE_BASE ** (jnp.arange(0, half, 2, dtype=F32) / half))
    ar = row[:, None] * inv[None, :]
    ac = col[:, None] * inv[None, :]
    ang = jnp.concatenate([ar, ar, ac, ac], axis=-1)
    cos = jnp.concatenate([jnp.cos(ang), jnp.ones((tm, QK_ROPE), F32)], axis=0)
    sin = jnp.concatenate([jnp.sin(ang), jnp.zeros((tm, QK_ROPE), F32)], axis=0)
    t = cos.shape[0]
    scale = QK_DIM ** -0.5 * LOG2E
    cq = jnp.concatenate([jnp.full((t, QK_NOPE), scale, F32), cos * scale,
                          jnp.zeros((t, HEAD_PAD - QK_DIM), F32)], axis=-1)
    sq = jnp.concatenate([jnp.zeros((t, QK_NOPE), F32), sin * scale,
                          jnp.zeros((t, HEAD_PAD - QK_DIM), F32)], axis=-1)
    ck = _pad_cols(cos, LANES)
    sk = _pad_cols(sin, LANES)
    return cq, sq, ck, sk


def kernel(x, c, ctx, c_ctx, w_mod, b_mod, norm_g, final_g, ab_w_in, ab_g_cq, ab_g_ckv, ab_w_uq, ab_w_ukv, hgrn_lb_logits, hgrn_g_norm, ab_w_out, ffn_w_gate, ffn_w_up, ffn_w_down, pool_w, pool_b, pool_scale, moe_router, moe_w_gate, moe_w_up, moe_w_down):
    b, s, d = x.shape
    ctx_len = ctx.shape[1]
    depth = w_mod.shape[0]
    dims = (b, s, ctx_len)
    n_lat, n_ctx = b * s, b * ctx_len
    assert d == D_MODEL and s % ROW_TILE == 0 and ctx_len % ROW_TILE == 0
    assert ctx_len % GLA_BLOCK == 0 and s % GLA_BLOCK == 0 and n_lat % ctx_len == 0

    h = jnp.concatenate([x.reshape(n_lat, d), ctx.reshape(n_ctx, d)], axis=0)
    n_groups = ((b + 1 + SUBLANES - 1) // SUBLANES) * SUBLANES
    cvec = jnp.concatenate([c, c_ctx[None, :], jnp.zeros((n_groups - b - 1, d), F32)], axis=0)
    mod_all = _mod_table(cvec, w_mod, b_mod).reshape(depth, n_groups, N_MOD, d)

    lb_p = jax.nn.softmax(hgrn_lb_logits.astype(F32), axis=0)
    lb_all = jnp.cumsum(lb_p, axis=0) - lb_p[:1]
    tabs = _rope_tabs(s, ROW_TILE)

    for l in range(depth):
        j = l // 2
        even = l % 2 == 0
        ctx_later = any(m % 2 == 0 for m in range(l + 1, depth))
        n_rows = n_lat + n_ctx if ctx_later else n_lat
        mod = mod_all[l]
        ng1, ng2 = norm_g[l, 0][None, :], norm_g[l, 1][None, :]
        if even:
            (q, k, v, qh, kf, lgf, kb, lgb, vh, hg) = _inproj(
                h, mod, ng1, _pack_in(ab_w_in[j]), ab_g_cq[j][None, :], ab_g_ckv[j][None, :],
                _pack_q(ab_w_uq[j]), _pack_kv(ab_w_ukv[j]), lb_all[j], tabs, dims)
            a = _attention_latent(q, k, v, dims)
            if ctx_later:
                a = jnp.concatenate([a, _attention_ctx(q, k, v, dims)], axis=0)
            o_f, o_b = _gla(qh, kf, lgf, kb, lgb, vh, dims)
            h = _outproj(a, o_f, o_b, hg, hgrn_g_norm[j][None, :], ab_w_out[j].astype(BF16), h, mod,
                         dims, n_rows)
            h = _ffn(h, mod, ng2, ffn_w_gate[j].astype(BF16), ffn_w_up[j].astype(BF16),
                     ffn_w_down[j].astype(BF16), dims, n_rows)
        else:
            h = _pool(h, mod, ng1, pool_w[j].astype(BF16), pool_b[j], pool_scale[j][None, :], dims, n_rows)
            router = _pad_cols(moe_router[j], LANES)
            h = _moe(h, mod, ng2, router, moe_w_gate, moe_w_up, moe_w_down, j, dims, n_rows)
    return _final_norm(h, final_g[None, :], n_lat).reshape(b, s, d)
```

```python
import functools

import jax
import jax.numpy as jnp
from jax import lax
from jax.experimental import pallas as pl
from jax.experimental.pallas import tpu as pltpu

F32 = jnp.float32
BF16 = jnp.bfloat16

D_MODEL = 1024
GRID_W = 64
N_MOD = 6
EPS = 1e-6
N_HEADS = 8
V_HEAD = 64
QK_NOPE = 64
QK_ROPE = 32
QK_DIM = QK_NOPE + QK_ROPE
Q_LORA = 384
KV_LORA = 256
ROPE_BASE = 10000.0
MLA_W = 512
HGRN_W = 512
HGRN_K = 128
N_HGRN_HEADS = 4
POOL_WINDOWS = (2, 4, 8, 16)
POOL_C = 256
N_EXPERTS = 8

LANES = 128
SUBLANES = 8
VMEM_LIMIT_BYTES = 56 * 1024 * 1024

HEAD_PAD = LANES
QKV_W = N_HEADS * HEAD_PAD
KR_OFF = Q_LORA + KV_LORA
KRR_OFF = KR_OFF + LANES
H_OFF = KRR_OFF + LANES
IN_COLS = H_OFF + 5 * HGRN_W
GLA_CHUNK = 64
GLA_BLOCK = 256
GLA_PHASE_GROUP = 2
ROW_TILE = 256
ATTN_Q_TILE = 512
ATTN_HEAD_GROUP = 2
MOE_TILE = 1024
MOE_FF_TILE = 512
MOE_FF_SPLIT = 2
LOG2E = 1.4426950408889634


def _params(*sem):
    return pltpu.CompilerParams(dimension_semantics=sem, vmem_limit_bytes=VMEM_LIMIT_BYTES)


def _dot(a, b):
    return jnp.dot(a, b, preferred_element_type=F32)


def _dot_nt(a, b):
    return lax.dot_general(a, b, (((1,), (1,)), ((), ())), preferred_element_type=F32)


def _dot_tn(a, b):
    return lax.dot_general(a, b, (((0,), (0,)), ((), ())), preferred_element_type=F32)


def _rms(x, g):
    return x * lax.rsqrt(jnp.mean(x * x, axis=-1, keepdims=True) + EPS) * g


def _silu(x):
    return x * jax.nn.sigmoid(x)


def _norm_mod(h, g, shift, scale):
    return _rms(h, g) * (1.0 + scale) + shift


def _mod_kernel(c_ref, w_ref, b_ref, o_ref):
    c = c_ref[...]
    o_ref[...] = jnp.dot(_silu(c), w_ref[...], preferred_element_type=F32,
                         precision=lax.Precision.HIGHEST) + b_ref[...]


def _mod_table(cvec, w_mod, b_mod):
    depth, d, nd = w_mod.shape
    g = cvec.shape[0]
    tn = 1536
    return pl.pallas_call(
        _mod_kernel,
        grid=(depth, nd // tn),
        in_specs=[pl.BlockSpec((g, d), lambda l, j: (0, 0)),
                  pl.BlockSpec((None, d, tn), lambda l, j: (l, 0, j)),
                  pl.BlockSpec((None, 1, tn), lambda l, j: (l, 0, j))],
        out_specs=pl.BlockSpec((None, g, tn), lambda l, j: (l, 0, j)),
        out_shape=jax.ShapeDtypeStruct((depth, g, nd), F32),
        compiler_params=_params("parallel", "parallel"),
        name="mod_table",
    )(cvec, w_mod, b_mod.reshape(depth, 1, nd))


def _inproj_kernel(h_ref, mod_ref, ng_ref, win_ref, gcq_ref, gckv_ref, wq_ref, wkv_ref,
                   lb_ref, cq_tab, sq_tab, ck_tab, sk_tab,
                   q_ref, k_ref, v_ref, qh_ref, kf_ref, lgf_ref, kb_ref, lgb_ref, vh_ref, hg_ref):
    u = _norm_mod(h_ref[...], ng_ref[...], mod_ref[0:1, :], mod_ref[1:2, :]).astype(BF16)
    p = _dot(u, win_ref[...])
    cqn = _rms(p[:, 0:Q_LORA], gcq_ref[...]).astype(BF16)
    a = _dot(cqn, wq_ref[...])
    cq = jnp.tile(cq_tab[...], (1, N_HEADS))
    sq = jnp.tile(sq_tab[...], (1, N_HEADS))
    q_ref[...] = (a[:, :QKV_W] * cq + a[:, QKV_W:] * sq).astype(BF16)
    ckvn = _rms(p[:, Q_LORA:KR_OFF], gckv_ref[...]).astype(BF16)
    kr = (p[:, KR_OFF:KRR_OFF] * ck_tab[...] + p[:, KRR_OFF:H_OFF] * sk_tab[...]).astype(BF16)
    kv = _dot(jnp.concatenate([ckvn, kr], axis=-1), wkv_ref[...])
    k_ref[...] = kv[:, :QKV_W].astype(BF16)
    lane = lax.broadcasted_iota(jnp.int32, (1, QKV_W), 1)
    ones_col = (lane % HEAD_PAD == V_HEAD).astype(F32)
    v_ref[...] = (kv[:, QKV_W:] + ones_col).astype(BF16)
    o = H_OFF
    qh_ref[...] = _silu(p[:, o:o + HGRN_W])
    kf = (1.0 - lb_ref[0:1, :]) * jax.nn.sigmoid(-p[:, o + HGRN_W:o + 2 * HGRN_W])
    kf_ref[...] = kf
    lgf_ref[...] = jnp.log1p(-kf)
    kb = (1.0 - lb_ref[1:2, :]) * jax.nn.sigmoid(-p[:, o + 2 * HGRN_W:o + 3 * HGRN_W])
    kb_ref[...] = kb
    lgb_ref[...] = jnp.log1p(-kb)
    vh_ref[...] = p[:, o + 3 * HGRN_W:o + 4 * HGRN_W]
    hg_ref[...] = p[:, o + 4 * HGRN_W:o + 5 * HGRN_W]


def _inproj(h, mod, ng, win, gcq, gckv, wq, wkv, lb, tabs, dims):
    n, d = h.shape
    b, s, ctx = dims
    tm = ROW_TILE
    n_lat_t = b * s // tm
    pos_blocks = s // tm

    def grp(i):
        return jnp.minimum(i * tm // s, b)

    def pos(i):
        return jnp.where(i < n_lat_t, i % pos_blocks, pos_blocks)

    row = lambda i: (i, 0)
    const = lambda i: (0, 0)
    tab_spec = pl.BlockSpec((tm, LANES), lambda i: (pos(i), 0))
    wide = jax.ShapeDtypeStruct((n, QKV_W), BF16)
    hg = jax.ShapeDtypeStruct((n, HGRN_W), F32)
    return pl.pallas_call(
        _inproj_kernel,
        grid=(n // tm,),
        in_specs=[pl.BlockSpec((tm, d), row),
                  pl.BlockSpec((None, N_MOD, d), lambda i: (grp(i), 0, 0)),
                  pl.BlockSpec((1, d), const),
                  pl.BlockSpec(win.shape, const),
                  pl.BlockSpec((1, Q_LORA), const),
                  pl.BlockSpec((1, KV_LORA), const),
                  pl.BlockSpec(wq.shape, const),
                  pl.BlockSpec(wkv.shape, const),
                  pl.BlockSpec((2, HGRN_W), const),
                  tab_spec, tab_spec, tab_spec, tab_spec],
        out_specs=[pl.BlockSpec((tm, QKV_W), row)] * 3 + [pl.BlockSpec((tm, HGRN_W), row)] * 7,
        out_shape=[wide] * 3 + [hg] * 7,
        compiler_params=_params("parallel"),
        name="inproj",
    )(h, mod, ng, win, gcq, gckv, wq, wkv, lb, *tabs)


def _attn_kernel(*refs, n_kv):
    q_ref = refs[0]
    k_refs = refs[1:1 + 2 * n_kv:2]
    v_refs = refs[2:2 + 2 * n_kv:2]
    o_ref = refs[-1]
    for h0 in range(0, N_HEADS, ATTN_HEAD_GROUP):
        heads = range(h0, h0 + ATTN_HEAD_GROUP)
        sls = {hd: slice(hd * HEAD_PAD, (hd + 1) * HEAD_PAD) for hd in heads}
        scores = {hd: [_dot_nt(q_ref[:, sls[hd]], k_ref[:, sls[hd]]) for k_ref in k_refs] for hd in heads}
        ms = {hd: functools.reduce(jnp.maximum, [jnp.max(sc, axis=-1, keepdims=True) for sc in scores[hd]])
              for hd in heads}
        for hd in heads:
            acc = None
            for sc, v_ref in zip(scores[hd], v_refs):
                part = _dot(jnp.exp2(sc - ms[hd]).astype(BF16), v_ref[:, sls[hd]])
                acc = part if acc is None else acc + part
            o = acc[:, :V_HEAD] / acc[:, V_HEAD:V_HEAD + 1]
            o_ref[:, hd * V_HEAD:(hd + 1) * V_HEAD] = o.astype(o_ref.dtype)


def _attention_latent(q, k, v, dims):
    b, s, ctx = dims
    tq = min(ATTN_Q_TILE, s)
    nq = s // tq
    lat = lambda bi, j: (bi, 0)
    cx = lambda bi, j: (b * s // ctx + bi, 0)
    return pl.pallas_call(
        functools.partial(_attn_kernel, n_kv=2),
        grid=(b, nq),
        in_specs=[pl.BlockSpec((tq, QKV_W), lambda bi, j: (bi * nq + j, 0)),
                  pl.BlockSpec((s, QKV_W), lat), pl.BlockSpec((s, QKV_W), lat),
                  pl.BlockSpec((ctx, QKV_W), cx), pl.BlockSpec((ctx, QKV_W), cx)],
        out_specs=pl.BlockSpec((tq, MLA_W), lambda bi, j: (bi * nq + j, 0)),
        out_shape=jax.ShapeDtypeStruct((b * s, MLA_W), BF16),
        compiler_params=_params("parallel", "arbitrary"),
        name="attn_latent",
    )(q, k, v, k, v)


def _attention_ctx(q, k, v, dims):
    b, s, ctx = dims
    cx = lambda bi: (b * s // ctx + bi, 0)
    return pl.pallas_call(
        functools.partial(_attn_kernel, n_kv=1),
        grid=(b,),
        in_specs=[pl.BlockSpec((ctx, QKV_W), cx), pl.BlockSpec((ctx, QKV_W), cx),
                  pl.BlockSpec((ctx, QKV_W), cx)],
        out_specs=pl.BlockSpec((ctx, MLA_W), lambda bi: (bi, 0)),
        out_shape=jax.ShapeDtypeStruct((b * ctx, MLA_W), BF16),
        compiler_params=_params("parallel"),
        name="attn_ctx",
    )(q, k, v)


def _bcast_block_row(g, bs, row):
    if bs == GLA_CHUNK:
        return g[row:row + 1, :]
    g3 = g.reshape(GLA_CHUNK // bs, bs, g.shape[-1])
    return jnp.broadcast_to(g3[:, row:row + 1, :], g3.shape).reshape(g.shape)


def _gla_consts(reverse):
    c = GLA_CHUNK
    t_idx = lax.broadcasted_iota(jnp.int32, (c, c), 0)
    s_idx = lax.broadcasted_iota(jnp.int32, (c, c), 1)
    pt = (c - 1 - t_idx) if reverse else t_idx
    ps = (c - 1 - s_idx) if reverse else s_idx
    tri = (ps <= pt).astype(BF16)
    levels = []
    for half in (32, 16, 8):
        bs = 2 * half
        mask = (t_idx // bs == s_idx // bs) & (pt % bs >= half) & (ps % bs < half)
        levels.append((bs, half if reverse else half - 1, False, mask))
    dmask = (t_idx // SUBLANES == s_idx // SUBLANES) & (ps <= pt)
    levels.append((SUBLANES, 4 if reverse else 3, True, dmask))
    end_row = 0 if reverse else c - 1
    return tri, levels, end_row


def _gla_kernel(qf_ref, kf_ref, lgf_ref, vf_ref, qb_ref, kb_ref, lgb_ref, vb_ref,
                of_ref, ob_ref, stf_ref, stb_ref):
    @pl.when(pl.program_id(1) == 0)
    def _():
        stf_ref[...] = jnp.zeros_like(stf_ref)
        stb_ref[...] = jnp.zeros_like(stb_ref)

    c = GLA_CHUNK
    n_chunks = GLA_BLOCK // c
    dirs = [(qf_ref, kf_ref, lgf_ref, vf_ref, of_ref, stf_ref, False),
            (qb_ref, kb_ref, lgb_ref, vb_ref, ob_ref, stb_ref, True)]
    consts = [_gla_consts(False), _gla_consts(True)]
    states = [[d[5][hd] for hd in range(N_HGRN_HEADS)] for d in dirs]
    for step0 in range(0, n_chunks, GLA_PHASE_GROUP):
      units = []
      for step in range(step0, step0 + GLA_PHASE_GROUP):
        for di, (q_ref, k_ref, lg_ref, v_ref, o_ref, _, reverse) in enumerate(dirs):
            tri, levels, end_row = consts[di]
            ci = n_chunks - 1 - step if reverse else step
            rows = slice(ci * c, (ci + 1) * c)
            lg = lg_ref[rows, :]
            hi = lg.astype(BF16)
            r1 = lg - hi.astype(F32)
            mid = r1.astype(BF16)
            lo = (r1 - mid.astype(F32)).astype(BF16)
            gc = (_dot(tri, hi) + _dot(tri, mid) + _dot(tri, lo)) * LOG2E
            for hd in range(N_HGRN_HEADS):
                hs = slice(hd * HGRN_K, (hd + 1) * HGRN_K)
                units.append(dict(di=di, hd=hd, rows=rows, hs=hs, g=gc[:, hs], o_ref=o_ref,
                                  qh=q_ref[rows, hs], kh=k_ref[rows, hs],
                                  vh=v_ref[rows, hs].astype(BF16), levels=levels, end_row=end_row,
                                  attn=jnp.zeros((c, c), F32)))
      if True:
        for li in range(4):
            for u in units:
                bs, row, diag, mask = u["levels"][li]
                g, qh, kh = u["g"], u["qh"], u["kh"]
                d = g - _bcast_block_row(g, bs, row)
                if diag:
                    qf = (qh * jnp.exp2(d)).astype(BF16)
                    kf = (kh * jnp.exp2(-d)).astype(BF16)
                else:
                    e = jnp.exp2(-jnp.abs(d))
                    qf = (qh * e).astype(BF16)
                    kf = (kh * e).astype(BF16)
                u["attn"] = jnp.where(mask, _dot_nt(qf, kf), u["attn"])
        for u in units:
            g, qh, kh, vh = u["g"], u["qh"], u["kh"], u["vh"]
            st = states[u["di"]][u["hd"]]
            g_end = g[u["end_row"]:u["end_row"] + 1, :]
            qe = (qh * jnp.exp2(g)).astype(BF16)
            ke = (kh * jnp.exp2(g_end - g)).astype(BF16)
            o = _dot_nt(qe, st.astype(BF16)) + _dot(u["attn"].astype(BF16), vh)
            u["o_ref"][u["rows"], u["hs"]] = o
            states[u["di"]][u["hd"]] = st * jnp.exp2(g_end) + _dot_tn(vh, ke)
    for di, d in enumerate(dirs):
        for hd in range(N_HGRN_HEADS):
            d[5][hd] = states[di][hd]


def _gla(qh, kf, lgf, kb, lgb, vh, dims):
    n = qh.shape[0]
    b, s, ctx = dims
    blk = GLA_BLOCK
    ncb, nsb, nlat = ctx // blk, s // blk, b * s // blk

    def fwd(bi, j):
        return (jnp.where(j < ncb, nlat + bi * ncb + j, bi * nsb + (j - ncb)), 0)

    def bwd(bi, j):
        return (jnp.where(j < ncb, nlat + bi * ncb + (ncb - 1 - j), bi * nsb + (nsb - 1 - (j - ncb))), 0)

    fs = pl.BlockSpec((blk, HGRN_W), fwd)
    bs = pl.BlockSpec((blk, HGRN_W), bwd)
    out = jax.ShapeDtypeStruct((n, HGRN_W), F32)
    return pl.pallas_call(
        _gla_kernel,
        grid=(b, ncb + nsb),
        in_specs=[fs, fs, fs, fs, bs, bs, bs, bs],
        out_specs=[fs, bs],
        out_shape=[out, out],
        scratch_shapes=[pltpu.VMEM((N_HGRN_HEADS, HGRN_K, HGRN_K), F32)] * 2,
        compiler_params=_params("parallel", "arbitrary"),
        name="gla",
    )(qh, kf, lgf, vh, qh, kb, lgb, vh)


def _outproj_kernel(a_ref, of_ref, ob_ref, hg_ref, gn_ref, w_ref, h_ref, mod_ref, o_ref):
    o = of_ref[...] + ob_ref[...]
    gate = _silu(hg_ref[...])
    ys = []
    for hd in range(N_HGRN_HEADS):
        hs = slice(hd * HGRN_K, (hd + 1) * HGRN_K)
        ys.append(_rms(o[:, hs], gn_ref[...]) * gate[:, hs])
    y = jnp.concatenate(ys, axis=-1).astype(BF16)
    mix = _dot(a_ref[...], w_ref[0:MLA_W, :]) + _dot(y, w_ref[MLA_W:, :])
    o_ref[...] = h_ref[...] + mod_ref[2:3, :] * mix


def _outproj(a, of, ob, hg, gn, w, h, mod, dims, n_rows):
    n, d = h.shape
    b, s, ctx = dims
    tm = ROW_TILE
    row = lambda i: (i, 0)
    const = lambda i: (0, 0)
    half = pl.BlockSpec((tm, HGRN_W), row)
    return pl.pallas_call(
        _outproj_kernel,
        grid=(n_rows // tm,),
        in_specs=[pl.BlockSpec((tm, MLA_W), row), half, half, half,
                  pl.BlockSpec((1, HGRN_K), const), pl.BlockSpec(w.shape, const),
                  pl.BlockSpec((tm, d), row),
                  pl.BlockSpec((None, N_MOD, d), lambda i: (jnp.minimum(i * tm // s, b), 0, 0))],
        out_specs=pl.BlockSpec((tm, d), row),
        out_shape=jax.ShapeDtypeStruct((n, d), F32),
        input_output_aliases={6: 0},
        compiler_params=_params("parallel"),
        name="outproj",
    )(a, of, ob, hg, gn, w, h, mod)


def _ffn_kernel(h_ref, mod_ref, ng_ref, wg_ref, wu_ref, wd_ref, o_ref):
    h = h_ref[...]
    v = _norm_mod(h, ng_ref[...], mod_ref[3:4, :], mod_ref[4:5, :]).astype(BF16)
    hid = (_silu(_dot(v, wg_ref[...])) * _dot(v, wu_ref[...])).astype(BF16)
    o_ref[...] = h + mod_ref[5:6, :] * _dot(hid, wd_ref[...])


def _ffn(h, mod, ng, wg, wu, wd, dims, n_rows):
    n, d = h.shape
    b, s, ctx = dims
    tm = ROW_TILE
    row = lambda i: (i, 0)
    const = lambda i: (0, 0)
    return pl.pallas_call(
        _ffn_kernel,
        grid=(n_rows // tm,),
        in_specs=[pl.BlockSpec((tm, d), row),
                  pl.BlockSpec((None, N_MOD, d), lambda i: (jnp.minimum(i * tm // s, b), 0, 0)),
                  pl.BlockSpec((1, d), const),
                  pl.BlockSpec(wg.shape, const, pipeline_mode=pl.Buffered(1)),
                  pl.BlockSpec(wu.shape, const, pipeline_mode=pl.Buffered(1)),
                  pl.BlockSpec(wd.shape, const, pipeline_mode=pl.Buffered(1))],
        out_specs=pl.BlockSpec((tm, d), row),
        out_shape=jax.ShapeDtypeStruct((n, d), F32),
        input_output_aliases={0: 0},
        compiler_params=_params("parallel"),
        name="ffn",
    )(h, mod, ng, wg, wu, wd)


def _pool_kernel(h_ref, prev_ref, next_ref, mod_ref, ng_ref, w_ref, b_ref, sc_ref, o_ref, ext_ref,
                 *, tm, seq_tiles_lat, n_lat_tiles, seq_tiles_ctx):
    i = pl.program_id(0)
    halo = SUBLANES
    shift, scale = mod_ref[0:1, :], mod_ref[1:2, :]
    in_lat = i < n_lat_tiles
    j = jnp.where(in_lat, i % seq_tiles_lat, (i - n_lat_tiles) % seq_tiles_ctx)
    n_seq_tiles = jnp.where(in_lat, seq_tiles_lat, seq_tiles_ctx)
    h = h_ref[...]
    u = _norm_mod(h, ng_ref[...], shift, scale)
    up = _norm_mod(prev_ref[...], ng_ref[...], shift, scale)
    un = _norm_mod(next_ref[...], ng_ref[...], shift, scale)
    ext_ref[0:halo, :] = jnp.where(j > 0, up, 0.0)
    ext_ref[halo:halo + tm, :] = u
    ext_ref[halo + tm:, :] = jnp.where(j < n_seq_tiles - 1, un, 0.0)
    pos = j * tm + lax.broadcasted_iota(jnp.int32, (tm, 1), 0)
    t_len = n_seq_tiles * tm
    outs = []
    for g, win in enumerate(POOL_WINDOWS):
        cs = slice(g * POOL_C, (g + 1) * POOL_C)
        acc = None
        for dlt in range(-(win // 2), win // 2):
            part = ext_ref[halo + dlt:halo + dlt + tm, cs]
            acc = part if acc is None else acc + part
        lo = jnp.maximum(pos - win // 2, 0)
        hi = jnp.minimum(pos - win // 2 + win, t_len)
        cnt = (hi - lo).astype(F32)
        pooled = (acc / cnt - u[:, cs]).astype(BF16)
        outs.append(_dot(pooled, w_ref[g]) + b_ref[g:g + 1, :])
    y = jnp.concatenate(outs, axis=-1) * sc_ref[...]
    o_ref[...] = h + mod_ref[2:3, :] * y


def _pool(h, mod, ng, w, bias, scale, dims, n_rows):
    n, d = h.shape
    b, s, ctx = dims
    tm = ROW_TILE
    hb = tm // SUBLANES
    n_blk8 = n // SUBLANES
    row = lambda i: (i, 0)
    const = lambda i: (0, 0)
    kern = functools.partial(_pool_kernel, tm=tm, seq_tiles_lat=s // tm, n_lat_tiles=b * s // tm,
                             seq_tiles_ctx=ctx // tm)
    return pl.pallas_call(
        kern,
        grid=(n_rows // tm,),
        in_specs=[pl.BlockSpec((tm, d), row),
                  pl.BlockSpec((SUBLANES, d), lambda i: (jnp.maximum(i * hb - 1, 0), 0)),
                  pl.BlockSpec((SUBLANES, d), lambda i: (jnp.minimum((i + 1) * hb, n_blk8 - 1), 0)),
                  pl.BlockSpec((None, N_MOD, d), lambda i: (jnp.minimum(i * tm // s, b), 0, 0)),
                  pl.BlockSpec((1, d), const),
                  pl.BlockSpec(w.shape, lambda i: (0, 0, 0)),
                  pl.BlockSpec(bias.shape, const),
                  pl.BlockSpec((1, d), const)],
        out_specs=pl.BlockSpec((tm, d), row),
        out_shape=jax.ShapeDtypeStruct((n_rows, d), F32),
        scratch_shapes=[pltpu.VMEM((tm + 2 * SUBLANES, d), F32)],
        compiler_params=_params("parallel"),
        name="pool",
    )(h, h, h, mod, ng, w, bias, scale)


def _route_kernel(h_ref, mod_ref, ng_ref, r_ref, v_ref, idx_ref, wt_ref, lst_ref, base_ref, run_ref):
    tm = h_ref.shape[0]

    @pl.when(pl.program_id(0) == 0)
    def _():
        run_ref[...] = jnp.zeros_like(run_ref)

    v = _norm_mod(h_ref[...], ng_ref[...], mod_ref[3:4, :], mod_ref[4:5, :])
    vb = pltpu.bitcast(v.astype(BF16).astype(F32), jnp.uint32)
    half = v.shape[-1] // 2
    v_ref[...] = (vb[:, :half] >> 16) | (vb[:, half:] & jnp.uint32(0xFFFF0000))
    logits = jnp.dot(v, r_ref[...], preferred_element_type=F32, precision=lax.Precision.HIGHEST)
    lane = lax.broadcasted_iota(jnp.int32, logits.shape, 1)
    neg = jnp.float32(-jnp.inf)
    logits = jnp.where(lane < N_EXPERTS, logits, neg)
    m1 = jnp.max(logits, axis=-1, keepdims=True)
    i1 = jnp.min(jnp.where(logits == m1, lane, LANES), axis=-1, keepdims=True)
    rest = jnp.where(lane == i1, neg, logits)
    m2 = jnp.max(rest, axis=-1, keepdims=True)
    i2 = jnp.min(jnp.where(rest == m2, lane, LANES), axis=-1, keepdims=True)
    e2 = jnp.exp(m2 - m1)
    w1 = 1.0 / (1.0 + e2)
    w2 = e2 / (1.0 + e2)
    wt_ref[...] = jnp.where(lane == 0, w1, jnp.where(lane == 1, w2, 0.0))
    chose = (lane == i1) | (lane == i2)
    oh = chose.astype(BF16)
    ti = lax.broadcasted_iota(jnp.int32, (tm, tm), 0)
    tj = lax.broadcasted_iota(jnp.int32, (tm, tm), 1)
    before = _dot((tj < ti).astype(BF16), oh)
    run = run_ref[...]
    pos = run + before
    r1 = jnp.sum(jnp.where(lane == i1, pos, 0.0), axis=-1, keepdims=True)
    r2 = jnp.sum(jnp.where(lane == i2, pos, 0.0), axis=-1, keepdims=True)
    idx_ref[...] = jnp.where(lane == 0, i1, jnp.where(lane == 1, i2, jnp.where(
        lane == 2, r1.astype(jnp.int32), jnp.where(lane == 3, r2.astype(jnp.int32), 0))))
    base_ref[...] = run.astype(jnp.int32)
    run_ref[...] = run + jnp.sum(oh.astype(F32), axis=0, keepdims=True)
    before_t = _dot_tn(oh, (ti < tj).astype(BF16))
    chose_t = _dot_tn(oh, (ti == tj).astype(BF16))
    ids = lax.broadcasted_iota(jnp.int32, (tm, LANES), 0).astype(BF16)
    lst = jnp.zeros((tm, LANES), F32)
    for e in range(N_EXPERTS):
        sel = (before_t[e:e + 1, :] == ti.astype(F32)) & (chose_t[e:e + 1, :] > 0.5)
        lst = jnp.where(lane == e, _dot(sel.astype(BF16), ids), lst)
    lst_ref[...] = lst.astype(jnp.int32)


def _route(h, mod, ng, router, dims, n_rows):
    n, d = h.shape
    b, s, ctx = dims
    tm = ROW_TILE
    assert tm <= 256
    nt = n_rows // tm
    row = lambda i: (i, 0)
    const = lambda i: (0, 0)
    return pl.pallas_call(
        _route_kernel,
        grid=(nt,),
        in_specs=[pl.BlockSpec((tm, d), row),
                  pl.BlockSpec((None, N_MOD, d), lambda i: (jnp.minimum(i * tm // s, b), 0, 0)),
                  pl.BlockSpec((1, d), const),
                  pl.BlockSpec((d, LANES), const)],
        out_specs=[pl.BlockSpec((tm, d // 2), row), pl.BlockSpec((tm, LANES), row),
                   pl.BlockSpec((tm, LANES), row), pl.BlockSpec((tm, LANES), row),
                   pl.BlockSpec((None, 1, LANES), lambda i: (i, 0, 0))],
        out_shape=[jax.ShapeDtypeStruct((n_rows, d // 2), jnp.uint32),
                   jax.ShapeDtypeStruct((n_rows, LANES), jnp.int32),
                   jax.ShapeDtypeStruct((n_rows, LANES), F32),
                   jax.ShapeDtypeStruct((n_rows, LANES), jnp.int32),
                   jax.ShapeDtypeStruct((nt, 1, LANES), jnp.int32)],
        scratch_shapes=[pltpu.VMEM((1, LANES), F32)],
        compiler_params=_params("arbitrary"),
        name="route",
    )(h, mod, ng, router)


def _moe_kernel(te_ref, nu_ref, x_ref, wg_ref, wu_ref, wd_ref, o_ref, xs_ref):
    i = pl.program_id(0)
    f = pl.program_id(1)

    @pl.when((i < nu_ref[0]) & (f == 0))
    def _():
        half = x_ref.shape[-1]
        xw = x_ref[...]
        xs_ref[:, :half] = pltpu.bitcast(xw << 16, F32).astype(BF16)
        xs_ref[:, half:] = pltpu.bitcast(xw & jnp.uint32(0xFFFF0000), F32).astype(BF16)
        o_ref[...] = jnp.zeros_like(o_ref)

    @pl.when(i < nu_ref[0])
    def _():
        x = xs_ref[...]
        w = wg_ref.shape[-1] // MOE_FF_SPLIT
        hids = []
        for c in range(MOE_FF_SPLIT):
            cs = slice(c * w, (c + 1) * w)
            hids.append((_silu(_dot(x, wg_ref[:, cs].astype(BF16))) *
                         _dot(x, wu_ref[:, cs].astype(BF16))).astype(BF16))
        part = None
        for c in range(MOE_FF_SPLIT):
            p = _dot(hids[c], wd_ref[c * w:(c + 1) * w, :].astype(BF16))
            part = p if part is None else part + p
        o_ref[...] += part


def _moe_experts(x_sorted, tile_expert, n_used, wg, wu, wd, layer):
    npad, dw = x_sorted.shape
    d = 2 * dw
    tm, tf = MOE_TILE, MOE_FF_TILE
    dff = wg.shape[-1]
    grid_spec = pltpu.PrefetchScalarGridSpec(
        num_scalar_prefetch=2,
        grid=(npad // tm, dff // tf),
        in_specs=[pl.BlockSpec((tm, dw), lambda i, f, te, nu: (i, 0)),
                  pl.BlockSpec((None, None, d, tf), lambda i, f, te, nu: (layer, te[i], 0, f)),
                  pl.BlockSpec((None, None, d, tf), lambda i, f, te, nu: (layer, te[i], 0, f)),
                  pl.BlockSpec((None, None, tf, d), lambda i, f, te, nu: (layer, te[i], f, 0))],
        out_specs=pl.BlockSpec((tm, d), lambda i, f, te, nu: (i, 0)),
        scratch_shapes=[pltpu.VMEM((tm, d), BF16)],
    )
    return pl.pallas_call(
        _moe_kernel,
        grid_spec=grid_spec,
        out_shape=jax.ShapeDtypeStruct((npad, d), F32),
        compiler_params=_params("parallel", "arbitrary"),
        name="moe_experts",
    )(tile_expert, n_used, x_sorted, wg, wu, wd)


def _combine_kernel(h_ref, y0_ref, y1_ref, wt_ref, mod_ref, o_ref):
    y = wt_ref[:, 0:1] * y0_ref[...] + wt_ref[:, 1:2] * y1_ref[...]
    o_ref[...] = h_ref[...] + mod_ref[5:6, :] * y


def _combine(h, y0, y1, wt, mod, dims, n_rows):
    n, d = h.shape
    b, s, ctx = dims
    tm = ROW_TILE
    row = lambda i: (i, 0)
    return pl.pallas_call(
        _combine_kernel,
        grid=(n_rows // tm,),
        in_specs=[pl.BlockSpec((tm, d), row), pl.BlockSpec((tm, d), row), pl.BlockSpec((tm, d), row),
                  pl.BlockSpec((tm, LANES), row),
                  pl.BlockSpec((None, N_MOD, d), lambda i: (jnp.minimum(i * tm // s, b), 0, 0))],
        out_specs=pl.BlockSpec((tm, d), row),
        out_shape=jax.ShapeDtypeStruct((n, d), F32),
        input_output_aliases={0: 0},
        compiler_params=_params("parallel"),
        name="moe_combine",
    )(h, y0, y1, wt, mod)


def _moe(h, mod, ng, router, wg, wu, wd, layer, dims, n_rows):
    tm, rt = MOE_TILE, ROW_TILE
    v, idx, wt, lst, base = _route(h, mod, ng, router, dims, n_rows)
    base = base[:, 0, :N_EXPERTS]
    e_sel, pos = idx[:, 0:2], idx[:, 2:4]
    last = e_sel[-rt:]
    counts = base[-1] + jnp.sum(last[:, :, None] == jnp.arange(N_EXPERTS), axis=(0, 1))
    tiles_per = (counts + tm - 1) // tm
    tile_end = jnp.cumsum(tiles_per)
    group_start = (tile_end - tiles_per) * tm
    slot = group_start[e_sel] + pos
    n_tiles = 2 * n_rows // tm + N_EXPERTS
    tile_expert = jnp.minimum(jnp.sum(jnp.arange(n_tiles)[:, None] >= tile_end[None, :], axis=1),
                              N_EXPERTS - 1).astype(jnp.int32)
    n_used = tile_end[-1:].astype(jnp.int32)
    rank = (jnp.arange(n_tiles * tm, dtype=jnp.int32).reshape(n_tiles, tm)
            - group_start[tile_expert][:, None])
    base_te = base.T[tile_expert]
    owns = base_te[:, None, :] <= rank[:, :, None]
    rtile = jnp.sum(owns, axis=-1) - 1
    local = rank - jnp.max(jnp.where(owns, base_te[:, None, :], 0), axis=-1)
    flat = jnp.clip((rtile * rt + local) * N_EXPERTS + tile_expert[:, None], 0, n_rows * N_EXPERTS - 1)
    src = rtile * rt + jnp.take(lst[:, :N_EXPERTS].reshape(-1), flat, mode="clip")
    src = jnp.where(rank < counts[tile_expert][:, None], src, 0).reshape(-1)
    x_sorted = jnp.take(v, src, axis=0, mode="clip")
    out_sorted = _moe_experts(x_sorted, tile_expert, n_used, wg, wu, wd, layer)
    y0 = jnp.take(out_sorted, slot[:, 0], axis=0, mode="clip")
    y1 = jnp.take(out_sorted, slot[:, 1], axis=0, mode="clip")
    return _combine(h, y0, y1, wt, mod, dims, n_rows)


def _final_kernel(h_ref, g_ref, o_ref):
    o_ref[...] = _rms(h_ref[...], g_ref[...])


def _final_norm(h, g, n_rows):
    n, d = h.shape
    tm = ROW_TILE
    row = lambda i: (i, 0)
    return pl.pallas_call(
        _final_kernel,
        grid=(n_rows // tm,),
        in_specs=[pl.BlockSpec((tm, d), row), pl.BlockSpec((1, d), lambda i: (0, 0))],
        out_specs=pl.BlockSpec((tm, d), row),
        out_shape=jax.ShapeDtypeStruct((n_rows, d), F32),
        compiler_params=_params("parallel"),
        name="final_norm",
    )(h, g)


def _rot_cols(w):
    ws = w.reshape(w.shape[:-1] + (2, 2, QK_ROPE // 4))
    return jnp.stack([-ws[..., 1, :], ws[..., 0, :]], axis=-2).reshape(w.shape)


def _pad_cols(w, width):
    return jnp.pad(w, ((0, 0), (0, width - w.shape[-1])))


def _pack_in(w_in):
    cq = w_in[:, :Q_LORA]
    ckv = w_in[:, Q_LORA:KR_OFF]
    kr = w_in[:, KR_OFF:KR_OFF + QK_ROPE]
    rest = w_in[:, KR_OFF + QK_ROPE:]
    return jnp.concatenate([cq, ckv, _pad_cols(kr, LANES), _pad_cols(_rot_cols(kr), LANES), rest],
                           axis=-1).astype(BF16)


def _pack_q(w_uq):
    w = w_uq.reshape(Q_LORA, N_HEADS, QK_DIM)
    nope, rope = w[..., :QK_NOPE], w[..., QK_NOPE:]
    z = jnp.zeros((Q_LORA, N_HEADS, HEAD_PAD - QK_DIM), w.dtype)
    plain = jnp.concatenate([nope, rope, z], axis=-1).reshape(Q_LORA, QKV_W)
    rot = jnp.concatenate([jnp.zeros_like(nope), _rot_cols(rope), z], axis=-1).reshape(Q_LORA, QKV_W)
    return jnp.concatenate([plain, rot], axis=-1).astype(BF16)


def _pack_kv(w_ukv):
    w = w_ukv.reshape(KV_LORA, N_HEADS, QK_NOPE + V_HEAD)
    z = jnp.zeros((KV_LORA, N_HEADS, HEAD_PAD - QK_NOPE), w.dtype)
    k_top = jnp.concatenate([w[..., :QK_NOPE], z], axis=-1).reshape(KV_LORA, QKV_W)
    v_top = jnp.concatenate([w[..., QK_NOPE:], z], axis=-1).reshape(KV_LORA, QKV_W)
    place = jnp.zeros((LANES, N_HEADS, HEAD_PAD), w.dtype)
    ii = jnp.arange(QK_ROPE)
    place = place.at[ii, :, QK_NOPE + ii].set(1.0).reshape(LANES, QKV_W)
    k_all = jnp.concatenate([k_top, place], axis=0)
    v_all = jnp.concatenate([v_top, jnp.zeros((LANES, QKV_W), w.dtype)], axis=0)
    return jnp.concatenate([k_all, v_all], axis=-1).astype(BF16)


def _rope_tabs(s, tm):
    rows = s // GRID_W
    row = jnp.repeat(jnp.arange(rows, dtype=F32), GRID_W)
    col = jnp.tile(jnp.arange(GRID_W, dtype=F32), rows)
    half = QK_ROPE // 2
    inv = 1.0 / (ROPE_BASE ** (jnp.arange(0, half, 2, dtype=F32) / half))
    ar = row[:, None] * inv[None, :]
    ac = col[:, None] * inv[None, :]
    ang = jnp.concatenate([ar, ar, ac, ac], axis=-1)
    cos = jnp.concatenate([jnp.cos(ang), jnp.ones((tm, QK_ROPE), F32)], axis=0)
    sin = jnp.concatenate([jnp.sin(ang), jnp.zeros((tm, QK_ROPE), F32)], axis=0)
    t = cos.shape[0]
    scale = QK_DIM ** -0.5 * LOG2E
    cq = jnp.concatenate([jnp.full((t, QK_NOPE), scale, F32), cos * scale,
                          jnp.zeros((t, HEAD_PAD - QK_DIM), F32)], axis=-1)
    sq = jnp.concatenate([jnp.zeros((t, QK_NOPE), F32), sin * scale,
                          jnp.zeros((t, HEAD_PAD - QK_DIM), F32)], axis=-1)
    ck = _pad_cols(cos, LANES)
    sk = _pad_cols(sin, LANES)
    return cq, sq, ck, sk


def kernel(x, c, ctx, c_ctx, w_mod, b_mod, norm_g, final_g, ab_w_in, ab_g_cq, ab_g_ckv, ab_w_uq, ab_w_ukv, hgrn_lb_logits, hgrn_g_norm, ab_w_out, ffn_w_gate, ffn_w_up, ffn_w_down, pool_w, pool_b, pool_scale, moe_router, moe_w_gate, moe_w_up, moe_w_down):
    b, s, d = x.shape
    ctx_len = ctx.shape[1]
    depth = w_mod.shape[0]
    dims = (b, s, ctx_len)
    n_lat, n_ctx = b * s, b * ctx_len
    assert d == D_MODEL and s % ROW_TILE == 0 and ctx_len % ROW_TILE == 0
    assert ctx_len % GLA_BLOCK == 0 and s % GLA_BLOCK == 0 and n_lat % ctx_len == 0

    h = jnp.concatenate([x.reshape(n_lat, d), ctx.reshape(n_ctx, d)], axis=0)
    n_groups = ((b + 1 + SUBLANES - 1) // SUBLANES) * SUBLANES
    cvec = jnp.concatenate([c, c_ctx[None, :], jnp.zeros((n_groups - b - 1, d), F32)], axis=0)
    mod_all = _mod_table(cvec, w_mod, b_mod).reshape(depth, n_groups, N_MOD, d)

    lb_p = jax.nn.softmax(hgrn_lb_logits.astype(F32), axis=0)
    lb_all = jnp.cumsum(lb_p, axis=0) - lb_p[:1]
    tabs = _rope_tabs(s, ROW_TILE)

    for l in range(depth):
        j = l // 2
        even = l % 2 == 0
        ctx_later = any(m % 2 == 0 for m in range(l + 1, depth))
        n_rows = n_lat + n_ctx if ctx_later else n_lat
        mod = mod_all[l]
        ng1, ng2 = norm_g[l, 0][None, :], norm_g[l, 1][None, :]
        if even:
            (q, k, v, qh, kf, lgf, kb, lgb, vh, hg) = _inproj(
                h, mod, ng1, _pack_in(ab_w_in[j]), ab_g_cq[j][None, :], ab_g_ckv[j][None, :],
                _pack_q(ab_w_uq[j]), _pack_kv(ab_w_ukv[j]), lb_all[j], tabs, dims)
            a = _attention_latent(q, k, v, dims)
            if ctx_later:
                a = jnp.concatenate([a, _attention_ctx(q, k, v, dims)], axis=0)
            o_f, o_b = _gla(qh, kf, lgf, kb, lgb, vh, dims)
            h = _outproj(a, o_f, o_b, hg, hgrn_g_norm[j][None, :], ab_w_out[j].astype(BF16), h, mod,
                         dims, n_rows)
            h = _ffn(h, mod, ng2, ffn_w_gate[j].astype(BF16), ffn_w_up[j].astype(BF16),
                     ffn_w_down[j].astype(BF16), dims, n_rows)
        else:
            h = _pool(h, mod, ng1, pool_w[j].astype(BF16), pool_b[j], pool_scale[j][None, :], dims, n_rows)
            router = _pad_cols(moe_router[j], LANES)
            h = _moe(h, mod, ng2, router, moe_w_gate, moe_w_up, moe_w_down, j, dims, n_rows)
    return _final_norm(h, final_g[None, :], n_lat).reshape(b, s, d)
```

```python
import functools

import jax
import jax.numpy as jnp
from jax import lax
from jax.experimental import pallas as pl
from jax.experimental.pallas import tpu as pltpu

F32 = jnp.float32
BF16 = jnp.bfloat16

D_MODEL = 1024
GRID_W = 64
N_MOD = 6
EPS = 1e-6
N_HEADS = 8
V_HEAD = 64
QK_NOPE = 64
QK_ROPE = 32
QK_DIM = QK_NOPE + QK_ROPE
Q_LORA = 384
KV_LORA = 256
ROPE_BASE = 10000.0
MLA_W = 512
HGRN_W = 512
HGRN_K = 128
N_HGRN_HEADS = 4
POOL_WINDOWS = (2, 4, 8, 16)
POOL_C = 256
N_EXPERTS = 8

LANES = 128
SUBLANES = 8
VMEM_LIMIT_BYTES = 56 * 1024 * 1024

HEAD_PAD = LANES
QKV_W = N_HEADS * HEAD_PAD
KR_OFF = Q_LORA + KV_LORA
KRR_OFF = KR_OFF + LANES
H_OFF = KRR_OFF + LANES
IN_COLS = H_OFF + 5 * HGRN_W
GLA_CHUNK = 64
GLA_BLOCK = 256
GLA_PHASE_GROUP = 2
ROW_TILE = 256
ATTN_Q_TILE = 512
ATTN_HEAD_GROUP = 2
MOE_TILE = 1024
MOE_FF_TILE = 512
MOE_FF_SPLIT = 2
LOG2E = 1.4426950408889634


def _params(*sem):
    return pltpu.CompilerParams(dimension_semantics=sem, vmem_limit_bytes=VMEM_LIMIT_BYTES)


def _dot(a, b):
    return jnp.dot(a, b, preferred_element_type=F32)


def _dot_nt(a, b):
    return lax.dot_general(a, b, (((1,), (1,)), ((), ())), preferred_element_type=F32)


def _dot_tn(a, b):
    return lax.dot_general(a, b, (((0,), (0,)), ((), ())), preferred_element_type=F32)


def _rms(x, g):
    return x * lax.rsqrt(jnp.mean(x * x, axis=-1, keepdims=True) + EPS) * g


def _silu(x):
    return x * jax.nn.sigmoid(x)


def _norm_mod(h, g, shift, scale):
    return _rms(h, g) * (1.0 + scale) + shift


def _mod_kernel(c_ref, w_ref, b_ref, o_ref):
    c = c_ref[...]
    o_ref[...] = jnp.dot(_silu(c), w_ref[...], preferred_element_type=F32,
                         precision=lax.Precision.HIGHEST) + b_ref[...]


def _mod_table(cvec, w_mod, b_mod):
    depth, d, nd = w_mod.shape
    g = cvec.shape[0]
    tn = 1536
    return pl.pallas_call(
        _mod_kernel,
        grid=(depth, nd // tn),
        in_specs=[pl.BlockSpec((g, d), lambda l, j: (0, 0)),
                  pl.BlockSpec((None, d, tn), lambda l, j: (l, 0, j)),
                  pl.BlockSpec((None, 1, tn), lambda l, j: (l, 0, j))],
        out_specs=pl.BlockSpec((None, g, tn), lambda l, j: (l, 0, j)),
        out_shape=jax.ShapeDtypeStruct((depth, g, nd), F32),
        compiler_params=_params("parallel", "parallel"),
        name="mod_table",
    )(cvec, w_mod, b_mod.reshape(depth, 1, nd))


def _inproj_kernel(h_ref, mod_ref, ng_ref, win_ref, gcq_ref, gckv_ref, wq_ref, wkv_ref,
                   lb_ref, cq_tab, sq_tab, ck_tab, sk_tab,
                   q_ref, k_ref, v_ref, qh_ref, kf_ref, lgf_ref, kb_ref, lgb_ref, vh_ref, hg_ref):
    u = _norm_mod(h_ref[...], ng_ref[...], mod_ref[0:1, :], mod_ref[1:2, :]).astype(BF16)
    p = _dot(u, win_ref[...])
    cqn = _rms(p[:, 0:Q_LORA], gcq_ref[...]).astype(BF16)
    a = _dot(cqn, wq_ref[...])
    cq = jnp.tile(cq_tab[...], (1, N_HEADS))
    sq = jnp.tile(sq_tab[...], (1, N_HEADS))
    q_ref[...] = (a[:, :QKV_W] * cq + a[:, QKV_W:] * sq).astype(BF16)
    ckvn = _rms(p[:, Q_LORA:KR_OFF], gckv_ref[...]).astype(BF16)
    kr = (p[:, KR_OFF:KRR_OFF] * ck_tab[...] + p[:, KRR_OFF:H_OFF] * sk_tab[...]).astype(BF16)
    kv = _dot(jnp.concatenate([ckvn, kr], axis=-1), wkv_ref[...])
    k_ref[...] = kv[:, :QKV_W].astype(BF16)
    lane = lax.broadcasted_iota(jnp.int32, (1, QKV_W), 1)
    ones_col = (lane % HEAD_PAD == V_HEAD).astype(F32)
    v_ref[...] = (kv[:, QKV_W:] + ones_col).astype(BF16)
    o = H_OFF
    qh_ref[...] = _silu(p[:, o:o + HGRN_W])
    kf = (1.0 - lb_ref[0:1, :]) * jax.nn.sigmoid(-p[:, o + HGRN_W:o + 2 * HGRN_W])
    kf_ref[...] = kf
    lgf_ref[...] = jnp.log1p(-kf)
    kb = (1.0 - lb_ref[1:2, :]) * jax.nn.sigmoid(-p[:, o + 2 * HGRN_W:o + 3 * HGRN_W])
    kb_ref[...] = kb
    lgb_ref[...] = jnp.log1p(-kb)
    vh_ref[...] = p[:, o + 3 * HGRN_W:o + 4 * HGRN_W]
    hg_ref[...] = p[:, o + 4 * HGRN_W:o + 5 * HGRN_W]


def _inproj(h, mod, ng, win, gcq, gckv, wq, wkv, lb, tabs, dims):
    n, d = h.shape
    b, s, ctx = dims
    tm = ROW_TILE
    n_lat_t = b * s // tm
    pos_blocks = s // tm

    def grp(i):
        return jnp.minimum(i * tm // s, b)

    def pos(i):
        return jnp.where(i < n_lat_t, i % pos_blocks, pos_blocks)

    row = lambda i: (i, 0)
    const = lambda i: (0, 0)
    tab_spec = pl.BlockSpec((tm, LANES), lambda i: (pos(i), 0))
    wide = jax.ShapeDtypeStruct((n, QKV_W), BF16)
    hg = jax.ShapeDtypeStruct((n, HGRN_W), F32)
    return pl.pallas_call(
        _inproj_kernel,
        grid=(n // tm,),
        in_specs=[pl.BlockSpec((tm, d), row),
                  pl.BlockSpec((None, N_MOD, d), lambda i: (grp(i), 0, 0)),
                  pl.BlockSpec((1, d), const),
                  pl.BlockSpec(win.shape, const),
                  pl.BlockSpec((1, Q_LORA), const),
                  pl.BlockSpec((1, KV_LORA), const),
                  pl.BlockSpec(wq.shape, const),
                  pl.BlockSpec(wkv.shape, const),
                  pl.BlockSpec((2, HGRN_W), const),
                  tab_spec, tab_spec, tab_spec, tab_spec],
        out_specs=[pl.BlockSpec((tm, QKV_W), row)] * 3 + [pl.BlockSpec((tm, HGRN_W), row)] * 7,
        out_shape=[wide] * 3 + [hg] * 7,
        compiler_params=_params("parallel"),
        name="inproj",
    )(h, mod, ng, win, gcq, gckv, wq, wkv, lb, *tabs)


def _attn_kernel(*refs, n_kv):
    q_ref = refs[0]
    k_refs = refs[1:1 + 2 * n_kv:2]
    v_refs = refs[2:2 + 2 * n_kv:2]
    o_ref = refs[-1]
    for h0 in range(0, N_HEADS, ATTN_HEAD_GROUP):
        heads = range(h0, h0 + ATTN_HEAD_GROUP)
        sls = {hd: slice(hd * HEAD_PAD, (hd + 1) * HEAD_PAD) for hd in heads}
        scores = {hd: [_dot_nt(q_ref[:, sls[hd]], k_ref[:, sls[hd]]) for k_ref in k_refs] for hd in heads}
        ms = {hd: functools.reduce(jnp.maximum, [jnp.max(sc, axis=-1, keepdims=True) for sc in scores[hd]])
              for hd in heads}
        for hd in heads:
            acc = None
            for sc, v_ref in zip(scores[hd], v_refs):
                part = _dot(jnp.exp2(sc - ms[hd]).astype(BF16), v_ref[:, sls[hd]])
                acc = part if acc is None else acc + part
            o = acc[:, :V_HEAD] / acc[:, V_HEAD:V_HEAD + 1]
            o_ref[:, hd * V_HEAD:(hd + 1) * V_HEAD] = o.astype(o_ref.dtype)


def _attention_latent(q, k, v, dims):
    b, s, ctx = dims
    tq = min(ATTN_Q_TILE, s)
    nq = s // tq
    lat = lambda bi, j: (bi, 0)
    cx = lambda bi, j: (b * s // ctx + bi, 0)
    return pl.pallas_call(
        functools.partial(_attn_kernel, n_kv=2),
        grid=(b, nq),
        in_specs=[pl.BlockSpec((tq, QKV_W), lambda bi, j: (bi * nq + j, 0)),
                  pl.BlockSpec((s, QKV_W), lat), pl.BlockSpec((s, QKV_W), lat),
                  pl.BlockSpec((ctx, QKV_W), cx), pl.BlockSpec((ctx, QKV_W), cx)],
        out_specs=pl.BlockSpec((tq, MLA_W), lambda bi, j: (bi * nq + j, 0)),
        out_shape=jax.ShapeDtypeStruct((b * s, MLA_W), BF16),
        compiler_params=_params("parallel", "arbitrary"),
        name="attn_latent",
    )(q, k, v, k, v)


def _attention_ctx(q, k, v, dims):
    b, s, ctx = dims
    cx = lambda bi: (b * s // ctx + bi, 0)
    return pl.pallas_call(
        functools.partial(_attn_kernel, n_kv=1),
        grid=(b,),
        in_specs=[pl.BlockSpec((ctx, QKV_W), cx), pl.BlockSpec((ctx, QKV_W), cx),
                  pl.BlockSpec((ctx, QKV_W), cx)],
        out_specs=pl.BlockSpec((ctx, MLA_W), lambda bi: (bi, 0)),
        out_shape=jax.ShapeDtypeStruct((b * ctx, MLA_W), BF16),
        compiler_params=_params("parallel"),
        name="attn_ctx",
    )(q, k, v)


def _bcast_block_row(g, bs, row):
    if bs == GLA_CHUNK:
        return g[row:row + 1, :]
    g3 = g.reshape(GLA_CHUNK // bs, bs, g.shape[-1])
    return jnp.broadcast_to(g3[:, row:row + 1, :], g3.shape).reshape(g.shape)


def _gla_consts(reverse):
    c = GLA_CHUNK
    t_idx = lax.broadcasted_iota(jnp.int32, (c, c), 0)
    s_idx = lax.broadcasted_iota(jnp.int32, (c, c), 1)
    pt = (c - 1 - t_idx) if reverse else t_idx
    ps = (c - 1 - s_idx) if reverse else s_idx
    tri = (ps <= pt).astype(BF16)
    levels = []
    for half in (32, 16, 8):
        bs = 2 * half
        mask = (t_idx // bs == s_idx // bs) & (pt % bs >= half) & (ps % bs < half)
        levels.append((bs, half if reverse else half - 1, False, mask))
    dmask = (t_idx // SUBLANES == s_idx // SUBLANES) & (ps <= pt)
    levels.append((SUBLANES, 4 if reverse else 3, True, dmask))
    end_row = 0 if reverse else c - 1
    return tri, levels, end_row


def _gla_kernel(qf_ref, kf_ref, lgf_ref, vf_ref, qb_ref, kb_ref, lgb_ref, vb_ref,
                of_ref, ob_ref, stf_ref, stb_ref):
    @pl.when(pl.program_id(1) == 0)
    def _():
        stf_ref[...] = jnp.zeros_like(stf_ref)
        stb_ref[...] = jnp.zeros_like(stb_ref)

    c = GLA_CHUNK
    n_chunks = GLA_BLOCK // c
    dirs = [(qf_ref, kf_ref, lgf_ref, vf_ref, of_ref, stf_ref, False),
            (qb_ref, kb_ref, lgb_ref, vb_ref, ob_ref, stb_ref, True)]
    consts = [_gla_consts(False), _gla_consts(True)]
    states = [[d[5][hd] for hd in range(N_HGRN_HEADS)] for d in dirs]
    for step0 in range(0, n_chunks, GLA_PHASE_GROUP):
      units = []
      for step in range(step0, step0 + GLA_PHASE_GROUP):
        for di, (q_ref, k_ref, lg_ref, v_ref, o_ref, _, reverse) in enumerate(dirs):
            tri, levels, end_row = consts[di]
            ci = n_chunks - 1 - step if reverse else step
            rows = slice(ci * c, (ci + 1) * c)
            lg = lg_ref[rows, :]
            hi = lg.astype(BF16)
            r1 = lg - hi.astype(F32)
            mid = r1.astype(BF16)
            lo = (r1 - mid.astype(F32)).astype(BF16)
            gc = (_dot(tri, hi) + _dot(tri, mid) + _dot(tri, lo)) * LOG2E
            for hd in range(N_HGRN_HEADS):
                hs = slice(hd * HGRN_K, (hd + 1) * HGRN_K)
                units.append(dict(di=di, hd=hd, rows=rows, hs=hs, g=gc[:, hs], o_ref=o_ref,
                                  qh=q_ref[rows, hs], kh=k_ref[rows, hs],
                                  vh=v_ref[rows, hs].astype(BF16), levels=levels, end_row=end_row,
                                  attn=jnp.zeros((c, c), F32)))
      if True:
        for li in range(4):
            for u in units:
                bs, row, diag, mask = u["levels"][li]
                g, qh, kh = u["g"], u["qh"], u["kh"]
                d = g - _bcast_block_row(g, bs, row)
                if diag:
                    qf = (qh * jnp.exp2(d)).astype(BF16)
                    kf = (kh * jnp.exp2(-d)).astype(BF16)
                else:
                    e = jnp.exp2(-jnp.abs(d))
                    qf = (qh * e).astype(BF16)
                    kf = (kh * e).astype(BF16)
                u["attn"] = jnp.where(mask, _dot_nt(qf, kf), u["attn"])
        for u in units:
            g, qh, kh, vh = u["g"], u["qh"], u["kh"], u["vh"]
            st = states[u["di"]][u["hd"]]
            g_end = g[u["end_row"]:u["end_row"] + 1, :]
            qe = (qh * jnp.exp2(g)).astype(BF16)
            ke = (kh * jnp.exp2(g_end - g)).astype(BF16)
            o = _dot_nt(qe, st.astype(BF16)) + _dot(u["attn"].astype(BF16), vh)
            u["o_ref"][u["rows"], u["hs"]] = o
            states[u["di"]][u["hd"]] = st * jnp.exp2(g_end) + _dot_tn(vh, ke)
    for di, d in enumerate(dirs):
        for hd in range(N_HGRN_HEADS):
            d[5][hd] = states[di][hd]


def _gla(qh, kf, lgf, kb, lgb, vh, dims):
    n = qh.shape[0]
    b, s, ctx = dims
    blk = GLA_BLOCK
    ncb, nsb, nlat = ctx // blk, s // blk, b * s // blk

    def fwd(bi, j):
        return (jnp.where(j < ncb, nlat + bi * ncb + j, bi * nsb + (j - ncb)), 0)

    def bwd(bi, j):
        return (jnp.where(j < ncb, nlat + bi * ncb + (ncb - 1 - j), bi * nsb + (nsb - 1 - (j - ncb))), 0)

    fs = pl.BlockSpec((blk, HGRN_W), fwd)
    bs = pl.BlockSpec((blk, HGRN_W), bwd)
    out = jax.ShapeDtypeStruct((n, HGRN_W), F32)
    return pl.pallas_call(
        _gla_kernel,
        grid=(b, ncb + nsb),
        in_specs=[fs, fs, fs, fs, bs, bs, bs, bs],
        out_specs=[fs, bs],
        out_shape=[out, out],
        scratch_shapes=[pltpu.VMEM((N_HGRN_HEADS, HGRN_K, HGRN_K), F32)] * 2,
        compiler_params=_params("parallel", "arbitrary"),
        name="gla",
    )(qh, kf, lgf, vh, qh, kb, lgb, vh)


def _outproj_kernel(a_ref, of_ref, ob_ref, hg_ref, gn_ref, w_ref, h_ref, mod_ref, o_ref):
    o = of_ref[...] + ob_ref[...]
    gate = _silu(hg_ref[...])
    ys = []
    for hd in range(N_HGRN_HEADS):
        hs = slice(hd * HGRN_K, (hd + 1) * HGRN_K)
        ys.append(_rms(o[:, hs], gn_ref[...]) * gate[:, hs])
    y = jnp.concatenate(ys, axis=-1).astype(BF16)
    mix = _dot(a_ref[...], w_ref[0:MLA_W, :]) + _dot(y, w_ref[MLA_W:, :])
    o_ref[...] = h_ref[...] + mod_ref[2:3, :] * mix


def _outproj(a, of, ob, hg, gn, w, h, mod, dims, n_rows):
    n, d = h.shape
    b, s, ctx = dims
    tm = ROW_TILE
    row = lambda i: (i, 0)
    const = lambda i: (0, 0)
    half = pl.BlockSpec((tm, HGRN_W), row)
    return pl.pallas_call(
        _outproj_kernel,
        grid=(n_rows // tm,),
        in_specs=[pl.BlockSpec((tm, MLA_W), row), half, half, half,
                  pl.BlockSpec((1, HGRN_K), const), pl.BlockSpec(w.shape, const),
                  pl.BlockSpec((tm, d), row),
                  pl.BlockSpec((None, N_MOD, d), lambda i: (jnp.minimum(i * tm // s, b), 0, 0))],
        out_specs=pl.BlockSpec((tm, d), row),
        out_shape=jax.ShapeDtypeStruct((n, d), F32),
        input_output_aliases={6: 0},
        compiler_params=_params("parallel"),
        name="outproj",
    )(a, of, ob, hg, gn, w, h, mod)


def _ffn_kernel(h_ref, mod_ref, ng_ref, wg_ref, wu_ref, wd_ref, o_ref):
    h = h_ref[...]
    v = _norm_mod(h, ng_ref[...], mod_ref[3:4, :], mod_ref[4:5, :]).astype(BF16)
    hid = (_silu(_dot(v, wg_ref[...])) * _dot(v, wu_ref[...])).astype(BF16)
    o_ref[...] = h + mod_ref[5:6, :] * _dot(hid, wd_ref[...])


def _ffn(h, mod, ng, wg, wu, wd, dims, n_rows):
    n, d = h.shape
    b, s, ctx = dims
    tm = ROW_TILE
    row = lambda i: (i, 0)
    const = lambda i: (0, 0)
    return pl.pallas_call(
        _ffn_kernel,
        grid=(n_rows // tm,),
        in_specs=[pl.BlockSpec((tm, d), row),
                  pl.BlockSpec((None, N_MOD, d), lambda i: (jnp.minimum(i * tm // s, b), 0, 0)),
                  pl.BlockSpec((1, d), const),
                  pl.BlockSpec(wg.shape, const, pipeline_mode=pl.Buffered(1)),
                  pl.BlockSpec(wu.shape, const, pipeline_mode=pl.Buffered(1)),
                  pl.BlockSpec(wd.shape, const, pipeline_mode=pl.Buffered(1))],
        out_specs=pl.BlockSpec((tm, d), row),
        out_shape=jax.ShapeDtypeStruct((n, d), F32),
        input_output_aliases={0: 0},
        compiler_params=_params("parallel"),
        name="ffn",
    )(h, mod, ng, wg, wu, wd)


def _pool_kernel(h_ref, prev_ref, next_ref, mod_ref, ng_ref, w_ref, b_ref, sc_ref, o_ref, ext_ref,
                 *, tm, seq_tiles_lat, n_lat_tiles, seq_tiles_ctx):
    i = pl.program_id(0)
    halo = SUBLANES
    shift, scale = mod_ref[0:1, :], mod_ref[1:2, :]
    in_lat = i < n_lat_tiles
    j = jnp.where(in_lat, i % seq_tiles_lat, (i - n_lat_tiles) % seq_tiles_ctx)
    n_seq_tiles = jnp.where(in_lat, seq_tiles_lat, seq_tiles_ctx)
    h = h_ref[...]
    u = _norm_mod(h, ng_ref[...], shift, scale)
    up = _norm_mod(prev_ref[...], ng_ref[...], shift, scale)
    un = _norm_mod(next_ref[...], ng_ref[...], shift, scale)
    ext_ref[0:halo, :] = jnp.where(j > 0, up, 0.0)
    ext_ref[halo:halo + tm, :] = u
    ext_ref[halo + tm:, :] = jnp.where(j < n_seq_tiles - 1, un, 0.0)
    pos = j * tm + lax.broadcasted_iota(jnp.int32, (tm, 1), 0)
    t_len = n_seq_tiles * tm
    outs = []
    for g, win in enumerate(POOL_WINDOWS):
        cs = slice(g * POOL_C, (g + 1) * POOL_C)
        acc = None
        for dlt in range(-(win // 2), win // 2):
            part = ext_ref[halo + dlt:halo + dlt + tm, cs]
            acc = part if acc is None else acc + part
        lo = jnp.maximum(pos - win // 2, 0)
        hi = jnp.minimum(pos - win // 2 + win, t_len)
        cnt = (hi - lo).astype(F32)
        pooled = (acc / cnt - u[:, cs]).astype(BF16)
        outs.append(_dot(pooled, w_ref[g]) + b_ref[g:g + 1, :])
    y = jnp.concatenate(outs, axis=-1) * sc_ref[...]
    o_ref[...] = h + mod_ref[2:3, :] * y


def _pool(h, mod, ng, w, bias, scale, dims, n_rows):
    n, d = h.shape
    b, s, ctx = dims
    tm = ROW_TILE
    hb = tm // SUBLANES
    n_blk8 = n // SUBLANES
    row = lambda i: (i, 0)
    const = lambda i: (0, 0)
    kern = functools.partial(_pool_kernel, tm=tm, seq_tiles_lat=s // tm, n_lat_tiles=b * s // tm,
                             seq_tiles_ctx=ctx // tm)
    return pl.pallas_call(
        kern,
        grid=(n_rows // tm,),
        in_specs=[pl.BlockSpec((tm, d), row),
                  pl.BlockSpec((SUBLANES, d), lambda i: (jnp.maximum(i * hb - 1, 0), 0)),
                  pl.BlockSpec((SUBLANES, d), lambda i: (jnp.minimum((i + 1) * hb, n_blk8 - 1), 0)),
                  pl.BlockSpec((None, N_MOD, d), lambda i: (jnp.minimum(i * tm // s, b), 0, 0)),
                  pl.BlockSpec((1, d), const),
                  pl.BlockSpec(w.shape, lambda i: (0, 0, 0)),
                  pl.BlockSpec(bias.shape, const),
                  pl.BlockSpec((1, d), const)],
        out_specs=pl.BlockSpec((tm, d), row),
        out_shape=jax.ShapeDtypeStruct((n_rows, d), F32),
        scratch_shapes=[pltpu.VMEM((tm + 2 * SUBLANES, d), F32)],
        compiler_params=_params("parallel"),
        name="pool",
    )(h, h, h, mod, ng, w, bias, scale)


def _route_kernel(h_ref, mod_ref, ng_ref, r_ref, v_ref, idx_ref, wt_ref, lst_ref, base_ref, run_ref):
    tm = h_ref.shape[0]

    @pl.when(pl.program_id(0) == 0)
    def _():
        run_ref[...] = jnp.zeros_like(run_ref)

    v = _norm_mod(h_ref[...], ng_ref[...], mod_ref[3:4, :], mod_ref[4:5, :])
    vb = pltpu.bitcast(v.astype(BF16).astype(F32), jnp.uint32)
    half = v.shape[-1] // 2
    v_ref[...] = (vb[:, :half] >> 16) | (vb[:, half:] & jnp.uint32(0xFFFF0000))
    v_hi = v.astype(BF16)
    v_lo = (v - v_hi.astype(F32)).astype(BF16)
    logits = _dot(v_hi, r_ref[0]) + (_dot(v_lo, r_ref[0]) + _dot(v_hi, r_ref[1]))
    lane = lax.broadcasted_iota(jnp.int32, logits.shape, 1)
    neg = jnp.float32(-jnp.inf)
    logits = jnp.where(lane < N_EXPERTS, logits, neg)
    m1 = jnp.max(logits, axis=-1, keepdims=True)
    i1 = jnp.min(jnp.where(logits == m1, lane, LANES), axis=-1, keepdims=True)
    rest = jnp.where(lane == i1, neg, logits)
    m2 = jnp.max(rest, axis=-1, keepdims=True)
    i2 = jnp.min(jnp.where(rest == m2, lane, LANES), axis=-1, keepdims=True)
    e2 = jnp.exp(m2 - m1)
    w1 = 1.0 / (1.0 + e2)
    w2 = e2 / (1.0 + e2)
    wt_ref[...] = jnp.where(lane == 0, w1, jnp.where(lane == 1, w2, 0.0))
    chose = (lane == i1) | (lane == i2)
    oh = chose.astype(BF16)
    ti = lax.broadcasted_iota(jnp.int32, (tm, tm), 0)
    tj = lax.broadcasted_iota(jnp.int32, (tm, tm), 1)
    before = _dot((tj < ti).astype(BF16), oh)
    run = run_ref[...]
    pos = run + before
    r1 = jnp.sum(jnp.where(lane == i1, pos, 0.0), axis=-1, keepdims=True)
    r2 = jnp.sum(jnp.where(lane == i2, pos, 0.0), axis=-1, keepdims=True)
    idx_ref[...] = jnp.where(lane == 0, i1, jnp.where(lane == 1, i2, jnp.where(
        lane == 2, r1.astype(jnp.int32), jnp.where(lane == 3, r2.astype(jnp.int32), 0))))
    base_ref[...] = run.astype(jnp.int32)
    run_ref[...] = run + jnp.sum(oh.astype(F32), axis=0, keepdims=True)
    before_t = _dot_tn(oh, (ti < tj).astype(BF16))
    chose_t = _dot_tn(oh, (ti == tj).astype(BF16))
    ids = lax.broadcasted_iota(jnp.int32, (tm, LANES), 0).astype(BF16)
    lst = jnp.zeros((tm, LANES), F32)
    for e in range(N_EXPERTS):
        sel = (before_t[e:e + 1, :] == ti.astype(F32)) & (chose_t[e:e + 1, :] > 0.5)
        lst = jnp.where(lane == e, _dot(sel.astype(BF16), ids), lst)
    lst_ref[...] = lst.astype(jnp.int32)


def _route(h, mod, ng, router, dims, n_rows):
    n, d = h.shape
    b, s, ctx = dims
    tm = ROW_TILE
    assert tm <= 256
    nt = n_rows // tm
    row = lambda i: (i, 0)
    const = lambda i: (0, 0)
    return pl.pallas_call(
        _route_kernel,
        grid=(nt,),
        in_specs=[pl.BlockSpec((tm, d), row),
                  pl.BlockSpec((None, N_MOD, d), lambda i: (jnp.minimum(i * tm // s, b), 0, 0)),
                  pl.BlockSpec((1, d), const),
                  pl.BlockSpec((2, d, LANES), lambda i: (0, 0, 0))],
        out_specs=[pl.BlockSpec((tm, d // 2), row), pl.BlockSpec((tm, LANES), row),
                   pl.BlockSpec((tm, LANES), row), pl.BlockSpec((tm, LANES), row),
                   pl.BlockSpec((None, 1, LANES), lambda i: (i, 0, 0))],
        out_shape=[jax.ShapeDtypeStruct((n_rows, d // 2), jnp.uint32),
                   jax.ShapeDtypeStruct((n_rows, LANES), jnp.int32),
                   jax.ShapeDtypeStruct((n_rows, LANES), F32),
                   jax.ShapeDtypeStruct((n_rows, LANES), jnp.int32),
                   jax.ShapeDtypeStruct((nt, 1, LANES), jnp.int32)],
        scratch_shapes=[pltpu.VMEM((1, LANES), F32)],
        compiler_params=_params("arbitrary"),
        name="route",
    )(h, mod, ng, router)


def _moe_kernel(te_ref, nu_ref, x_ref, wg_ref, wu_ref, wd_ref, o_ref, xs_ref):
    i = pl.program_id(0)
    f = pl.program_id(1)

    @pl.when((i < nu_ref[0]) & (f == 0))
    def _():
        half = x_ref.shape[-1]
        xw = x_ref[...]
        xs_ref[:, :half] = pltpu.bitcast(xw << 16, F32).astype(BF16)
        xs_ref[:, half:] = pltpu.bitcast(xw & jnp.uint32(0xFFFF0000), F32).astype(BF16)
        o_ref[...] = jnp.zeros_like(o_ref)

    @pl.when(i < nu_ref[0])
    def _():
        x = xs_ref[...]
        w = wg_ref.shape[-1] // MOE_FF_SPLIT
        hids = []
        for c in range(MOE_FF_SPLIT):
            cs = slice(c * w, (c + 1) * w)
            hids.append((_silu(_dot(x, wg_ref[:, cs].astype(BF16))) *
                         _dot(x, wu_ref[:, cs].astype(BF16))).astype(BF16))
        part = None
        for c in range(MOE_FF_SPLIT):
            p = _dot(hids[c], wd_ref[c * w:(c + 1) * w, :].astype(BF16))
            part = p if part is None else part + p
        o_ref[...] += part


def _moe_experts(x_sorted, tile_expert, n_used, wg, wu, wd, layer):
    npad, dw = x_sorted.shape
    d = 2 * dw
    tm, tf = MOE_TILE, MOE_FF_TILE
    dff = wg.shape[-1]
    grid_spec = pltpu.PrefetchScalarGridSpec(
        num_scalar_prefetch=2,
        grid=(npad // tm, dff // tf),
        in_specs=[pl.BlockSpec((tm, dw), lambda i, f, te, nu: (i, 0)),
                  pl.BlockSpec((None, None, d, tf), lambda i, f, te, nu: (layer, te[i], 0, f)),
                  pl.BlockSpec((None, None, d, tf), lambda i, f, te, nu: (layer, te[i], 0, f)),
                  pl.BlockSpec((None, None, tf, d), lambda i, f, te, nu: (layer, te[i], f, 0))],
        out_specs=pl.BlockSpec((tm, d), lambda i, f, te, nu: (i, 0)),
        scratch_shapes=[pltpu.VMEM((tm, d), BF16)],
    )
    return pl.pallas_call(
        _moe_kernel,
        grid_spec=grid_spec,
        out_shape=jax.ShapeDtypeStruct((npad, d), F32),
        compiler_params=_params("parallel", "arbitrary"),
        name="moe_experts",
    )(tile_expert, n_used, x_sorted, wg, wu, wd)


def _combine_kernel(h_ref, y0_ref, y1_ref, wt_ref, mod_ref, *rest):
    y = wt_ref[:, 0:1] * y0_ref[...] + wt_ref[:, 1:2] * y1_ref[...]
    h = h_ref[...] + mod_ref[5:6, :] * y
    if len(rest) == 2:
        fg_ref, o_ref = rest
        o_ref[...] = _rms(h, fg_ref[...])
    else:
        rest[0][...] = h


def _combine(h, y0, y1, wt, mod, dims, n_rows, final_g=None):
    n, d = h.shape
    b, s, ctx = dims
    tm = ROW_TILE
    row = lambda i: (i, 0)
    in_specs = [pl.BlockSpec((tm, d), row), pl.BlockSpec((tm, d), row), pl.BlockSpec((tm, d), row),
                pl.BlockSpec((tm, LANES), row),
                pl.BlockSpec((None, N_MOD, d), lambda i: (jnp.minimum(i * tm // s, b), 0, 0))]
    args = [h, y0, y1, wt, mod]
    if final_g is not None:
        in_specs.append(pl.BlockSpec((1, d), lambda i: (0, 0)))
        args.append(final_g)
    return pl.pallas_call(
        _combine_kernel,
        grid=(n_rows // tm,),
        in_specs=in_specs,
        out_specs=pl.BlockSpec((tm, d), row),
        out_shape=jax.ShapeDtypeStruct((n_rows if final_g is not None else n, d), F32),
        input_output_aliases={} if final_g is not None else {0: 0},
        compiler_params=_params("parallel"),
        name="moe_combine",
    )(*args)


def _moe(h, mod, ng, router, wg, wu, wd, layer, dims, n_rows, final_g=None):
    tm, rt = MOE_TILE, ROW_TILE
    v, idx, wt, lst, base = _route(h, mod, ng, router, dims, n_rows)
    base = base[:, 0, :N_EXPERTS]
    e_sel, pos = idx[:, 0:2], idx[:, 2:4]
    last = e_sel[-rt:]
    counts = base[-1] + jnp.sum(last[:, :, None] == jnp.arange(N_EXPERTS), axis=(0, 1))
    tiles_per = (counts + tm - 1) // tm
    tile_end = jnp.cumsum(tiles_per)
    group_start = (tile_end - tiles_per) * tm
    slot = group_start[e_sel] + pos
    n_tiles = 2 * n_rows // tm + N_EXPERTS
    tile_expert = jnp.minimum(jnp.sum(jnp.arange(n_tiles)[:, None] >= tile_end[None, :], axis=1),
                              N_EXPERTS - 1).astype(jnp.int32)
    n_used = tile_end[-1:].astype(jnp.int32)
    rank = (jnp.arange(n_tiles * tm, dtype=jnp.int32).reshape(n_tiles, tm)
            - group_start[tile_expert][:, None])
    base_te = base.T[tile_expert]
    owns = base_te[:, None, :] <= rank[:, :, None]
    rtile = jnp.sum(owns, axis=-1) - 1
    local = rank - jnp.max(jnp.where(owns, base_te[:, None, :], 0), axis=-1)
    flat = jnp.clip((rtile * rt + local) * N_EXPERTS + tile_expert[:, None], 0, n_rows * N_EXPERTS - 1)
    src = rtile * rt + jnp.take(lst[:, :N_EXPERTS].reshape(-1), flat, mode="clip")
    spread = jnp.arange(n_tiles * tm, dtype=jnp.int32).reshape(n_tiles, tm) % n_rows
    src = jnp.where(rank < counts[tile_expert][:, None], src, spread).reshape(-1)
    x_sorted = jnp.take(v, src, axis=0, mode="clip")
    out_sorted = _moe_experts(x_sorted, tile_expert, n_used, wg, wu, wd, layer)
    y0 = jnp.take(out_sorted, slot[:, 0], axis=0, mode="clip")
    y1 = jnp.take(out_sorted, slot[:, 1], axis=0, mode="clip")
    return _combine(h, y0, y1, wt, mod, dims, n_rows, final_g)


def _final_kernel(h_ref, g_ref, o_ref):
    o_ref[...] = _rms(h_ref[...], g_ref[...])


def _final_norm(h, g, n_rows):
    n, d = h.shape
    tm = ROW_TILE
    row = lambda i: (i, 0)
    return pl.pallas_call(
        _final_kernel,
        grid=(n_rows // tm,),
        in_specs=[pl.BlockSpec((tm, d), row), pl.BlockSpec((1, d), lambda i: (0, 0))],
        out_specs=pl.BlockSpec((tm, d), row),
        out_shape=jax.ShapeDtypeStruct((n_rows, d), F32),
        compiler_params=_params("parallel"),
        name="final_norm",
    )(h, g)


def _rot_cols(w):
    ws = w.reshape(w.shape[:-1] + (2, 2, QK_ROPE // 4))
    return jnp.stack([-ws[..., 1, :], ws[..., 0, :]], axis=-2).reshape(w.shape)


def _pad_cols(w, width):
    return jnp.pad(w, ((0, 0), (0, width - w.shape[-1])))


def _pack_in(w_in):
    cq = w_in[:, :Q_LORA]
    ckv = w_in[:, Q_LORA:KR_OFF]
    kr = w_in[:, KR_OFF:KR_OFF + QK_ROPE]
    rest = w_in[:, KR_OFF + QK_ROPE:]
    return jnp.concatenate([cq, ckv, _pad_cols(kr, LANES), _pad_cols(_rot_cols(kr), LANES), rest],
                           axis=-1).astype(BF16)


def _pack_q(w_uq):
    w = w_uq.reshape(Q_LORA, N_HEADS, QK_DIM)
    nope, rope = w[..., :QK_NOPE], w[..., QK_NOPE:]
    z = jnp.zeros((Q_LORA, N_HEADS, HEAD_PAD - QK_DIM), w.dtype)
    plain = jnp.concatenate([nope, rope, z], axis=-1).reshape(Q_LORA, QKV_W)
    rot = jnp.concatenate([jnp.zeros_like(nope), _rot_cols(rope), z], axis=-1).reshape(Q_LORA, QKV_W)
    return jnp.concatenate([plain, rot], axis=-1).astype(BF16)


def _pack_kv(w_ukv):
    w = w_ukv.reshape(KV_LORA, N_HEADS, QK_NOPE + V_HEAD)
    z = jnp.zeros((KV_LORA, N_HEADS, HEAD_PAD - QK_NOPE), w.dtype)
    k_top = jnp.concatenate([w[..., :QK_NOPE], z], axis=-1).reshape(KV_LORA, QKV_W)
    v_top = jnp.concatenate([w[..., QK_NOPE:], z], axis=-1).reshape(KV_LORA, QKV_W)
    place = jnp.zeros((LANES, N_HEADS, HEAD_PAD), w.dtype)
    ii = jnp.arange(QK_ROPE)
    place = place.at[ii, :, QK_NOPE + ii].set(1.0).reshape(LANES, QKV_W)
    k_all = jnp.concatenate([k_top, place], axis=0)
    v_all = jnp.concatenate([v_top, jnp.zeros((LANES, QKV_W), w.dtype)], axis=0)
    return jnp.concatenate([k_all, v_all], axis=-1).astype(BF16)


def _rope_tabs(s, tm):
    rows = s // GRID_W
    row = jnp.repeat(jnp.arange(rows, dtype=F32), GRID_W)
    col = jnp.tile(jnp.arange(GRID_W, dtype=F32), rows)
    half = QK_ROPE // 2
    inv = 1.0 / (ROPE_BASE ** (jnp.arange(0, half, 2, dtype=F32) / half))
    ar = row[:, None] * inv[None, :]
    ac = col[:, None] * inv[None, :]
    ang = jnp.concatenate([ar, ar, ac, ac], axis=-1)
    cos = jnp.concatenate([jnp.cos(ang), jnp.ones((tm, QK_ROPE), F32)], axis=0)
    sin = jnp.concatenate([jnp.sin(ang), jnp.zeros((tm, QK_ROPE), F32)], axis=0)
    t = cos.shape[0]
    scale = QK_DIM ** -0.5 * LOG2E
    cq = jnp.concatenate([jnp.full((t, QK_NOPE), scale, F32), cos * scale,
                          jnp.zeros((t, HEAD_PAD - QK_DIM), F32)], axis=-1)
    sq = jnp.concatenate([jnp.zeros((t, QK_NOPE), F32), sin * scale,
                          jnp.zeros((t, HEAD_PAD - QK_DIM), F32)], axis=-1)
    ck = _pad_cols(cos, LANES)
    sk = _pad_cols(sin, LANES)
    return cq, sq, ck, sk


def kernel(x, c, ctx, c_ctx, w_mod, b_mod, norm_g, final_g, ab_w_in, ab_g_cq, ab_g_ckv, ab_w_uq, ab_w_ukv, hgrn_lb_logits, hgrn_g_norm, ab_w_out, ffn_w_gate, ffn_w_up, ffn_w_down, pool_w, pool_b, pool_scale, moe_router, moe_w_gate, moe_w_up, moe_w_down):
    b, s, d = x.shape
    ctx_len = ctx.shape[1]
    depth = w_mod.shape[0]
    dims = (b, s, ctx_len)
    n_lat, n_ctx = b * s, b * ctx_len
    assert d == D_MODEL and s % ROW_TILE == 0 and ctx_len % ROW_TILE == 0
    assert ctx_len % GLA_BLOCK == 0 and s % GLA_BLOCK == 0 and n_lat % ctx_len == 0

    h = jnp.concatenate([x.reshape(n_lat, d), ctx.reshape(n_ctx, d)], axis=0)
    n_groups = ((b + 1 + SUBLANES - 1) // SUBLANES) * SUBLANES
    cvec = jnp.concatenate([c, c_ctx[None, :], jnp.zeros((n_groups - b - 1, d), F32)], axis=0)
    mod_all = _mod_table(cvec, w_mod, b_mod).reshape(depth, n_groups, N_MOD, d)

    lb_p = jax.nn.softmax(hgrn_lb_logits.astype(F32), axis=0)
    lb_all = jnp.cumsum(lb_p, axis=0) - lb_p[:1]
    tabs = _rope_tabs(s, ROW_TILE)

    for l in range(depth):
        j = l // 2
        even = l % 2 == 0
        ctx_later = any(m % 2 == 0 for m in range(l + 1, depth))
        n_rows = n_lat + n_ctx if ctx_later else n_lat
        mod = mod_all[l]
        ng1, ng2 = norm_g[l, 0][None, :], norm_g[l, 1][None, :]
        if even:
            (q, k, v, qh, kf, lgf, kb, lgb, vh, hg) = _inproj(
                h, mod, ng1, _pack_in(ab_w_in[j]), ab_g_cq[j][None, :], ab_g_ckv[j][None, :],
                _pack_q(ab_w_uq[j]), _pack_kv(ab_w_ukv[j]), lb_all[j], tabs, dims)
            a = _attention_latent(q, k, v, dims)
            if ctx_later:
                a = jnp.concatenate([a, _attention_ctx(q, k, v, dims)], axis=0)
            o_f, o_b = _gla(qh, kf, lgf, kb, lgb, vh, dims)
            h = _outproj(a, o_f, o_b, hg, hgrn_g_norm[j][None, :], ab_w_out[j].astype(BF16), h, mod,
                         dims, n_rows)
            h = _ffn(h, mod, ng2, ffn_w_gate[j].astype(BF16), ffn_w_up[j].astype(BF16),
                     ffn_w_down[j].astype(BF16), dims, n_rows)
        else:
            h = _pool(h, mod, ng1, pool_w[j].astype(BF16), pool_b[j], pool_scale[j][None, :], dims, n_rows)
            r_full = _pad_cols(moe_router[j], LANES)
            r_hi = r_full.astype(BF16)
            router = jnp.stack([r_hi, (r_full - r_hi.astype(F32)).astype(BF16)])
            last = l == depth - 1
            h = _moe(h, mod, ng2, router, moe_w_gate, moe_w_up, moe_w_down, j, dims, n_rows,
                     final_g[None, :] if last else None)
            if last:
                return h.reshape(b, s, d)
    return _final_norm(h, final_g[None, :], n_lat).reshape(b, s, d)
```

```python
import functools

import jax
import jax.numpy as jnp
from jax import lax
from jax.experimental import pallas as pl
from jax.experimental.pallas import tpu as pltpu

F32 = jnp.float32
BF16 = jnp.bfloat16

D_MODEL = 1024
GRID_W = 64
N_MOD = 6
EPS = 1e-6
N_HEADS = 8
V_HEAD = 64
QK_NOPE = 64
QK_ROPE = 32
QK_DIM = QK_NOPE + QK_ROPE
Q_LORA = 384
KV_LORA = 256
ROPE_BASE = 10000.0
MLA_W = 512
HGRN_W = 512
HGRN_K = 128
N_HGRN_HEADS = 4
POOL_WINDOWS = (2, 4, 8, 16)
POOL_C = 256
N_EXPERTS = 8

LANES = 128
SUBLANES = 8
VMEM_LIMIT_BYTES = 56 * 1024 * 1024

HEAD_PAD = LANES
QKV_W = N_HEADS * HEAD_PAD
KR_OFF = Q_LORA + KV_LORA
KRR_OFF = KR_OFF + LANES
H_OFF = KRR_OFF + LANES
IN_COLS = H_OFF + 5 * HGRN_W
GLA_CHUNK = 64
GLA_BLOCK = 256
GLA_PHASE_GROUP = 2
ROW_TILE = 256
ROUTE_SUBTILES = 4
MLP_TILE = 512
ATTN_Q_TILE = 512
ATTN_HEAD_GROUP = 2
MOE_TILE = 1024
MOE_FF_TILE = 512
MOE_FF_SPLIT = 2
LOG2E = 1.4426950408889634


def _params(*sem):
    return pltpu.CompilerParams(dimension_semantics=sem, vmem_limit_bytes=VMEM_LIMIT_BYTES)


def _dot(a, b):
    return jnp.dot(a, b, preferred_element_type=F32)


def _dot_nt(a, b):
    return lax.dot_general(a, b, (((1,), (1,)), ((), ())), preferred_element_type=F32)


def _dot_tn(a, b):
    return lax.dot_general(a, b, (((0,), (0,)), ((), ())), preferred_element_type=F32)


def _rms(x, g):
    return x * lax.rsqrt(jnp.mean(x * x, axis=-1, keepdims=True) + EPS) * g


def _silu(x):
    return x * jax.nn.sigmoid(x)


def _norm_mod(h, g, shift, scale):
    return _rms(h, g) * (1.0 + scale) + shift


def _mod_kernel(c_ref, w_ref, b_ref, o_ref):
    c = c_ref[...]
    o_ref[...] = jnp.dot(_silu(c), w_ref[...], preferred_element_type=F32,
                         precision=lax.Precision.HIGHEST) + b_ref[...]


def _mod_table(cvec, w_mod, b_mod):
    depth, d, nd = w_mod.shape
    g = cvec.shape[0]
    tn = 1536
    return pl.pallas_call(
        _mod_kernel,
        grid=(depth, nd // tn),
        in_specs=[pl.BlockSpec((g, d), lambda l, j: (0, 0)),
                  pl.BlockSpec((None, d, tn), lambda l, j: (l, 0, j)),
                  pl.BlockSpec((None, 1, tn), lambda l, j: (l, 0, j))],
        out_specs=pl.BlockSpec((None, g, tn), lambda l, j: (l, 0, j)),
        out_shape=jax.ShapeDtypeStruct((depth, g, nd), F32),
        compiler_params=_params("parallel", "parallel"),
        name="mod_table",
    )(cvec, w_mod, b_mod.reshape(depth, 1, nd))


def _inproj_kernel(h_ref, mod_ref, ng_ref, win_ref, gcq_ref, gckv_ref, wq_ref, wkv_ref,
                   lb_ref, cq_tab, sq_tab, ck_tab, sk_tab,
                   q_ref, k_ref, v_ref, qh_ref, kf_ref, lgf_ref, kb_ref, lgb_ref, vh_ref, hg_ref):
    u = _norm_mod(h_ref[...], ng_ref[...], mod_ref[0:1, :], mod_ref[1:2, :]).astype(BF16)
    p = _dot(u, win_ref[...])
    cqn = _rms(p[:, 0:Q_LORA], gcq_ref[...]).astype(BF16)
    a = _dot(cqn, wq_ref[...])
    cq = jnp.tile(cq_tab[...], (1, N_HEADS))
    sq = jnp.tile(sq_tab[...], (1, N_HEADS))
    q_ref[...] = (a[:, :QKV_W] * cq + a[:, QKV_W:] * sq).astype(BF16)
    ckvn = _rms(p[:, Q_LORA:KR_OFF], gckv_ref[...]).astype(BF16)
    kr = (p[:, KR_OFF:KRR_OFF] * ck_tab[...] + p[:, KRR_OFF:H_OFF] * sk_tab[...]).astype(BF16)
    kv = _dot(jnp.concatenate([ckvn, kr], axis=-1), wkv_ref[...])
    k_ref[...] = kv[:, :QKV_W].astype(BF16)
    lane = lax.broadcasted_iota(jnp.int32, (1, QKV_W), 1)
    ones_col = (lane % HEAD_PAD == V_HEAD).astype(F32)
    v_ref[...] = (kv[:, QKV_W:] + ones_col).astype(BF16)
    o = H_OFF
    qh_ref[...] = _silu(p[:, o:o + HGRN_W])
    kf = (1.0 - lb_ref[0:1, :]) * jax.nn.sigmoid(-p[:, o + HGRN_W:o + 2 * HGRN_W])
    kf_ref[...] = kf
    lgf_ref[...] = jnp.log1p(-kf)
    kb = (1.0 - lb_ref[1:2, :]) * jax.nn.sigmoid(-p[:, o + 2 * HGRN_W:o + 3 * HGRN_W])
    kb_ref[...] = kb
    lgb_ref[...] = jnp.log1p(-kb)
    vh_ref[...] = p[:, o + 3 * HGRN_W:o + 4 * HGRN_W]
    hg_ref[...] = p[:, o + 4 * HGRN_W:o + 5 * HGRN_W]


def _inproj(h, mod, ng, win, gcq, gckv, wq, wkv, lb, tabs, dims):
    n, d = h.shape
    b, s, ctx = dims
    tm = ROW_TILE
    n_lat_t = b * s // tm
    pos_blocks = s // tm

    def grp(i):
        return jnp.minimum(i * tm // s, b)

    def pos(i):
        return jnp.where(i < n_lat_t, i % pos_blocks, pos_blocks)

    row = lambda i: (i, 0)
    const = lambda i: (0, 0)
    tab_spec = pl.BlockSpec((tm, LANES), lambda i: (pos(i), 0))
    wide = jax.ShapeDtypeStruct((n, QKV_W), BF16)
    hg = jax.ShapeDtypeStruct((n, HGRN_W), F32)
    return pl.pallas_call(
        _inproj_kernel,
        grid=(n // tm,),
        in_specs=[pl.BlockSpec((tm, d), row),
                  pl.BlockSpec((None, N_MOD, d), lambda i: (grp(i), 0, 0)),
                  pl.BlockSpec((1, d), const),
                  pl.BlockSpec(win.shape, const),
                  pl.BlockSpec((1, Q_LORA), const),
                  pl.BlockSpec((1, KV_LORA), const),
                  pl.BlockSpec(wq.shape, const),
                  pl.BlockSpec(wkv.shape, const),
                  pl.BlockSpec((2, HGRN_W), const),
                  tab_spec, tab_spec, tab_spec, tab_spec],
        out_specs=[pl.BlockSpec((tm, QKV_W), row)] * 3 + [pl.BlockSpec((tm, HGRN_W), row)] * 7,
        out_shape=[wide] * 3 + [hg] * 7,
        compiler_params=_params("parallel"),
        name="inproj",
    )(h, mod, ng, win, gcq, gckv, wq, wkv, lb, *tabs)


def _attn_kernel(*refs, n_kv):
    q_ref = refs[0]
    k_refs = refs[1:1 + 2 * n_kv:2]
    v_refs = refs[2:2 + 2 * n_kv:2]
    o_ref = refs[-1]
    for h0 in range(0, N_HEADS, ATTN_HEAD_GROUP):
        heads = range(h0, h0 + ATTN_HEAD_GROUP)
        sls = {hd: slice(hd * HEAD_PAD, (hd + 1) * HEAD_PAD) for hd in heads}
        scores = {hd: [_dot_nt(q_ref[:, sls[hd]], k_ref[:, sls[hd]]) for k_ref in k_refs] for hd in heads}
        ms = {hd: functools.reduce(jnp.maximum, [jnp.max(sc, axis=-1, keepdims=True) for sc in scores[hd]])
              for hd in heads}
        for hd in heads:
            acc = None
            for sc, v_ref in zip(scores[hd], v_refs):
                part = _dot(jnp.exp2(sc - ms[hd]).astype(BF16), v_ref[:, sls[hd]])
                acc = part if acc is None else acc + part
            o = acc[:, :V_HEAD] / acc[:, V_HEAD:V_HEAD + 1]
            o_ref[:, hd * V_HEAD:(hd + 1) * V_HEAD] = o.astype(o_ref.dtype)


def _attention_latent(q, k, v, dims):
    b, s, ctx = dims
    tq = min(ATTN_Q_TILE, s)
    nq = s // tq
    lat = lambda bi, j: (bi, 0)
    cx = lambda bi, j: (b * s // ctx + bi, 0)
    return pl.pallas_call(
        functools.partial(_attn_kernel, n_kv=2),
        grid=(b, nq),
        in_specs=[pl.BlockSpec((tq, QKV_W), lambda bi, j: (bi * nq + j, 0)),
                  pl.BlockSpec((s, QKV_W), lat), pl.BlockSpec((s, QKV_W), lat),
                  pl.BlockSpec((ctx, QKV_W), cx), pl.BlockSpec((ctx, QKV_W), cx)],
        out_specs=pl.BlockSpec((tq, MLA_W), lambda bi, j: (bi * nq + j, 0)),
        out_shape=jax.ShapeDtypeStruct((b * s, MLA_W), BF16),
        compiler_params=_params("parallel", "arbitrary"),
        name="attn_latent",
    )(q, k, v, k, v)


def _attention_ctx(q, k, v, dims):
    b, s, ctx = dims
    cx = lambda bi: (b * s // ctx + bi, 0)
    return pl.pallas_call(
        functools.partial(_attn_kernel, n_kv=1),
        grid=(b,),
        in_specs=[pl.BlockSpec((ctx, QKV_W), cx), pl.BlockSpec((ctx, QKV_W), cx),
                  pl.BlockSpec((ctx, QKV_W), cx)],
        out_specs=pl.BlockSpec((ctx, MLA_W), lambda bi: (bi, 0)),
        out_shape=jax.ShapeDtypeStruct((b * ctx, MLA_W), BF16),
        compiler_params=_params("parallel"),
        name="attn_ctx",
    )(q, k, v)


def _bcast_block_row(g, bs, row):
    if bs == GLA_CHUNK:
        return g[row:row + 1, :]
    g3 = g.reshape(GLA_CHUNK // bs, bs, g.shape[-1])
    return jnp.broadcast_to(g3[:, row:row + 1, :], g3.shape).reshape(g.shape)


def _gla_consts(reverse):
    c = GLA_CHUNK
    t_idx = lax.broadcasted_iota(jnp.int32, (c, c), 0)
    s_idx = lax.broadcasted_iota(jnp.int32, (c, c), 1)
    pt = (c - 1 - t_idx) if reverse else t_idx
    ps = (c - 1 - s_idx) if reverse else s_idx
    tri = (ps <= pt).astype(BF16)
    levels = []
    for half in (32, 16, 8):
        bs = 2 * half
        mask = (t_idx // bs == s_idx // bs) & (pt % bs >= half) & (ps % bs < half)
        levels.append((bs, half if reverse else half - 1, False, mask))
    dmask = (t_idx // SUBLANES == s_idx // SUBLANES) & (ps <= pt)
    levels.append((SUBLANES, 4 if reverse else 3, True, dmask))
    end_row = 0 if reverse else c - 1
    return tri, levels, end_row


def _gla_kernel(qf_ref, kf_ref, lgf_ref, vf_ref, qb_ref, kb_ref, lgb_ref, vb_ref,
                of_ref, ob_ref, stf_ref, stb_ref):
    @pl.when(pl.program_id(1) == 0)
    def _():
        stf_ref[...] = jnp.zeros_like(stf_ref)
        stb_ref[...] = jnp.zeros_like(stb_ref)

    c = GLA_CHUNK
    n_chunks = GLA_BLOCK // c
    dirs = [(qf_ref, kf_ref, lgf_ref, vf_ref, of_ref, stf_ref, False),
            (qb_ref, kb_ref, lgb_ref, vb_ref, ob_ref, stb_ref, True)]
    consts = [_gla_consts(False), _gla_consts(True)]
    states = [[d[5][hd] for hd in range(N_HGRN_HEADS)] for d in dirs]
    for step0 in range(0, n_chunks, GLA_PHASE_GROUP):
      units = []
      for step in range(step0, step0 + GLA_PHASE_GROUP):
        for di, (q_ref, k_ref, lg_ref, v_ref, o_ref, _, reverse) in enumerate(dirs):
            tri, levels, end_row = consts[di]
            ci = n_chunks - 1 - step if reverse else step
            rows = slice(ci * c, (ci + 1) * c)
            lg = lg_ref[rows, :]
            hi = lg.astype(BF16)
            r1 = lg - hi.astype(F32)
            mid = r1.astype(BF16)
            lo = (r1 - mid.astype(F32)).astype(BF16)
            gc = (_dot(tri, hi) + _dot(tri, mid) + _dot(tri, lo)) * LOG2E
            for hd in range(N_HGRN_HEADS):
                hs = slice(hd * HGRN_K, (hd + 1) * HGRN_K)
                units.append(dict(di=di, hd=hd, rows=rows, hs=hs, g=gc[:, hs], o_ref=o_ref,
                                  qh=q_ref[rows, hs], kh=k_ref[rows, hs],
                                  vh=v_ref[rows, hs].astype(BF16), levels=levels, end_row=end_row,
                                  attn=jnp.zeros((c, c), F32)))
      if True:
        for li in range(4):
            for u in units:
                bs, row, diag, mask = u["levels"][li]
                g, qh, kh = u["g"], u["qh"], u["kh"]
                d = g - _bcast_block_row(g, bs, row)
                if diag:
                    qf = (qh * jnp.exp2(d)).astype(BF16)
                    kf = (kh * jnp.exp2(-d)).astype(BF16)
                else:
                    e = jnp.exp2(-jnp.abs(d))
                    qf = (qh * e).astype(BF16)
                    kf = (kh * e).astype(BF16)
                u["attn"] = jnp.where(mask, _dot_nt(qf, kf), u["attn"])
        for u in units:
            g, qh, kh, vh = u["g"], u["qh"], u["kh"], u["vh"]
            st = states[u["di"]][u["hd"]]
            g_end = g[u["end_row"]:u["end_row"] + 1, :]
            qe = (qh * jnp.exp2(g)).astype(BF16)
            ke = (kh * jnp.exp2(g_end - g)).astype(BF16)
            o = _dot_nt(qe, st.astype(BF16)) + _dot(u["attn"].astype(BF16), vh)
            u["o_ref"][u["rows"], u["hs"]] = o
            states[u["di"]][u["hd"]] = st * jnp.exp2(g_end) + _dot_tn(vh, ke)
    for di, d in enumerate(dirs):
        for hd in range(N_HGRN_HEADS):
            d[5][hd] = states[di][hd]


def _gla(qh, kf, lgf, kb, lgb, vh, dims):
    n = qh.shape[0]
    b, s, ctx = dims
    blk = GLA_BLOCK
    ncb, nsb, nlat = ctx // blk, s // blk, b * s // blk

    def fwd(bi, j):
        return (jnp.where(j < ncb, nlat + bi * ncb + j, bi * nsb + (j - ncb)), 0)

    def bwd(bi, j):
        return (jnp.where(j < ncb, nlat + bi * ncb + (ncb - 1 - j), bi * nsb + (nsb - 1 - (j - ncb))), 0)

    fs = pl.BlockSpec((blk, HGRN_W), fwd)
    bs = pl.BlockSpec((blk, HGRN_W), bwd)
    out = jax.ShapeDtypeStruct((n, HGRN_W), F32)
    return pl.pallas_call(
        _gla_kernel,
        grid=(b, ncb + nsb),
        in_specs=[fs, fs, fs, fs, bs, bs, bs, bs],
        out_specs=[fs, bs],
        out_shape=[out, out],
        scratch_shapes=[pltpu.VMEM((N_HGRN_HEADS, HGRN_K, HGRN_K), F32)] * 2,
        compiler_params=_params("parallel", "arbitrary"),
        name="gla",
    )(qh, kf, lgf, vh, qh, kb, lgb, vh)


def _outproj_kernel(a_ref, of_ref, ob_ref, hg_ref, gn_ref, w_ref, h_ref, mod_ref, o_ref):
    o = of_ref[...] + ob_ref[...]
    gate = _silu(hg_ref[...])
    ys = []
    for hd in range(N_HGRN_HEADS):
        hs = slice(hd * HGRN_K, (hd + 1) * HGRN_K)
        ys.append(_rms(o[:, hs], gn_ref[...]) * gate[:, hs])
    y = jnp.concatenate(ys, axis=-1).astype(BF16)
    mix = _dot(a_ref[...], w_ref[0:MLA_W, :]) + _dot(y, w_ref[MLA_W:, :])
    o_ref[...] = h_ref[...] + mod_ref[2:3, :] * mix


def _outproj(a, of, ob, hg, gn, w, h, mod, dims, n_rows):
    n, d = h.shape
    b, s, ctx = dims
    tm = MLP_TILE
    row = lambda i: (i, 0)
    const = lambda i: (0, 0)
    half = pl.BlockSpec((tm, HGRN_W), row)
    return pl.pallas_call(
        _outproj_kernel,
        grid=(n_rows // tm,),
        in_specs=[pl.BlockSpec((tm, MLA_W), row), half, half, half,
                  pl.BlockSpec((1, HGRN_K), const), pl.BlockSpec(w.shape, const),
                  pl.BlockSpec((tm, d), row),
                  pl.BlockSpec((None, N_MOD, d), lambda i: (jnp.minimum(i * tm // s, b), 0, 0))],
        out_specs=pl.BlockSpec((tm, d), row),
        out_shape=jax.ShapeDtypeStruct((n, d), F32),
        input_output_aliases={6: 0},
        compiler_params=_params("parallel"),
        name="outproj",
    )(a, of, ob, hg, gn, w, h, mod)


def _ffn_kernel(h_ref, mod_ref, ng_ref, wg_ref, wu_ref, wd_ref, o_ref):
    h = h_ref[...]
    v = _norm_mod(h, ng_ref[...], mod_ref[3:4, :], mod_ref[4:5, :]).astype(BF16)
    hid = (_silu(_dot(v, wg_ref[...])) * _dot(v, wu_ref[...])).astype(BF16)
    o_ref[...] = h + mod_ref[5:6, :] * _dot(hid, wd_ref[...])


def _ffn(h, mod, ng, wg, wu, wd, dims, n_rows):
    n, d = h.shape
    b, s, ctx = dims
    tm = MLP_TILE
    row = lambda i: (i, 0)
    const = lambda i: (0, 0)
    return pl.pallas_call(
        _ffn_kernel,
        grid=(n_rows // tm,),
        in_specs=[pl.BlockSpec((tm, d), row),
                  pl.BlockSpec((None, N_MOD, d), lambda i: (jnp.minimum(i * tm // s, b), 0, 0)),
                  pl.BlockSpec((1, d), const),
                  pl.BlockSpec(wg.shape, const, pipeline_mode=pl.Buffered(1)),
                  pl.BlockSpec(wu.shape, const, pipeline_mode=pl.Buffered(1)),
                  pl.BlockSpec(wd.shape, const, pipeline_mode=pl.Buffered(1))],
        out_specs=pl.BlockSpec((tm, d), row),
        out_shape=jax.ShapeDtypeStruct((n, d), F32),
        input_output_aliases={0: 0},
        compiler_params=_params("parallel"),
        name="ffn",
    )(h, mod, ng, wg, wu, wd)


def _pool_kernel(h_ref, prev_ref, next_ref, mod_ref, ng_ref, w_ref, b_ref, sc_ref, o_ref, ext_ref,
                 *, tm, seq_tiles_lat, n_lat_tiles, seq_tiles_ctx):
    i = pl.program_id(0)
    halo = SUBLANES
    shift, scale = mod_ref[0:1, :], mod_ref[1:2, :]
    in_lat = i < n_lat_tiles
    j = jnp.where(in_lat, i % seq_tiles_lat, (i - n_lat_tiles) % seq_tiles_ctx)
    n_seq_tiles = jnp.where(in_lat, seq_tiles_lat, seq_tiles_ctx)
    h = h_ref[...]
    u = _norm_mod(h, ng_ref[...], shift, scale)
    up = _norm_mod(prev_ref[...], ng_ref[...], shift, scale)
    un = _norm_mod(next_ref[...], ng_ref[...], shift, scale)
    ext_ref[0:halo, :] = jnp.where(j > 0, up, 0.0)
    ext_ref[halo:halo + tm, :] = u
    ext_ref[halo + tm:, :] = jnp.where(j < n_seq_tiles - 1, un, 0.0)
    pos = j * tm + lax.broadcasted_iota(jnp.int32, (tm, 1), 0)
    t_len = n_seq_tiles * tm
    outs = []
    for g, win in enumerate(POOL_WINDOWS):
        cs = slice(g * POOL_C, (g + 1) * POOL_C)
        acc = None
        for dlt in range(-(win // 2), win // 2):
            part = ext_ref[halo + dlt:halo + dlt + tm, cs]
            acc = part if acc is None else acc + part
        lo = jnp.maximum(pos - win // 2, 0)
        hi = jnp.minimum(pos - win // 2 + win, t_len)
        cnt = (hi - lo).astype(F32)
        pooled = (acc / cnt - u[:, cs]).astype(BF16)
        outs.append(_dot(pooled, w_ref[g]) + b_ref[g:g + 1, :])
    y = jnp.concatenate(outs, axis=-1) * sc_ref[...]
    o_ref[...] = h + mod_ref[2:3, :] * y


def _pool(h, mod, ng, w, bias, scale, dims, n_rows):
    n, d = h.shape
    b, s, ctx = dims
    tm = ROW_TILE
    hb = tm // SUBLANES
    n_blk8 = n // SUBLANES
    row = lambda i: (i, 0)
    const = lambda i: (0, 0)
    kern = functools.partial(_pool_kernel, tm=tm, seq_tiles_lat=s // tm, n_lat_tiles=b * s // tm,
                             seq_tiles_ctx=ctx // tm)
    return pl.pallas_call(
        kern,
        grid=(n_rows // tm,),
        in_specs=[pl.BlockSpec((tm, d), row),
                  pl.BlockSpec((SUBLANES, d), lambda i: (jnp.maximum(i * hb - 1, 0), 0)),
                  pl.BlockSpec((SUBLANES, d), lambda i: (jnp.minimum((i + 1) * hb, n_blk8 - 1), 0)),
                  pl.BlockSpec((None, N_MOD, d), lambda i: (jnp.minimum(i * tm // s, b), 0, 0)),
                  pl.BlockSpec((1, d), const),
                  pl.BlockSpec(w.shape, lambda i: (0, 0, 0)),
                  pl.BlockSpec(bias.shape, const),
                  pl.BlockSpec((1, d), const)],
        out_specs=pl.BlockSpec((tm, d), row),
        out_shape=jax.ShapeDtypeStruct((n_rows, d), F32),
        scratch_shapes=[pltpu.VMEM((tm + 2 * SUBLANES, d), F32)],
        compiler_params=_params("parallel"),
        name="pool",
    )(h, h, h, mod, ng, w, bias, scale)


def _route_kernel(h_ref, mod_ref, ng_ref, r_ref, v_ref, idx_ref, wt_ref, lst_ref, base_ref, run_ref):
    rt = ROW_TILE
    subs = [slice(k * rt, (k + 1) * rt) for k in range(h_ref.shape[0] // rt)]

    @pl.when(pl.program_id(0) == 0)
    def _():
        run_ref[...] = jnp.zeros_like(run_ref)

    lane = lax.broadcasted_iota(jnp.int32, (rt, LANES), 1)
    ti = lax.broadcasted_iota(jnp.int32, (rt, rt), 0)
    tj = lax.broadcasted_iota(jnp.int32, (rt, rt), 1)
    neg = jnp.float32(-jnp.inf)
    vs = [_norm_mod(h_ref[sl, :], ng_ref[...], mod_ref[3:4, :], mod_ref[4:5, :]) for sl in subs]
    v_his = [v.astype(BF16) for v in vs]
    half = vs[0].shape[-1] // 2
    for sl, v_hi in zip(subs, v_his):
        vb = pltpu.bitcast(v_hi.astype(F32), jnp.uint32)
        v_ref[sl, :] = (vb[:, :half] >> 16) | (vb[:, half:] & jnp.uint32(0xFFFF0000))
    logits = []
    for v, v_hi in zip(vs, v_his):
        v_lo = (v - v_hi.astype(F32)).astype(BF16)
        lg = _dot(v_hi, r_ref[0]) + (_dot(v_lo, r_ref[0]) + _dot(v_hi, r_ref[1]))
        logits.append(jnp.where(lane < N_EXPERTS, lg, neg))
    m1s = [jnp.max(lg, axis=-1, keepdims=True) for lg in logits]
    i1s = [jnp.min(jnp.where(lg == m1, lane, LANES), axis=-1, keepdims=True) for lg, m1 in zip(logits, m1s)]
    rests = [jnp.where(lane == i1, neg, lg) for lg, i1 in zip(logits, i1s)]
    m2s = [jnp.max(r, axis=-1, keepdims=True) for r in rests]
    i2s = [jnp.min(jnp.where(r == m2, lane, LANES), axis=-1, keepdims=True) for r, m2 in zip(rests, m2s)]
    for sl, m1, m2 in zip(subs, m1s, m2s):
        e2 = jnp.exp(m2 - m1)
        wt_ref[sl, :] = jnp.where(lane == 0, 1.0 / (1.0 + e2), jnp.where(lane == 1, e2 / (1.0 + e2), 0.0))
    ohs = [((lane == i1) | (lane == i2)).astype(BF16) for i1, i2 in zip(i1s, i2s)]
    befores = [_dot((tj < ti).astype(BF16), oh) for oh in ohs]
    befores_t = [_dot_tn(oh, (ti < tj).astype(BF16)) for oh in ohs]
    choses_t = [_dot_tn(oh, (ti == tj).astype(BF16)) for oh in ohs]
    run = run_ref[...]
    for k, sl in enumerate(subs):
        pos = run + befores[k]
        r1 = jnp.sum(jnp.where(lane == i1s[k], pos, 0.0), axis=-1, keepdims=True)
        r2 = jnp.sum(jnp.where(lane == i2s[k], pos, 0.0), axis=-1, keepdims=True)
        idx_ref[sl, :] = jnp.where(lane == 0, i1s[k], jnp.where(lane == 1, i2s[k], jnp.where(
            lane == 2, r1.astype(jnp.int32), jnp.where(lane == 3, r2.astype(jnp.int32), 0))))
        base_ref[k] = run.astype(jnp.int32)
        run = run + jnp.sum(ohs[k].astype(F32), axis=0, keepdims=True)
    run_ref[...] = run
    ids = lax.broadcasted_iota(jnp.int32, (rt, LANES), 0).astype(BF16)
    lsts = [jnp.zeros((rt, LANES), F32) for _ in subs]
    for e in range(N_EXPERTS):
        for k in range(len(subs)):
            sel = (befores_t[k][e:e + 1, :] == ti.astype(F32)) & (choses_t[k][e:e + 1, :] > 0.5)
            lsts[k] = jnp.where(lane == e, _dot(sel.astype(BF16), ids), lsts[k])
    for sl, lst in zip(subs, lsts):
        lst_ref[sl, :] = lst.astype(jnp.int32)


def _route(h, mod, ng, router, dims, n_rows):
    n, d = h.shape
    b, s, ctx = dims
    assert ROW_TILE <= 256
    nsub = ROUTE_SUBTILES
    while s % (nsub * ROW_TILE) or (n_rows - b * s) % (nsub * ROW_TILE):
        nsub //= 2
    tm = nsub * ROW_TILE
    nt = n_rows // ROW_TILE
    row = lambda i: (i, 0)
    const = lambda i: (0, 0)
    return pl.pallas_call(
        _route_kernel,
        grid=(n_rows // tm,),
        in_specs=[pl.BlockSpec((tm, d), row),
                  pl.BlockSpec((None, N_MOD, d), lambda i: (jnp.minimum(i * tm // s, b), 0, 0)),
                  pl.BlockSpec((1, d), const),
                  pl.BlockSpec((2, d, LANES), lambda i: (0, 0, 0))],
        out_specs=[pl.BlockSpec((tm, d // 2), row), pl.BlockSpec((tm, LANES), row),
                   pl.BlockSpec((tm, LANES), row), pl.BlockSpec((tm, LANES), row),
                   pl.BlockSpec((nsub, 1, LANES), lambda i: (i, 0, 0))],
        out_shape=[jax.ShapeDtypeStruct((n_rows, d // 2), jnp.uint32),
                   jax.ShapeDtypeStruct((n_rows, LANES), jnp.int32),
                   jax.ShapeDtypeStruct((n_rows, LANES), F32),
                   jax.ShapeDtypeStruct((n_rows, LANES), jnp.int32),
                   jax.ShapeDtypeStruct((nt, 1, LANES), jnp.int32)],
        scratch_shapes=[pltpu.VMEM((1, LANES), F32)],
        compiler_params=_params("arbitrary"),
        name="route",
    )(h, mod, ng, router)


def _moe_kernel(te_ref, nu_ref, x_ref, wg_ref, wu_ref, wd_ref, o_ref, xs_ref):
    i = pl.program_id(0)
    f = pl.program_id(1)

    @pl.when((i < nu_ref[0]) & (f == 0))
    def _():
        half = x_ref.shape[-1]
        xw = x_ref[...]
        xs_ref[:, :half] = pltpu.bitcast(xw << 16, F32).astype(BF16)
        xs_ref[:, half:] = pltpu.bitcast(xw & jnp.uint32(0xFFFF0000), F32).astype(BF16)
        o_ref[...] = jnp.zeros_like(o_ref)

    @pl.when(i < nu_ref[0])
    def _():
        x = xs_ref[...]
        w = wg_ref.shape[-1] // MOE_FF_SPLIT
        hids = []
        for c in range(MOE_FF_SPLIT):
            cs = slice(c * w, (c + 1) * w)
            hids.append((_silu(_dot(x, wg_ref[:, cs].astype(BF16))) *
                         _dot(x, wu_ref[:, cs].astype(BF16))).astype(BF16))
        part = None
        for c in range(MOE_FF_SPLIT):
            p = _dot(hids[c], wd_ref[c * w:(c + 1) * w, :].astype(BF16))
            part = p if part is None else part + p
        o_ref[...] += part


def _moe_experts(x_sorted, tile_expert, n_used, wg, wu, wd, layer):
    npad, dw = x_sorted.shape
    d = 2 * dw
    tm, tf = MOE_TILE, MOE_FF_TILE
    dff = wg.shape[-1]
    grid_spec = pltpu.PrefetchScalarGridSpec(
        num_scalar_prefetch=2,
        grid=(npad // tm, dff // tf),
        in_specs=[pl.BlockSpec((tm, dw), lambda i, f, te, nu: (i, 0)),
                  pl.BlockSpec((None, None, d, tf), lambda i, f, te, nu: (layer, te[i], 0, f)),
                  pl.BlockSpec((None, None, d, tf), lambda i, f, te, nu: (layer, te[i], 0, f)),
                  pl.BlockSpec((None, None, tf, d), lambda i, f, te, nu: (layer, te[i], f, 0))],
        out_specs=pl.BlockSpec((tm, d), lambda i, f, te, nu: (i, 0)),
        scratch_shapes=[pltpu.VMEM((tm, d), BF16)],
    )
    return pl.pallas_call(
        _moe_kernel,
        grid_spec=grid_spec,
        out_shape=jax.ShapeDtypeStruct((npad, d), F32),
        compiler_params=_params("parallel", "arbitrary"),
        name="moe_experts",
    )(tile_expert, n_used, x_sorted, wg, wu, wd)


def _combine_kernel(h_ref, y0_ref, y1_ref, wt_ref, mod_ref, *rest):
    y = wt_ref[:, 0:1] * y0_ref[...] + wt_ref[:, 1:2] * y1_ref[...]
    h = h_ref[...] + mod_ref[5:6, :] * y
    if len(rest) == 2:
        fg_ref, o_ref = rest
        o_ref[...] = _rms(h, fg_ref[...])
    else:
        rest[0][...] = h


def _combine(h, y0, y1, wt, mod, dims, n_rows, final_g=None):
    n, d = h.shape
    b, s, ctx = dims
    tm = ROW_TILE
    row = lambda i: (i, 0)
    in_specs = [pl.BlockSpec((tm, d), row), pl.BlockSpec((tm, d), row), pl.BlockSpec((tm, d), row),
                pl.BlockSpec((tm, LANES), row),
                pl.BlockSpec((None, N_MOD, d), lambda i: (jnp.minimum(i * tm // s, b), 0, 0))]
    args = [h, y0, y1, wt, mod]
    if final_g is not None:
        in_specs.append(pl.BlockSpec((1, d), lambda i: (0, 0)))
        args.append(final_g)
    return pl.pallas_call(
        _combine_kernel,
        grid=(n_rows // tm,),
        in_specs=in_specs,
        out_specs=pl.BlockSpec((tm, d), row),
        out_shape=jax.ShapeDtypeStruct((n_rows if final_g is not None else n, d), F32),
        input_output_aliases={} if final_g is not None else {0: 0},
        compiler_params=_params("parallel"),
        name="moe_combine",
    )(*args)


def _moe(h, mod, ng, router, wg, wu, wd, layer, dims, n_rows, final_g=None):
    tm, rt = MOE_TILE, ROW_TILE
    v, idx, wt, lst, base = _route(h, mod, ng, router, dims, n_rows)
    base = base[:, 0, :N_EXPERTS]
    e_sel, pos = idx[:, 0:2], idx[:, 2:4]
    last = e_sel[-rt:]
    counts = base[-1] + jnp.sum(last[:, :, None] == jnp.arange(N_EXPERTS), axis=(0, 1))
    tiles_per = (counts + tm - 1) // tm
    tile_end = jnp.cumsum(tiles_per)
    group_start = (tile_end - tiles_per) * tm
    slot = group_start[e_sel] + pos
    n_tiles = 2 * n_rows // tm + N_EXPERTS
    tile_expert = jnp.minimum(jnp.sum(jnp.arange(n_tiles)[:, None] >= tile_end[None, :], axis=1),
                              N_EXPERTS - 1).astype(jnp.int32)
    n_used = tile_end[-1:].astype(jnp.int32)
    rank = (jnp.arange(n_tiles * tm, dtype=jnp.int32).reshape(n_tiles, tm)
            - group_start[tile_expert][:, None])
    base_te = base.T[tile_expert]
    owns = base_te[:, None, :] <= rank[:, :, None]
    rtile = jnp.sum(owns, axis=-1) - 1
    local = rank - jnp.max(jnp.where(owns, base_te[:, None, :], 0), axis=-1)
    flat = jnp.clip((rtile * rt + local) * N_EXPERTS + tile_expert[:, None], 0, n_rows * N_EXPERTS - 1)
    src = rtile * rt + jnp.take(lst[:, :N_EXPERTS].reshape(-1), flat, mode="clip")
    spread = jnp.arange(n_tiles * tm, dtype=jnp.int32).reshape(n_tiles, tm) % n_rows
    src = jnp.where(rank < counts[tile_expert][:, None], src, spread).reshape(-1)
    x_sorted = jnp.take(v, src, axis=0, mode="clip")
    out_sorted = _moe_experts(x_sorted, tile_expert, n_used, wg, wu, wd, layer)
    y0 = jnp.take(out_sorted, slot[:, 0], axis=0, mode="clip")
    y1 = jnp.take(out_sorted, slot[:, 1], axis=0, mode="clip")
    return _combine(h, y0, y1, wt, mod, dims, n_rows, final_g)


def _final_kernel(h_ref, g_ref, o_ref):
    o_ref[...] = _rms(h_ref[...], g_ref[...])


def _final_norm(h, g, n_rows):
    n, d = h.shape
    tm = ROW_TILE
    row = lambda i: (i, 0)
    return pl.pallas_call(
        _final_kernel,
        grid=(n_rows // tm,),
        in_specs=[pl.BlockSpec((tm, d), row), pl.BlockSpec((1, d), lambda i: (0, 0))],
        out_specs=pl.BlockSpec((tm, d), row),
        out_shape=jax.ShapeDtypeStruct((n_rows, d), F32),
        compiler_params=_params("parallel"),
        name="final_norm",
    )(h, g)


def _rot_cols(w):
    ws = w.reshape(w.shape[:-1] + (2, 2, QK_ROPE // 4))
    return jnp.stack([-ws[..., 1, :], ws[..., 0, :]], axis=-2).reshape(w.shape)


def _pad_cols(w, width):
    return jnp.pad(w, ((0, 0), (0, width - w.shape[-1])))


def _pack_in(w_in):
    cq = w_in[:, :Q_LORA]
    ckv = w_in[:, Q_LORA:KR_OFF]
    kr = w_in[:, KR_OFF:KR_OFF + QK_ROPE]
    rest = w_in[:, KR_OFF + QK_ROPE:]
    return jnp.concatenate([cq, ckv, _pad_cols(kr, LANES), _pad_cols(_rot_cols(kr), LANES), rest],
                           axis=-1).astype(BF16)


def _pack_q(w_uq):
    w = w_uq.reshape(Q_LORA, N_HEADS, QK_DIM)
    nope, rope = w[..., :QK_NOPE], w[..., QK_NOPE:]
    z = jnp.zeros((Q_LORA, N_HEADS, HEAD_PAD - QK_DIM), w.dtype)
    plain = jnp.concatenate([nope, rope, z], axis=-1).reshape(Q_LORA, QKV_W)
    rot = jnp.concatenate([jnp.zeros_like(nope), _rot_cols(rope), z], axis=-1).reshape(Q_LORA, QKV_W)
    return jnp.concatenate([plain, rot], axis=-1).astype(BF16)


def _pack_kv(w_ukv):
    w = w_ukv.reshape(KV_LORA, N_HEADS, QK_NOPE + V_HEAD)
    z = jnp.zeros((KV_LORA, N_HEADS, HEAD_PAD - QK_NOPE), w.dtype)
    k_top = jnp.concatenate([w[..., :QK_NOPE], z], axis=-1).reshape(KV_LORA, QKV_W)
    v_top = jnp.concatenate([w[..., QK_NOPE:], z], axis=-1).reshape(KV_LORA, QKV_W)
    place = jnp.zeros((LANES, N_HEADS, HEAD_PAD), w.dtype)
    ii = jnp.arange(QK_ROPE)
    place = place.at[ii, :, QK_NOPE + ii].set(1.0).reshape(LANES, QKV_W)
    k_all = jnp.concatenate([k_top, place], axis=0)
    v_all = jnp.concatenate([v_top, jnp.zeros((LANES, QKV_W), w.dtype)], axis=0)
    return jnp.concatenate([k_all, v_all], axis=-1).astype(BF16)


def _rope_tabs(s, tm):
    rows = s // GRID_W
    row = jnp.repeat(jnp.arange(rows, dtype=F32), GRID_W)
    col = jnp.tile(jnp.arange(GRID_W, dtype=F32), rows)
    half = QK_ROPE // 2
    inv = 1.0 / (ROPE_BASE ** (jnp.arange(0, half, 2, dtype=F32) / half))
    ar = row[:, None] * inv[None, :]
    ac = col[:, None] * inv[None, :]
    ang = jnp.concatenate([ar, ar, ac, ac], axis=-1)
    cos = jnp.concatenate([jnp.cos(ang), jnp.ones((tm, QK_ROPE), F32)], axis=0)
    sin = jnp.concatenate([jnp.sin(ang), jnp.zeros((tm, QK_ROPE), F32)], axis=0)
    t = cos.shape[0]
    scale = QK_DIM ** -0.5 * LOG2E
    cq = jnp.concatenate([jnp.full((t, QK_NOPE), scale, F32), cos * scale,
                          jnp.zeros((t, HEAD_PAD - QK_DIM), F32)], axis=-1)
    sq = jnp.concatenate([jnp.zeros((t, QK_NOPE), F32), sin * scale,
                          jnp.zeros((t, HEAD_PAD - QK_DIM), F32)], axis=-1)
    ck = _pad_cols(cos, LANES)
    sk = _pad_cols(sin, LANES)
    return cq, sq, ck, sk


def kernel(x, c, ctx, c_ctx, w_mod, b_mod, norm_g, final_g, ab_w_in, ab_g_cq, ab_g_ckv, ab_w_uq, ab_w_ukv, hgrn_lb_logits, hgrn_g_norm, ab_w_out, ffn_w_gate, ffn_w_up, ffn_w_down, pool_w, pool_b, pool_scale, moe_router, moe_w_gate, moe_w_up, moe_w_down):
    b, s, d = x.shape
    ctx_len = ctx.shape[1]
    depth = w_mod.shape[0]
    dims = (b, s, ctx_len)
    n_lat, n_ctx = b * s, b * ctx_len
    assert d == D_MODEL and s % ROW_TILE == 0 and ctx_len % ROW_TILE == 0
    assert ctx_len % GLA_BLOCK == 0 and s % GLA_BLOCK == 0 and n_lat % ctx_len == 0

    h = jnp.concatenate([x.reshape(n_lat, d), ctx.reshape(n_ctx, d)], axis=0)
    n_groups = ((b + 1 + SUBLANES - 1) // SUBLANES) * SUBLANES
    cvec = jnp.concatenate([c, c_ctx[None, :], jnp.zeros((n_groups - b - 1, d), F32)], axis=0)
    mod_all = _mod_table(cvec, w_mod, b_mod).reshape(depth, n_groups, N_MOD, d)

    lb_p = jax.nn.softmax(hgrn_lb_logits.astype(F32), axis=0)
    lb_all = jnp.cumsum(lb_p, axis=0) - lb_p[:1]
    tabs = _rope_tabs(s, ROW_TILE)

    for l in range(depth):
        j = l // 2
        even = l % 2 == 0
        ctx_later = any(m % 2 == 0 for m in range(l + 1, depth))
        n_rows = n_lat + n_ctx if ctx_later else n_lat
        mod = mod_all[l]
        ng1, ng2 = norm_g[l, 0][None, :], norm_g[l, 1][None, :]
        if even:
            (q, k, v, qh, kf, lgf, kb, lgb, vh, hg) = _inproj(
                h, mod, ng1, _pack_in(ab_w_in[j]), ab_g_cq[j][None, :], ab_g_ckv[j][None, :],
                _pack_q(ab_w_uq[j]), _pack_kv(ab_w_ukv[j]), lb_all[j], tabs, dims)
            a = _attention_latent(q, k, v, dims)
            if ctx_later:
                a = jnp.concatenate([a, _attention_ctx(q, k, v, dims)], axis=0)
            o_f, o_b = _gla(qh, kf, lgf, kb, lgb, vh, dims)
            h = _outproj(a, o_f, o_b, hg, hgrn_g_norm[j][None, :], ab_w_out[j].astype(BF16), h, mod,
                         dims, n_rows)
            h = _ffn(h, mod, ng2, ffn_w_gate[j].astype(BF16), ffn_w_up[j].astype(BF16),
                     ffn_w_down[j].astype(BF16), dims, n_rows)
        else:
            h = _pool(h, mod, ng1, pool_w[j].astype(BF16), pool_b[j], pool_scale[j][None, :], dims, n_rows)
            r_full = _pad_cols(moe_router[j], LANES)
            r_hi = r_full.astype(BF16)
            router = jnp.stack([r_hi, (r_full - r_hi.astype(F32)).astype(BF16)])
            last = l == depth - 1
            h = _moe(h, mod, ng2, router, moe_w_gate, moe_w_up, moe_w_down, j, dims, n_rows,
                     final_g[None, :] if last else None)
            if last:
                return h.reshape(b, s, d)
    return _final_norm(h, final_g[None, :], n_lat).reshape(b, s, d)
```

```python
import functools

import jax
import jax.numpy as jnp
from jax import lax
from jax.experimental import pallas as pl
from jax.experimental.pallas import tpu as pltpu

F32 = jnp.float32
BF16 = jnp.bfloat16

D_MODEL = 1024
GRID_W = 64
N_MOD = 6
EPS = 1e-6
N_HEADS = 8
V_HEAD = 64
QK_NOPE = 64
QK_ROPE = 32
QK_DIM = QK_NOPE + QK_ROPE
Q_LORA = 384
KV_LORA = 256
ROPE_BASE = 10000.0
MLA_W = 512
HGRN_W = 512
HGRN_K = 128
N_HGRN_HEADS = 4
POOL_WINDOWS = (2, 4, 8, 16)
POOL_C = 256
N_EXPERTS = 8

LANES = 128
SUBLANES = 8
VMEM_LIMIT_BYTES = 56 * 1024 * 1024

HEAD_PAD = LANES
QKV_W = N_HEADS * HEAD_PAD
KR_OFF = Q_LORA + KV_LORA
H_OFF = KR_OFF + LANES
IN_COLS = H_OFF + 5 * HGRN_W
GLA_CHUNK = 64
GLA_BLOCK = 256
GLA_PHASE_GROUP = 2
ROW_TILE = 256
ROUTE_SUBTILES = 4
MLP_TILE = 512
ATTN_Q_TILE = 512
ATTN_HEAD_GROUP = 2
MOE_TILE = 1024
MOE_FF_TILE = 512
MOE_FF_SPLIT = 2
LOG2E = 1.4426950408889634


def _params(*sem):
    return pltpu.CompilerParams(dimension_semantics=sem, vmem_limit_bytes=VMEM_LIMIT_BYTES)


def _dot(a, b):
    return jnp.dot(a, b, preferred_element_type=F32)


def _dot_nt(a, b):
    return lax.dot_general(a, b, (((1,), (1,)), ((), ())), preferred_element_type=F32)


def _dot_tn(a, b):
    return lax.dot_general(a, b, (((0,), (0,)), ((), ())), preferred_element_type=F32)


def _rms(x, g):
    return x * lax.rsqrt(jnp.mean(x * x, axis=-1, keepdims=True) + EPS) * g


def _silu(x):
    return x * jax.nn.sigmoid(x)


def _norm_mod(h, g, shift, scale):
    return _rms(h, g) * (1.0 + scale) + shift


def _mod_kernel(c_ref, w_ref, b_ref, o_ref):
    c = c_ref[...]
    o_ref[...] = jnp.dot(_silu(c), w_ref[...], preferred_element_type=F32,
                         precision=lax.Precision.HIGHEST) + b_ref[...]


def _mod_table(cvec, w_mod, b_mod):
    depth, d, nd = w_mod.shape
    g = cvec.shape[0]
    tn = 1536
    return pl.pallas_call(
        _mod_kernel,
        grid=(depth, nd // tn),
        in_specs=[pl.BlockSpec((g, d), lambda l, j: (0, 0)),
                  pl.BlockSpec((None, d, tn), lambda l, j: (l, 0, j)),
                  pl.BlockSpec((None, 1, tn), lambda l, j: (l, 0, j))],
        out_specs=pl.BlockSpec((None, g, tn), lambda l, j: (l, 0, j)),
        out_shape=jax.ShapeDtypeStruct((depth, g, nd), F32),
        compiler_params=_params("parallel", "parallel"),
        name="mod_table",
    )(cvec, w_mod, b_mod.reshape(depth, 1, nd))


def _inproj_kernel(h_ref, mod_ref, ng_ref, win_ref, gcq_ref, gckv_ref, wq_ref, wkv_ref,
                   lb_ref, cq_tab, s1q_tab, s2q_tab, ck_tab, s1k_tab, s2k_tab,
                   q_ref, k_ref, v_ref, qh_ref, kf_ref, lgf_ref, kb_ref, lgb_ref, vh_ref, hg_ref):
    half_rot = QK_ROPE // 4

    def rope(x, c_tab, s1_tab, s2_tab):
        return (x * c_tab[...] + pltpu.roll(x, LANES - half_rot, 1) * s1_tab[...]
                + pltpu.roll(x, half_rot, 1) * s2_tab[...])

    u = _norm_mod(h_ref[...], ng_ref[...], mod_ref[0:1, :], mod_ref[1:2, :]).astype(BF16)
    p = _dot(u, win_ref[...])
    cqn = _rms(p[:, 0:Q_LORA], gcq_ref[...]).astype(BF16)
    a = _dot(cqn, wq_ref[...])
    q_ref[...] = jnp.concatenate(
        [rope(a[:, hd * HEAD_PAD:(hd + 1) * HEAD_PAD], cq_tab, s1q_tab, s2q_tab) for hd in range(N_HEADS)],
        axis=-1).astype(BF16)
    ckvn = _rms(p[:, Q_LORA:KR_OFF], gckv_ref[...]).astype(BF16)
    kv = _dot(ckvn, wkv_ref[...])
    kr = pltpu.roll(rope(p[:, KR_OFF:H_OFF], ck_tab, s1k_tab, s2k_tab), QK_NOPE, 1)
    k_ref[...] = jnp.concatenate(
        [kv[:, hd * HEAD_PAD:(hd + 1) * HEAD_PAD] + kr for hd in range(N_HEADS)], axis=-1).astype(BF16)
    lane = lax.broadcasted_iota(jnp.int32, (1, QKV_W), 1)
    ones_col = (lane % HEAD_PAD == V_HEAD).astype(F32)
    v_ref[...] = (kv[:, QKV_W:] + ones_col).astype(BF16)
    o = H_OFF
    qh_ref[...] = _silu(p[:, o:o + HGRN_W])
    kf = (1.0 - lb_ref[0:1, :]) * jax.nn.sigmoid(-p[:, o + HGRN_W:o + 2 * HGRN_W])
    kf_ref[...] = kf
    lgf_ref[...] = jnp.log1p(-kf)
    kb = (1.0 - lb_ref[1:2, :]) * jax.nn.sigmoid(-p[:, o + 2 * HGRN_W:o + 3 * HGRN_W])
    kb_ref[...] = kb
    lgb_ref[...] = jnp.log1p(-kb)
    vh_ref[...] = p[:, o + 3 * HGRN_W:o + 4 * HGRN_W]
    hg_ref[...] = p[:, o + 4 * HGRN_W:o + 5 * HGRN_W]


def _inproj(h, mod, ng, win, gcq, gckv, wq, wkv, lb, tabs, dims):
    n, d = h.shape
    b, s, ctx = dims
    tm = ROW_TILE
    n_lat_t = b * s // tm
    pos_blocks = s // tm

    def grp(i):
        return jnp.minimum(i * tm // s, b)

    def pos(i):
        return jnp.where(i < n_lat_t, i % pos_blocks, pos_blocks)

    row = lambda i: (i, 0)
    const = lambda i: (0, 0)
    tab_spec = pl.BlockSpec((tm, LANES), lambda i: (pos(i), 0))
    wide = jax.ShapeDtypeStruct((n, QKV_W), BF16)
    hg = jax.ShapeDtypeStruct((n, HGRN_W), F32)
    return pl.pallas_call(
        _inproj_kernel,
        grid=(n // tm,),
        in_specs=[pl.BlockSpec((tm, d), row),
                  pl.BlockSpec((None, N_MOD, d), lambda i: (grp(i), 0, 0)),
                  pl.BlockSpec((1, d), const),
                  pl.BlockSpec(win.shape, const),
                  pl.BlockSpec((1, Q_LORA), const),
                  pl.BlockSpec((1, KV_LORA), const),
                  pl.BlockSpec(wq.shape, const),
                  pl.BlockSpec(wkv.shape, const),
                  pl.BlockSpec((2, HGRN_W), const),
                  tab_spec, tab_spec, tab_spec, tab_spec, tab_spec, tab_spec],
        out_specs=[pl.BlockSpec((tm, QKV_W), row)] * 3 + [pl.BlockSpec((tm, HGRN_W), row)] * 7,
        out_shape=[wide] * 3 + [hg] * 7,
        compiler_params=_params("parallel"),
        name="inproj",
    )(h, mod, ng, win, gcq, gckv, wq, wkv, lb, *tabs)


def _attn_kernel(*refs, n_kv):
    q_ref = refs[0]
    k_refs = refs[1:1 + 2 * n_kv:2]
    v_refs = refs[2:2 + 2 * n_kv:2]
    o_ref = refs[-1]
    for h0 in range(0, N_HEADS, ATTN_HEAD_GROUP):
        heads = range(h0, h0 + ATTN_HEAD_GROUP)
        sls = {hd: slice(hd * HEAD_PAD, (hd + 1) * HEAD_PAD) for hd in heads}
        scores = {hd: [_dot_nt(q_ref[:, sls[hd]], k_ref[:, sls[hd]]) for k_ref in k_refs] for hd in heads}
        ms = {hd: functools.reduce(jnp.maximum, [jnp.max(sc, axis=-1, keepdims=True) for sc in scores[hd]])
              for hd in heads}
        for hd in heads:
            acc = None
            for sc, v_ref in zip(scores[hd], v_refs):
                part = _dot(jnp.exp2(sc - ms[hd]).astype(BF16), v_ref[:, sls[hd]])
                acc = part if acc is None else acc + part
            o = acc[:, :V_HEAD] / acc[:, V_HEAD:V_HEAD + 1]
            o_ref[:, hd * V_HEAD:(hd + 1) * V_HEAD] = o.astype(o_ref.dtype)


def _attention_latent(q, k, v, dims):
    b, s, ctx = dims
    tq = min(ATTN_Q_TILE, s)
    nq = s // tq
    lat = lambda bi, j: (bi, 0)
    cx = lambda bi, j: (b * s // ctx + bi, 0)
    return pl.pallas_call(
        functools.partial(_attn_kernel, n_kv=2),
        grid=(b, nq),
        in_specs=[pl.BlockSpec((tq, QKV_W), lambda bi, j: (bi * nq + j, 0)),
                  pl.BlockSpec((s, QKV_W), lat), pl.BlockSpec((s, QKV_W), lat),
                  pl.BlockSpec((ctx, QKV_W), cx), pl.BlockSpec((ctx, QKV_W), cx)],
        out_specs=pl.BlockSpec((tq, MLA_W), lambda bi, j: (bi * nq + j, 0)),
        out_shape=jax.ShapeDtypeStruct((b * s, MLA_W), BF16),
        compiler_params=_params("parallel", "arbitrary"),
        name="attn_latent",
    )(q, k, v, k, v)


def _attention_ctx(q, k, v, dims):
    b, s, ctx = dims
    cx = lambda bi: (b * s // ctx + bi, 0)
    return pl.pallas_call(
        functools.partial(_attn_kernel, n_kv=1),
        grid=(b,),
        in_specs=[pl.BlockSpec((ctx, QKV_W), cx), pl.BlockSpec((ctx, QKV_W), cx),
                  pl.BlockSpec((ctx, QKV_W), cx)],
        out_specs=pl.BlockSpec((ctx, MLA_W), lambda bi: (bi, 0)),
        out_shape=jax.ShapeDtypeStruct((b * ctx, MLA_W), BF16),
        compiler_params=_params("parallel"),
        name="attn_ctx",
    )(q, k, v)


def _bcast_block_row(g, bs, row):
    if bs == GLA_CHUNK:
        return g[row:row + 1, :]
    g3 = g.reshape(GLA_CHUNK // bs, bs, g.shape[-1])
    return jnp.broadcast_to(g3[:, row:row + 1, :], g3.shape).reshape(g.shape)


def _gla_consts(reverse):
    c = GLA_CHUNK
    t_idx = lax.broadcasted_iota(jnp.int32, (c, c), 0)
    s_idx = lax.broadcasted_iota(jnp.int32, (c, c), 1)
    pt = (c - 1 - t_idx) if reverse else t_idx
    ps = (c - 1 - s_idx) if reverse else s_idx
    tri = (ps <= pt).astype(BF16)
    levels = []
    for half in (32, 16, 8):
        bs = 2 * half
        mask = (t_idx // bs == s_idx // bs) & (pt % bs >= half) & (ps % bs < half)
        levels.append((bs, half if reverse else half - 1, False, mask))
    dmask = (t_idx // SUBLANES == s_idx // SUBLANES) & (ps <= pt)
    levels.append((SUBLANES, 4 if reverse else 3, True, dmask))
    end_row = 0 if reverse else c - 1
    return tri, levels, end_row


def _gla_kernel(qf_ref, kf_ref, lgf_ref, vf_ref, qb_ref, kb_ref, lgb_ref, vb_ref,
                of_ref, ob_ref, stf_ref, stb_ref):
    @pl.when(pl.program_id(1) == 0)
    def _():
        stf_ref[...] = jnp.zeros_like(stf_ref)
        stb_ref[...] = jnp.zeros_like(stb_ref)

    c = GLA_CHUNK
    n_chunks = GLA_BLOCK // c
    dirs = [(qf_ref, kf_ref, lgf_ref, vf_ref, of_ref, stf_ref, False),
            (qb_ref, kb_ref, lgb_ref, vb_ref, ob_ref, stb_ref, True)]
    consts = [_gla_consts(False), _gla_consts(True)]
    states = [[d[5][hd] for hd in range(N_HGRN_HEADS)] for d in dirs]
    for step0 in range(0, n_chunks, GLA_PHASE_GROUP):
      units = []
      for step in range(step0, step0 + GLA_PHASE_GROUP):
        for di, (q_ref, k_ref, lg_ref, v_ref, o_ref, _, reverse) in enumerate(dirs):
            tri, levels, end_row = consts[di]
            ci = n_chunks - 1 - step if reverse else step
            rows = slice(ci * c, (ci + 1) * c)
            lg = lg_ref[rows, :]
            hi = lg.astype(BF16)
            r1 = lg - hi.astype(F32)
            mid = r1.astype(BF16)
            lo = (r1 - mid.astype(F32)).astype(BF16)
            gc = (_dot(tri, hi) + _dot(tri, mid) + _dot(tri, lo)) * LOG2E
            for hd in range(N_HGRN_HEADS):
                hs = slice(hd * HGRN_K, (hd + 1) * HGRN_K)
                units.append(dict(di=di, hd=hd, rows=rows, hs=hs, g=gc[:, hs], o_ref=o_ref,
                                  qh=q_ref[rows, hs], kh=k_ref[rows, hs],
                                  vh=v_ref[rows, hs].astype(BF16), levels=levels, end_row=end_row,
                                  attn=jnp.zeros((c, c), F32)))
      if True:
        for li in range(4):
            for u in units:
                bs, row, diag, mask = u["levels"][li]
                g, qh, kh = u["g"], u["qh"], u["kh"]
                d = g - _bcast_block_row(g, bs, row)
                if diag:
                    qf = (qh * jnp.exp2(d)).astype(BF16)
                    kf = (kh * jnp.exp2(-d)).astype(BF16)
                else:
                    e = jnp.exp2(-jnp.abs(d))
                    qf = (qh * e).astype(BF16)
                    kf = (kh * e).astype(BF16)
                u["attn"] = jnp.where(mask, _dot_nt(qf, kf), u["attn"])
        for u in units:
            g, qh, kh, vh = u["g"], u["qh"], u["kh"], u["vh"]
            st = states[u["di"]][u["hd"]]
            g_end = g[u["end_row"]:u["end_row"] + 1, :]
            qe = (qh * jnp.exp2(g)).astype(BF16)
            ke = (kh * jnp.exp2(g_end - g)).astype(BF16)
            o = _dot_nt(qe, st.astype(BF16)) + _dot(u["attn"].astype(BF16), vh)
            u["o_ref"][u["rows"], u["hs"]] = o
            states[u["di"]][u["hd"]] = st * jnp.exp2(g_end) + _dot_tn(vh, ke)
    for di, d in enumerate(dirs):
        for hd in range(N_HGRN_HEADS):
            d[5][hd] = states[di][hd]


def _gla(qh, kf, lgf, kb, lgb, vh, dims):
    n = qh.shape[0]
    b, s, ctx = dims
    blk = GLA_BLOCK
    ncb, nsb, nlat = ctx // blk, s // blk, b * s // blk

    def fwd(bi, j):
        return (jnp.where(j < ncb, nlat + bi * ncb + j, bi * nsb + (j - ncb)), 0)

    def bwd(bi, j):
        return (jnp.where(j < ncb, nlat + bi * ncb + (ncb - 1 - j), bi * nsb + (nsb - 1 - (j - ncb))), 0)

    fs = pl.BlockSpec((blk, HGRN_W), fwd)
    bs = pl.BlockSpec((blk, HGRN_W), bwd)
    out = jax.ShapeDtypeStruct((n, HGRN_W), F32)
    return pl.pallas_call(
        _gla_kernel,
        grid=(b, ncb + nsb),
        in_specs=[fs, fs, fs, fs, bs, bs, bs, bs],
        out_specs=[fs, bs],
        out_shape=[out, out],
        scratch_shapes=[pltpu.VMEM((N_HGRN_HEADS, HGRN_K, HGRN_K), F32)] * 2,
        compiler_params=_params("parallel", "arbitrary"),
        name="gla",
    )(qh, kf, lgf, vh, qh, kb, lgb, vh)


def _outproj_kernel(a_ref, of_ref, ob_ref, hg_ref, gn_ref, w_ref, h_ref, mod_ref, o_ref):
    o = of_ref[...] + ob_ref[...]
    gate = _silu(hg_ref[...])
    ys = []
    for hd in range(N_HGRN_HEADS):
        hs = slice(hd * HGRN_K, (hd + 1) * HGRN_K)
        ys.append(_rms(o[:, hs], gn_ref[...]) * gate[:, hs])
    y = jnp.concatenate(ys, axis=-1).astype(BF16)
    mix = _dot(a_ref[...], w_ref[0:MLA_W, :]) + _dot(y, w_ref[MLA_W:, :])
    o_ref[...] = h_ref[...] + mod_ref[2:3, :] * mix


def _outproj(a, of, ob, hg, gn, w, h, mod, dims, n_rows):
    n, d = h.shape
    b, s, ctx = dims
    tm = MLP_TILE
    row = lambda i: (i, 0)
    const = lambda i: (0, 0)
    half = pl.BlockSpec((tm, HGRN_W), row)
    return pl.pallas_call(
        _outproj_kernel,
        grid=(n_rows // tm,),
        in_specs=[pl.BlockSpec((tm, MLA_W), row), half, half, half,
                  pl.BlockSpec((1, HGRN_K), const), pl.BlockSpec(w.shape, const),
                  pl.BlockSpec((tm, d), row),
                  pl.BlockSpec((None, N_MOD, d), lambda i: (jnp.minimum(i * tm // s, b), 0, 0))],
        out_specs=pl.BlockSpec((tm, d), row),
        out_shape=jax.ShapeDtypeStruct((n, d), F32),
        input_output_aliases={6: 0},
        compiler_params=_params("parallel"),
        name="outproj",
    )(a, of, ob, hg, gn, w, h, mod)


def _ffn_kernel(h_ref, mod_ref, ng_ref, wg_ref, wu_ref, wd_ref, o_ref):
    h = h_ref[...]
    v = _norm_mod(h, ng_ref[...], mod_ref[3:4, :], mod_ref[4:5, :]).astype(BF16)
    hid = (_silu(_dot(v, wg_ref[...])) * _dot(v, wu_ref[...])).astype(BF16)
    o_ref[...] = h + mod_ref[5:6, :] * _dot(hid, wd_ref[...])


def _ffn(h, mod, ng, wg, wu, wd, dims, n_rows):
    n, d = h.shape
    b, s, ctx = dims
    tm = MLP_TILE
    row = lambda i: (i, 0)
    const = lambda i: (0, 0)
    return pl.pallas_call(
        _ffn_kernel,
        grid=(n_rows // tm,),
        in_specs=[pl.BlockSpec((tm, d), row),
                  pl.BlockSpec((None, N_MOD, d), lambda i: (jnp.minimum(i * tm // s, b), 0, 0)),
                  pl.BlockSpec((1, d), const),
                  pl.BlockSpec(wg.shape, const, pipeline_mode=pl.Buffered(1)),
                  pl.BlockSpec(wu.shape, const, pipeline_mode=pl.Buffered(1)),
                  pl.BlockSpec(wd.shape, const, pipeline_mode=pl.Buffered(1))],
        out_specs=pl.BlockSpec((tm, d), row),
        out_shape=jax.ShapeDtypeStruct((n, d), F32),
        input_output_aliases={0: 0},
        compiler_params=_params("parallel"),
        name="ffn",
    )(h, mod, ng, wg, wu, wd)


def _pool_kernel(h_ref, prev_ref, next_ref, mod_ref, ng_ref, w_ref, b_ref, sc_ref, o_ref, ext_ref,
                 *, tm, seq_tiles_lat, n_lat_tiles, seq_tiles_ctx):
    i = pl.program_id(0)
    halo = SUBLANES
    shift, scale = mod_ref[0:1, :], mod_ref[1:2, :]
    in_lat = i < n_lat_tiles
    j = jnp.where(in_lat, i % seq_tiles_lat, (i - n_lat_tiles) % seq_tiles_ctx)
    n_seq_tiles = jnp.where(in_lat, seq_tiles_lat, seq_tiles_ctx)
    h = h_ref[...]
    u = _norm_mod(h, ng_ref[...], shift, scale)
    up = _norm_mod(prev_ref[...], ng_ref[...], shift, scale)
    un = _norm_mod(next_ref[...], ng_ref[...], shift, scale)
    ext_ref[0:halo, :] = jnp.where(j > 0, up, 0.0)
    ext_ref[halo:halo + tm, :] = u
    ext_ref[halo + tm:, :] = jnp.where(j < n_seq_tiles - 1, un, 0.0)
    pos = j * tm + lax.broadcasted_iota(jnp.int32, (tm, 1), 0)
    t_len = n_seq_tiles * tm
    outs = []
    for g, win in enumerate(POOL_WINDOWS):
        cs = slice(g * POOL_C, (g + 1) * POOL_C)
        acc = None
        for dlt in range(-(win // 2), win // 2):
            part = ext_ref[halo + dlt:halo + dlt + tm, cs]
            acc = part if acc is None else acc + part
        lo = jnp.maximum(pos - win // 2, 0)
        hi = jnp.minimum(pos - win // 2 + win, t_len)
        cnt = (hi - lo).astype(F32)
        pooled = (acc / cnt - u[:, cs]).astype(BF16)
        outs.append(_dot(pooled, w_ref[g]) + b_ref[g:g + 1, :])
    y = jnp.concatenate(outs, axis=-1) * sc_ref[...]
    o_ref[...] = h + mod_ref[2:3, :] * y


def _pool(h, mod, ng, w, bias, scale, dims, n_rows):
    n, d = h.shape
    b, s, ctx = dims
    tm = ROW_TILE
    hb = tm // SUBLANES
    n_blk8 = n // SUBLANES
    row = lambda i: (i, 0)
    const = lambda i: (0, 0)
    kern = functools.partial(_pool_kernel, tm=tm, seq_tiles_lat=s // tm, n_lat_tiles=b * s // tm,
                             seq_tiles_ctx=ctx // tm)
    return pl.pallas_call(
        kern,
        grid=(n_rows // tm,),
        in_specs=[pl.BlockSpec((tm, d), row),
                  pl.BlockSpec((SUBLANES, d), lambda i: (jnp.maximum(i * hb - 1, 0), 0)),
                  pl.BlockSpec((SUBLANES, d), lambda i: (jnp.minimum((i + 1) * hb, n_blk8 - 1), 0)),
                  pl.BlockSpec((None, N_MOD, d), lambda i: (jnp.minimum(i * tm // s, b), 0, 0)),
                  pl.BlockSpec((1, d), const),
                  pl.BlockSpec(w.shape, lambda i: (0, 0, 0)),
                  pl.BlockSpec(bias.shape, const),
                  pl.BlockSpec((1, d), const)],
        out_specs=pl.BlockSpec((tm, d), row),
        out_shape=jax.ShapeDtypeStruct((n_rows, d), F32),
        scratch_shapes=[pltpu.VMEM((tm + 2 * SUBLANES, d), F32)],
        compiler_params=_params("parallel"),
        name="pool",
    )(h, h, h, mod, ng, w, bias, scale)


def _route_kernel(h_ref, mod_ref, ng_ref, r_ref, v_ref, idx_ref, wt_ref, lst_ref, base_ref, run_ref):
    rt = ROW_TILE
    subs = [slice(k * rt, (k + 1) * rt) for k in range(h_ref.shape[0] // rt)]

    @pl.when(pl.program_id(0) == 0)
    def _():
        run_ref[...] = jnp.zeros_like(run_ref)

    lane = lax.broadcasted_iota(jnp.int32, (rt, LANES), 1)
    ti = lax.broadcasted_iota(jnp.int32, (rt, rt), 0)
    tj = lax.broadcasted_iota(jnp.int32, (rt, rt), 1)
    neg = jnp.float32(-jnp.inf)
    vs = [_norm_mod(h_ref[sl, :], ng_ref[...], mod_ref[3:4, :], mod_ref[4:5, :]) for sl in subs]
    v_his = [v.astype(BF16) for v in vs]
    half = vs[0].shape[-1] // 2
    for sl, v_hi in zip(subs, v_his):
        vb = pltpu.bitcast(v_hi.astype(F32), jnp.uint32)
        v_ref[sl, :] = (vb[:, :half] >> 16) | (vb[:, half:] & jnp.uint32(0xFFFF0000))
    logits = []
    for v, v_hi in zip(vs, v_his):
        v_lo = (v - v_hi.astype(F32)).astype(BF16)
        lg = _dot(v_hi, r_ref[0]) + (_dot(v_lo, r_ref[0]) + _dot(v_hi, r_ref[1]))
        logits.append(jnp.where(lane < N_EXPERTS, lg, neg))
    m1s = [jnp.max(lg, axis=-1, keepdims=True) for lg in logits]
    i1s = [jnp.min(jnp.where(lg == m1, lane, LANES), axis=-1, keepdims=True) for lg, m1 in zip(logits, m1s)]
    rests = [jnp.where(lane == i1, neg, lg) for lg, i1 in zip(logits, i1s)]
    m2s = [jnp.max(r, axis=-1, keepdims=True) for r in rests]
    i2s = [jnp.min(jnp.where(r == m2, lane, LANES), axis=-1, keepdims=True) for r, m2 in zip(rests, m2s)]
    for sl, m1, m2 in zip(subs, m1s, m2s):
        e2 = jnp.exp(m2 - m1)
        wt_ref[sl, :] = jnp.where(lane == 0, 1.0 / (1.0 + e2), jnp.where(lane == 1, e2 / (1.0 + e2), 0.0))
    ohs = [((lane == i1) | (lane == i2)).astype(BF16) for i1, i2 in zip(i1s, i2s)]
    befores = [_dot((tj < ti).astype(BF16), oh) for oh in ohs]
    befores_t = [_dot_tn(oh, (ti < tj).astype(BF16)) for oh in ohs]
    choses_t = [_dot_tn(oh, (ti == tj).astype(BF16)) for oh in ohs]
    run = run_ref[...]
    for k, sl in enumerate(subs):
        pos = run + befores[k]
        r1 = jnp.sum(jnp.where(lane == i1s[k], pos, 0.0), axis=-1, keepdims=True)
        r2 = jnp.sum(jnp.where(lane == i2s[k], pos, 0.0), axis=-1, keepdims=True)
        idx_ref[sl, :] = jnp.where(lane == 0, i1s[k], jnp.where(lane == 1, i2s[k], jnp.where(
            lane == 2, r1.astype(jnp.int32), jnp.where(lane == 3, r2.astype(jnp.int32), 0))))
        base_ref[k] = run.astype(jnp.int32)
        run = run + jnp.sum(ohs[k].astype(F32), axis=0, keepdims=True)
    run_ref[...] = run
    ids = lax.broadcasted_iota(jnp.int32, (rt, LANES), 0).astype(BF16)
    lsts = [jnp.zeros((rt, LANES), F32) for _ in subs]
    for e in range(N_EXPERTS):
        for k in range(len(subs)):
            sel = (befores_t[k][e:e + 1, :] == ti.astype(F32)) & (choses_t[k][e:e + 1, :] > 0.5)
            lsts[k] = jnp.where(lane == e, _dot(sel.astype(BF16), ids), lsts[k])
    for sl, lst in zip(subs, lsts):
        lst_ref[sl, :] = lst.astype(jnp.int32)


def _route(h, mod, ng, router, dims, n_rows):
    n, d = h.shape
    b, s, ctx = dims
    assert ROW_TILE <= 256
    nsub = ROUTE_SUBTILES
    while s % (nsub * ROW_TILE) or (n_rows - b * s) % (nsub * ROW_TILE):
        nsub //= 2
    tm = nsub * ROW_TILE
    nt = n_rows // ROW_TILE
    row = lambda i: (i, 0)
    const = lambda i: (0, 0)
    return pl.pallas_call(
        _route_kernel,
        grid=(n_rows // tm,),
        in_specs=[pl.BlockSpec((tm, d), row),
                  pl.BlockSpec((None, N_MOD, d), lambda i: (jnp.minimum(i * tm // s, b), 0, 0)),
                  pl.BlockSpec((1, d), const),
                  pl.BlockSpec((2, d, LANES), lambda i: (0, 0, 0))],
        out_specs=[pl.BlockSpec((tm, d // 2), row), pl.BlockSpec((tm, LANES), row),
                   pl.BlockSpec((tm, LANES), row), pl.BlockSpec((tm, LANES), row),
                   pl.BlockSpec((nsub, 1, LANES), lambda i: (i, 0, 0))],
        out_shape=[jax.ShapeDtypeStruct((n_rows, d // 2), jnp.uint32),
                   jax.ShapeDtypeStruct((n_rows, LANES), jnp.int32),
                   jax.ShapeDtypeStruct((n_rows, LANES), F32),
                   jax.ShapeDtypeStruct((n_rows, LANES), jnp.int32),
                   jax.ShapeDtypeStruct((nt, 1, LANES), jnp.int32)],
        scratch_shapes=[pltpu.VMEM((1, LANES), F32)],
        compiler_params=_params("arbitrary"),
        name="route",
    )(h, mod, ng, router)


def _moe_kernel(te_ref, nu_ref, x_ref, wg_ref, wu_ref, wd_ref, o_ref, xs_ref):
    i = pl.program_id(0)
    f = pl.program_id(1)

    @pl.when((i < nu_ref[0]) & (f == 0))
    def _():
        half = x_ref.shape[-1]
        xw = x_ref[...]
        xs_ref[:, :half] = pltpu.bitcast(xw << 16, F32).astype(BF16)
        xs_ref[:, half:] = pltpu.bitcast(xw & jnp.uint32(0xFFFF0000), F32).astype(BF16)
        o_ref[...] = jnp.zeros_like(o_ref)

    @pl.when(i < nu_ref[0])
    def _():
        x = xs_ref[...]
        w = wg_ref.shape[-1] // MOE_FF_SPLIT
        hids = []
        for c in range(MOE_FF_SPLIT):
            cs = slice(c * w, (c + 1) * w)
            hids.append((_silu(_dot(x, wg_ref[:, cs].astype(BF16))) *
                         _dot(x, wu_ref[:, cs].astype(BF16))).astype(BF16))
        part = None
        for c in range(MOE_FF_SPLIT):
            p = _dot(hids[c], wd_ref[c * w:(c + 1) * w, :].astype(BF16))
            part = p if part is None else part + p
        o_ref[...] += part


def _moe_experts(x_sorted, tile_expert, n_used, wg, wu, wd, layer):
    npad, dw = x_sorted.shape
    d = 2 * dw
    tm, tf = MOE_TILE, MOE_FF_TILE
    dff = wg.shape[-1]
    grid_spec = pltpu.PrefetchScalarGridSpec(
        num_scalar_prefetch=2,
        grid=(npad // tm, dff // tf),
        in_specs=[pl.BlockSpec((tm, dw), lambda i, f, te, nu: (i, 0)),
                  pl.BlockSpec((None, None, d, tf), lambda i, f, te, nu: (layer, te[i], 0, f)),
                  pl.BlockSpec((None, None, d, tf), lambda i, f, te, nu: (layer, te[i], 0, f)),
                  pl.BlockSpec((None, None, tf, d), lambda i, f, te, nu: (layer, te[i], f, 0))],
        out_specs=pl.BlockSpec((tm, d), lambda i, f, te, nu: (i, 0)),
        scratch_shapes=[pltpu.VMEM((tm, d), BF16)],
    )
    return pl.pallas_call(
        _moe_kernel,
        grid_spec=grid_spec,
        out_shape=jax.ShapeDtypeStruct((npad, d), F32),
        compiler_params=_params("parallel", "arbitrary"),
        name="moe_experts",
    )(tile_expert, n_used, x_sorted, wg, wu, wd)


def _combine_kernel(h_ref, y0_ref, y1_ref, wt_ref, mod_ref, *rest):
    y = wt_ref[:, 0:1] * y0_ref[...] + wt_ref[:, 1:2] * y1_ref[...]
    h = h_ref[...] + mod_ref[5:6, :] * y
    if len(rest) == 2:
        fg_ref, o_ref = rest
        o_ref[...] = _rms(h, fg_ref[...])
    else:
        rest[0][...] = h


def _combine(h, y0, y1, wt, mod, dims, n_rows, final_g=None):
    n, d = h.shape
    b, s, ctx = dims
    tm = ROW_TILE
    row = lambda i: (i, 0)
    in_specs = [pl.BlockSpec((tm, d), row), pl.BlockSpec((tm, d), row), pl.BlockSpec((tm, d), row),
                pl.BlockSpec((tm, LANES), row),
                pl.BlockSpec((None, N_MOD, d), lambda i: (jnp.minimum(i * tm // s, b), 0, 0))]
    args = [h, y0, y1, wt, mod]
    if final_g is not None:
        in_specs.append(pl.BlockSpec((1, d), lambda i: (0, 0)))
        args.append(final_g)
    return pl.pallas_call(
        _combine_kernel,
        grid=(n_rows // tm,),
        in_specs=in_specs,
        out_specs=pl.BlockSpec((tm, d), row),
        out_shape=jax.ShapeDtypeStruct((n_rows if final_g is not None else n, d), F32),
        input_output_aliases={} if final_g is not None else {0: 0},
        compiler_params=_params("parallel"),
        name="moe_combine",
    )(*args)


def _moe(h, mod, ng, router, wg, wu, wd, layer, dims, n_rows, final_g=None):
    tm, rt = MOE_TILE, ROW_TILE
    v, idx, wt, lst, base = _route(h, mod, ng, router, dims, n_rows)
    base = base[:, 0, :N_EXPERTS]
    e_sel, pos = idx[:, 0:2], idx[:, 2:4]
    last = e_sel[-rt:]
    counts = base[-1] + jnp.sum(last[:, :, None] == jnp.arange(N_EXPERTS), axis=(0, 1))
    tiles_per = (counts + tm - 1) // tm
    tile_end = jnp.cumsum(tiles_per)
    group_start = (tile_end - tiles_per) * tm
    slot = group_start[e_sel] + pos
    n_tiles = 2 * n_rows // tm + N_EXPERTS
    tile_expert = jnp.minimum(jnp.sum(jnp.arange(n_tiles)[:, None] >= tile_end[None, :], axis=1),
                              N_EXPERTS - 1).astype(jnp.int32)
    n_used = tile_end[-1:].astype(jnp.int32)
    rank = (jnp.arange(n_tiles * tm, dtype=jnp.int32).reshape(n_tiles, tm)
            - group_start[tile_expert][:, None])
    base_te = base.T[tile_expert]
    owns = base_te[:, None, :] <= rank[:, :, None]
    rtile = jnp.sum(owns, axis=-1) - 1
    local = rank - jnp.max(jnp.where(owns, base_te[:, None, :], 0), axis=-1)
    flat = jnp.clip((rtile * rt + local) * N_EXPERTS + tile_expert[:, None], 0, n_rows * N_EXPERTS - 1)
    src = rtile * rt + jnp.take(lst[:, :N_EXPERTS].reshape(-1), flat, mode="clip")
    spread = jnp.arange(n_tiles * tm, dtype=jnp.int32).reshape(n_tiles, tm) % n_rows
    src = jnp.where(rank < counts[tile_expert][:, None], src, spread).reshape(-1)
    x_sorted = jnp.take(v, src, axis=0, mode="clip")
    out_sorted = _moe_experts(x_sorted, tile_expert, n_used, wg, wu, wd, layer)
    y0 = jnp.take(out_sorted, slot[:, 0], axis=0, mode="clip")
    y1 = jnp.take(out_sorted, slot[:, 1], axis=0, mode="clip")
    return _combine(h, y0, y1, wt, mod, dims, n_rows, final_g)


def _final_kernel(h_ref, g_ref, o_ref):
    o_ref[...] = _rms(h_ref[...], g_ref[...])


def _final_norm(h, g, n_rows):
    n, d = h.shape
    tm = ROW_TILE
    row = lambda i: (i, 0)
    return pl.pallas_call(
        _final_kernel,
        grid=(n_rows // tm,),
        in_specs=[pl.BlockSpec((tm, d), row), pl.BlockSpec((1, d), lambda i: (0, 0))],
        out_specs=pl.BlockSpec((tm, d), row),
        out_shape=jax.ShapeDtypeStruct((n_rows, d), F32),
        compiler_params=_params("parallel"),
        name="final_norm",
    )(h, g)


def _pad_cols(w, width):
    return jnp.pad(w, ((0, 0), (0, width - w.shape[-1])))


def _pack_in(w_in):
    cq = w_in[:, :Q_LORA]
    ckv = w_in[:, Q_LORA:KR_OFF]
    kr = w_in[:, KR_OFF:KR_OFF + QK_ROPE]
    rest = w_in[:, KR_OFF + QK_ROPE:]
    return jnp.concatenate([cq, ckv, _pad_cols(kr, LANES), rest], axis=-1).astype(BF16)


def _pack_q(w_uq):
    w = w_uq.reshape(Q_LORA, N_HEADS, QK_DIM)
    z = jnp.zeros((Q_LORA, N_HEADS, HEAD_PAD - QK_DIM), w.dtype)
    return jnp.concatenate([w, z], axis=-1).reshape(Q_LORA, QKV_W).astype(BF16)


def _pack_kv(w_ukv):
    w = w_ukv.reshape(KV_LORA, N_HEADS, QK_NOPE + V_HEAD)
    z = jnp.zeros((KV_LORA, N_HEADS, HEAD_PAD - QK_NOPE), w.dtype)
    k_all = jnp.concatenate([w[..., :QK_NOPE], z], axis=-1).reshape(KV_LORA, QKV_W)
    v_all = jnp.concatenate([w[..., QK_NOPE:], z], axis=-1).reshape(KV_LORA, QKV_W)
    return jnp.concatenate([k_all, v_all], axis=-1).astype(BF16)


def _rope_tabs(s, tm):
    rows = s // GRID_W
    row = jnp.repeat(jnp.arange(rows, dtype=F32), GRID_W)
    col = jnp.tile(jnp.arange(GRID_W, dtype=F32), rows)
    half = QK_ROPE // 2
    inv = 1.0 / (ROPE_BASE ** (jnp.arange(0, half, 2, dtype=F32) / half))
    ar = row[:, None] * inv[None, :]
    ac = col[:, None] * inv[None, :]
    ang = jnp.concatenate([ar, ar, ac, ac], axis=-1)
    cos = jnp.concatenate([jnp.cos(ang), jnp.ones((tm, QK_ROPE), F32)], axis=0)
    sin = jnp.concatenate([jnp.sin(ang), jnp.zeros((tm, QK_ROPE), F32)], axis=0)
    t = cos.shape[0]
    first = (jnp.arange(QK_ROPE) % (QK_ROPE // 2)) < QK_ROPE // 4
    sin1 = jnp.where(first, -sin, 0.0)
    sin2 = jnp.where(first, 0.0, sin)
    scale = QK_DIM ** -0.5 * LOG2E

    def q_tab(rope_part, nope_val):
        return jnp.concatenate([jnp.full((t, QK_NOPE), nope_val, F32), rope_part * scale,
                                jnp.zeros((t, HEAD_PAD - QK_DIM), F32)], axis=-1)

    return (q_tab(cos, scale), q_tab(sin1, 0.0), q_tab(sin2, 0.0),
            _pad_cols(cos, LANES), _pad_cols(sin1, LANES), _pad_cols(sin2, LANES))


def kernel(x, c, ctx, c_ctx, w_mod, b_mod, norm_g, final_g, ab_w_in, ab_g_cq, ab_g_ckv, ab_w_uq, ab_w_ukv, hgrn_lb_logits, hgrn_g_norm, ab_w_out, ffn_w_gate, ffn_w_up, ffn_w_down, pool_w, pool_b, pool_scale, moe_router, moe_w_gate, moe_w_up, moe_w_down):
    b, s, d = x.shape
    ctx_len = ctx.shape[1]
    depth = w_mod.shape[0]
    dims = (b, s, ctx_len)
    n_lat, n_ctx = b * s, b * ctx_len
    assert d == D_MODEL and s % ROW_TILE == 0 and ctx_len % ROW_TILE == 0
    assert ctx_len % GLA_BLOCK == 0 and s % GLA_BLOCK == 0 and n_lat % ctx_len == 0

    h = jnp.concatenate([x.reshape(n_lat, d), ctx.reshape(n_ctx, d)], axis=0)
    n_groups = ((b + 1 + SUBLANES - 1) // SUBLANES) * SUBLANES
    cvec = jnp.concatenate([c, c_ctx[None, :], jnp.zeros((n_groups - b - 1, d), F32)], axis=0)
    mod_all = _mod_table(cvec, w_mod, b_mod).reshape(depth, n_groups, N_MOD, d)

    lb_p = jax.nn.softmax(hgrn_lb_logits.astype(F32), axis=0)
    lb_all = jnp.cumsum(lb_p, axis=0) - lb_p[:1]
    tabs = _rope_tabs(s, ROW_TILE)

    for l in range(depth):
        j = l // 2
        even = l % 2 == 0
        ctx_later = any(m % 2 == 0 for m in range(l + 1, depth))
        n_rows = n_lat + n_ctx if ctx_later else n_lat
        mod = mod_all[l]
        ng1, ng2 = norm_g[l, 0][None, :], norm_g[l, 1][None, :]
        if even:
            (q, k, v, qh, kf, lgf, kb, lgb, vh, hg) = _inproj(
                h, mod, ng1, _pack_in(ab_w_in[j]), ab_g_cq[j][None, :], ab_g_ckv[j][None, :],
                _pack_q(ab_w_uq[j]), _pack_kv(ab_w_ukv[j]), lb_all[j], tabs, dims)
            a = _attention_latent(q, k, v, dims)
            if ctx_later:
                a = jnp.concatenate([a, _attention_ctx(q, k, v, dims)], axis=0)
            o_f, o_b = _gla(qh, kf, lgf, kb, lgb, vh, dims)
            h = _outproj(a, o_f, o_b, hg, hgrn_g_norm[j][None, :], ab_w_out[j].astype(BF16), h, mod,
                         dims, n_rows)
            h = _ffn(h, mod, ng2, ffn_w_gate[j].astype(BF16), ffn_w_up[j].astype(BF16),
                     ffn_w_down[j].astype(BF16), dims, n_rows)
        else:
            h = _pool(h, mod, ng1, pool_w[j].astype(BF16), pool_b[j], pool_scale[j][None, :], dims, n_rows)
            r_full = _pad_cols(moe_router[j], LANES)
            r_hi = r_full.astype(BF16)
            router = jnp.stack([r_hi, (r_full - r_hi.astype(F32)).astype(BF16)])
            last = l == depth - 1
            h = _moe(h, mod, ng2, router, moe_w_gate, moe_w_up, moe_w_down, j, dims, n_rows,
                     final_g[None, :] if last else None)
            if last:
                return h.reshape(b, s, d)
    return _final_norm(h, final_g[None, :], n_lat).reshape(b, s, d)
```

```python
import functools

import jax
import jax.numpy as jnp
from jax import lax
from jax.experimental import pallas as pl
from jax.experimental.pallas import tpu as pltpu

F32 = jnp.float32
BF16 = jnp.bfloat16

D_MODEL = 1024
GRID_W = 64
N_MOD = 6
EPS = 1e-6
N_HEADS = 8
V_HEAD = 64
QK_NOPE = 64
QK_ROPE = 32
QK_DIM = QK_NOPE + QK_ROPE
Q_LORA = 384
KV_LORA = 256
ROPE_BASE = 10000.0
MLA_W = 512
HGRN_W = 512
HGRN_K = 128
N_HGRN_HEADS = 4
POOL_WINDOWS = (2, 4, 8, 16)
POOL_C = 256
N_EXPERTS = 8

LANES = 128
SUBLANES = 8
VMEM_LIMIT_BYTES = 56 * 1024 * 1024

HEAD_PAD = LANES
QKV_W = N_HEADS * HEAD_PAD
KR_OFF = Q_LORA + KV_LORA
H_OFF = KR_OFF + LANES
IN_COLS = H_OFF + 5 * HGRN_W
GLA_CHUNK = 64
GLA_BLOCK = 256
GLA_PHASE_GROUP = 2
ROW_TILE = 256
ROUTE_SUBTILES = 4
MLP_TILE = 512
ATTN_Q_TILE = 512
ATTN_HEAD_GROUP = 4
MOE_TILE = 1024
MOE_FF_TILE = 512
MOE_FF_SPLIT = 2
LOG2E = 1.4426950408889634


def _params(*sem):
    return pltpu.CompilerParams(dimension_semantics=sem, vmem_limit_bytes=VMEM_LIMIT_BYTES)


def _dot(a, b):
    return jnp.dot(a, b, preferred_element_type=F32)


def _dot_nt(a, b):
    return lax.dot_general(a, b, (((1,), (1,)), ((), ())), preferred_element_type=F32)


def _dot_tn(a, b):
    return lax.dot_general(a, b, (((0,), (0,)), ((), ())), preferred_element_type=F32)


def _rms(x, g):
    return x * lax.rsqrt(jnp.mean(x * x, axis=-1, keepdims=True) + EPS) * g


def _silu(x):
    return x * jax.nn.sigmoid(x)


def _norm_mod(h, g, shift, scale):
    return _rms(h, g) * (1.0 + scale) + shift


def _mod_kernel(c_ref, w_ref, b_ref, o_ref):
    c = c_ref[...]
    o_ref[...] = jnp.dot(_silu(c), w_ref[...], preferred_element_type=F32,
                         precision=lax.Precision.HIGHEST) + b_ref[...]


def _mod_table(cvec, w_mod, b_mod):
    depth, d, nd = w_mod.shape
    g = cvec.shape[0]
    tn = 1536
    return pl.pallas_call(
        _mod_kernel,
        grid=(depth, nd // tn),
        in_specs=[pl.BlockSpec((g, d), lambda l, j: (0, 0)),
                  pl.BlockSpec((None, d, tn), lambda l, j: (l, 0, j)),
                  pl.BlockSpec((None, 1, tn), lambda l, j: (l, 0, j))],
        out_specs=pl.BlockSpec((None, g, tn), lambda l, j: (l, 0, j)),
        out_shape=jax.ShapeDtypeStruct((depth, g, nd), F32),
        compiler_params=_params("parallel", "parallel"),
        name="mod_table",
    )(cvec, w_mod, b_mod.reshape(depth, 1, nd))


def _inproj_body(h, mod_ref, ng_ref, win_ref, gcq_ref, gckv_ref, wq_ref, wkv_ref,
                   lb_ref, cq_tab, s1q_tab, s2q_tab, ck_tab, s1k_tab, s2k_tab,
                   q_ref, k_ref, v_ref, qh_ref, kf_ref, lgf_ref, kb_ref, lgb_ref, vh_ref, hg_ref):
    half_rot = QK_ROPE // 4

    def rope(x, c_tab, s1_tab, s2_tab):
        return (x * c_tab[...] + pltpu.roll(x, LANES - half_rot, 1) * s1_tab[...]
                + pltpu.roll(x, half_rot, 1) * s2_tab[...])

    u = _norm_mod(h, ng_ref[...], mod_ref[0:1, :], mod_ref[1:2, :]).astype(BF16)
    p = _dot(u, win_ref[...])
    cqn = _rms(p[:, 0:Q_LORA], gcq_ref[...]).astype(BF16)
    a = _dot(cqn, wq_ref[...])
    q_ref[...] = jnp.concatenate(
        [rope(a[:, hd * HEAD_PAD:(hd + 1) * HEAD_PAD], cq_tab, s1q_tab, s2q_tab) for hd in range(N_HEADS)],
        axis=-1).astype(BF16)
    ckvn = _rms(p[:, Q_LORA:KR_OFF], gckv_ref[...]).astype(BF16)
    kv = _dot(ckvn, wkv_ref[...])
    kr = pltpu.roll(rope(p[:, KR_OFF:H_OFF], ck_tab, s1k_tab, s2k_tab), QK_NOPE, 1)
    k_ref[...] = jnp.concatenate(
        [kv[:, hd * HEAD_PAD:(hd + 1) * HEAD_PAD] + kr for hd in range(N_HEADS)], axis=-1).astype(BF16)
    lane = lax.broadcasted_iota(jnp.int32, (1, QKV_W), 1)
    ones_col = (lane % HEAD_PAD == V_HEAD).astype(F32)
    v_ref[...] = (kv[:, QKV_W:] + ones_col).astype(BF16)
    o = H_OFF
    qh_ref[...] = _silu(p[:, o:o + HGRN_W])
    kf = (1.0 - lb_ref[0:1, :]) * jax.nn.sigmoid(-p[:, o + HGRN_W:o + 2 * HGRN_W])
    kf_ref[...] = kf
    lgf_ref[...] = jnp.log1p(-kf)
    kb = (1.0 - lb_ref[1:2, :]) * jax.nn.sigmoid(-p[:, o + 2 * HGRN_W:o + 3 * HGRN_W])
    kb_ref[...] = kb
    lgb_ref[...] = jnp.log1p(-kb)
    vh_ref[...] = p[:, o + 3 * HGRN_W:o + 4 * HGRN_W]
    hg_ref[...] = p[:, o + 4 * HGRN_W:o + 5 * HGRN_W]


def _rows_from(refs, lat_tiles):
    if lat_tiles is None:
        return refs[0][...], refs[1:]
    return jnp.where(pl.program_id(0) < lat_tiles, refs[0][...], refs[1][...]), refs[2:]


def _inproj_kernel(*refs, lat_tiles):
    h, rest = _rows_from(refs, lat_tiles)
    _inproj_body(h, *rest)


def _split_rows(h, tm):
    if not isinstance(h, tuple):
        return [pl.BlockSpec((tm, h.shape[1]), lambda i: (i, 0))], [h], None
    x2, c2 = h
    lat_tiles = x2.shape[0] // tm
    specs = [pl.BlockSpec((tm, x2.shape[1]), lambda i: (jnp.minimum(i, lat_tiles - 1), 0)),
             pl.BlockSpec((tm, c2.shape[1]), lambda i: (jnp.maximum(i - lat_tiles, 0), 0))]
    return specs, [x2, c2], lat_tiles


def _inproj(h, mod, ng, win, gcq, gckv, wq, wkv, lb, tabs, dims):
    b, s, ctx = dims
    n, d = b * (s + ctx), D_MODEL
    tm = ROW_TILE
    h_specs, h_args, lat_tiles = _split_rows(h, tm)
    n_lat_t = b * s // tm
    pos_blocks = s // tm

    def grp(i):
        return jnp.minimum(i * tm // s, b)

    def pos(i):
        return jnp.where(i < n_lat_t, i % pos_blocks, pos_blocks)

    row = lambda i: (i, 0)
    const = lambda i: (0, 0)
    tab_spec = pl.BlockSpec((tm, LANES), lambda i: (pos(i), 0))
    wide = jax.ShapeDtypeStruct((n, QKV_W), BF16)
    hg = jax.ShapeDtypeStruct((n, HGRN_W), F32)
    return pl.pallas_call(
        functools.partial(_inproj_kernel, lat_tiles=lat_tiles),
        grid=(n // tm,),
        in_specs=h_specs + [
                  pl.BlockSpec((None, N_MOD, d), lambda i: (grp(i), 0, 0)),
                  pl.BlockSpec((1, d), const),
                  pl.BlockSpec(win.shape, const),
                  pl.BlockSpec((1, Q_LORA), const),
                  pl.BlockSpec((1, KV_LORA), const),
                  pl.BlockSpec(wq.shape, const),
                  pl.BlockSpec(wkv.shape, const),
                  pl.BlockSpec((2, HGRN_W), const),
                  tab_spec, tab_spec, tab_spec, tab_spec, tab_spec, tab_spec],
        out_specs=[pl.BlockSpec((tm, QKV_W), row)] * 3 + [pl.BlockSpec((tm, HGRN_W), row)] * 7,
        out_shape=[wide] * 3 + [hg] * 7,
        compiler_params=_params("parallel"),
        name="inproj",
    )(*h_args, mod, ng, win, gcq, gckv, wq, wkv, lb, *tabs)


def _attn_kernel(*refs, n_kv):
    q_ref = refs[0]
    k_refs = refs[1:1 + 2 * n_kv:2]
    v_refs = refs[2:2 + 2 * n_kv:2]
    o_ref = refs[-1]
    for h0 in range(0, N_HEADS, ATTN_HEAD_GROUP):
        heads = range(h0, h0 + ATTN_HEAD_GROUP)
        sls = {hd: slice(hd * HEAD_PAD, (hd + 1) * HEAD_PAD) for hd in heads}
        scores = {hd: [_dot_nt(q_ref[:, sls[hd]], k_ref[:, sls[hd]]) for k_ref in k_refs] for hd in heads}
        ms = {hd: functools.reduce(jnp.maximum, [jnp.max(sc, axis=-1, keepdims=True) for sc in scores[hd]])
              for hd in heads}
        for hd in heads:
            acc = None
            for sc, v_ref in zip(scores[hd], v_refs):
                part = _dot(jnp.exp2(sc - ms[hd]).astype(BF16), v_ref[:, sls[hd]])
                acc = part if acc is None else acc + part
            o = acc[:, :V_HEAD] / acc[:, V_HEAD:V_HEAD + 1]
            o_ref[:, hd * V_HEAD:(hd + 1) * V_HEAD] = o.astype(o_ref.dtype)


def _attention_latent(q, k, v, dims):
    b, s, ctx = dims
    tq = min(ATTN_Q_TILE, s)
    nq = s // tq
    lat = lambda bi, j: (bi, 0)
    cx = lambda bi, j: (b * s // ctx + bi, 0)
    return pl.pallas_call(
        functools.partial(_attn_kernel, n_kv=2),
        grid=(b, nq),
        in_specs=[pl.BlockSpec((tq, QKV_W), lambda bi, j: (bi * nq + j, 0)),
                  pl.BlockSpec((s, QKV_W), lat), pl.BlockSpec((s, QKV_W), lat),
                  pl.BlockSpec((ctx, QKV_W), cx), pl.BlockSpec((ctx, QKV_W), cx)],
        out_specs=pl.BlockSpec((tq, MLA_W), lambda bi, j: (bi * nq + j, 0)),
        out_shape=jax.ShapeDtypeStruct((b * s, MLA_W), BF16),
        compiler_params=_params("parallel", "arbitrary"),
        name="attn_latent",
    )(q, k, v, k, v)


def _attention_ctx(q, k, v, dims):
    b, s, ctx = dims
    cx = lambda bi: (b * s // ctx + bi, 0)
    return pl.pallas_call(
        functools.partial(_attn_kernel, n_kv=1),
        grid=(b,),
        in_specs=[pl.BlockSpec((ctx, QKV_W), cx), pl.BlockSpec((ctx, QKV_W), cx),
                  pl.BlockSpec((ctx, QKV_W), cx)],
        out_specs=pl.BlockSpec((ctx, MLA_W), lambda bi: (bi, 0)),
        out_shape=jax.ShapeDtypeStruct((b * ctx, MLA_W), BF16),
        compiler_params=_params("parallel"),
        name="attn_ctx",
    )(q, k, v)


def _bcast_block_row(g, bs, row):
    if bs == GLA_CHUNK:
        return g[row:row + 1, :]
    g3 = g.reshape(GLA_CHUNK // bs, bs, g.shape[-1])
    return jnp.broadcast_to(g3[:, row:row + 1, :], g3.shape).reshape(g.shape)


def _gla_consts(reverse):
    c = GLA_CHUNK
    t_idx = lax.broadcasted_iota(jnp.int32, (c, c), 0)
    s_idx = lax.broadcasted_iota(jnp.int32, (c, c), 1)
    pt = (c - 1 - t_idx) if reverse else t_idx
    ps = (c - 1 - s_idx) if reverse else s_idx
    tri = (ps <= pt).astype(BF16)
    levels = []
    for half in (32, 16, 8):
        bs = 2 * half
        mask = (t_idx // bs == s_idx // bs) & (pt % bs >= half) & (ps % bs < half)
        levels.append((bs, half if reverse else half - 1, False, mask))
    dmask = (t_idx // SUBLANES == s_idx // SUBLANES) & (ps <= pt)
    levels.append((SUBLANES, 4 if reverse else 3, True, dmask))
    end_row = 0 if reverse else c - 1
    return tri, levels, end_row


def _gla_kernel(qf_ref, kf_ref, lgf_ref, vf_ref, qb_ref, kb_ref, lgb_ref, vb_ref,
                of_ref, ob_ref, stf_ref, stb_ref):
    @pl.when(pl.program_id(1) == 0)
    def _():
        stf_ref[...] = jnp.zeros_like(stf_ref)
        stb_ref[...] = jnp.zeros_like(stb_ref)

    c = GLA_CHUNK
    n_chunks = GLA_BLOCK // c
    dirs = [(qf_ref, kf_ref, lgf_ref, vf_ref, of_ref, stf_ref, False),
            (qb_ref, kb_ref, lgb_ref, vb_ref, ob_ref, stb_ref, True)]
    consts = [_gla_consts(False), _gla_consts(True)]
    states = [[d[5][hd] for hd in range(N_HGRN_HEADS)] for d in dirs]
    for step0 in range(0, n_chunks, GLA_PHASE_GROUP):
      units = []
      for step in range(step0, step0 + GLA_PHASE_GROUP):
        for di, (q_ref, k_ref, lg_ref, v_ref, o_ref, _, reverse) in enumerate(dirs):
            tri, levels, end_row = consts[di]
            ci = n_chunks - 1 - step if reverse else step
            rows = slice(ci * c, (ci + 1) * c)
            lg = lg_ref[rows, :]
            hi = lg.astype(BF16)
            r1 = lg - hi.astype(F32)
            mid = r1.astype(BF16)
            lo = (r1 - mid.astype(F32)).astype(BF16)
            gc = (_dot(tri, hi) + _dot(tri, mid) + _dot(tri, lo)) * LOG2E
            for hd in range(N_HGRN_HEADS):
                hs = slice(hd * HGRN_K, (hd + 1) * HGRN_K)
                units.append(dict(di=di, hd=hd, rows=rows, hs=hs, g=gc[:, hs], o_ref=o_ref,
                                  qh=q_ref[rows, hs], kh=k_ref[rows, hs],
                                  vh=v_ref[rows, hs].astype(BF16), levels=levels, end_row=end_row,
                                  attn=jnp.zeros((c, c), F32)))
      if True:
        for li in range(4):
            for u in units:
                bs, row, diag, mask = u["levels"][li]
                g, qh, kh = u["g"], u["qh"], u["kh"]
                d = g - _bcast_block_row(g, bs, row)
                if diag:
                    qf = (qh * jnp.exp2(d)).astype(BF16)
                    kf = (kh * jnp.exp2(-d)).astype(BF16)
                else:
                    e = jnp.exp2(-jnp.abs(d))
                    qf = (qh * e).astype(BF16)
                    kf = (kh * e).astype(BF16)
                u["attn"] = jnp.where(mask, _dot_nt(qf, kf), u["attn"])
        for u in units:
            g, qh, kh, vh = u["g"], u["qh"], u["kh"], u["vh"]
            st = states[u["di"]][u["hd"]]
            g_end = g[u["end_row"]:u["end_row"] + 1, :]
            qe = (qh * jnp.exp2(g)).astype(BF16)
            ke = (kh * jnp.exp2(g_end - g)).astype(BF16)
            o = _dot_nt(qe, st.astype(BF16)) + _dot(u["attn"].astype(BF16), vh)
            u["o_ref"][u["rows"], u["hs"]] = o
            states[u["di"]][u["hd"]] = st * jnp.exp2(g_end) + _dot_tn(vh, ke)
    for di, d in enumerate(dirs):
        for hd in range(N_HGRN_HEADS):
            d[5][hd] = states[di][hd]


def _gla(qh, kf, lgf, kb, lgb, vh, dims):
    n = qh.shape[0]
    b, s, ctx = dims
    blk = GLA_BLOCK
    ncb, nsb, nlat = ctx // blk, s // blk, b * s // blk

    def fwd(bi, j):
        return (jnp.where(j < ncb, nlat + bi * ncb + j, bi * nsb + (j - ncb)), 0)

    def bwd(bi, j):
        return (jnp.where(j < ncb, nlat + bi * ncb + (ncb - 1 - j), bi * nsb + (nsb - 1 - (j - ncb))), 0)

    fs = pl.BlockSpec((blk, HGRN_W), fwd)
    bs = pl.BlockSpec((blk, HGRN_W), bwd)
    out = jax.ShapeDtypeStruct((n, HGRN_W), F32)
    return pl.pallas_call(
        _gla_kernel,
        grid=(b, ncb + nsb),
        in_specs=[fs, fs, fs, fs, bs, bs, bs, bs],
        out_specs=[fs, bs],
        out_shape=[out, out],
        scratch_shapes=[pltpu.VMEM((N_HGRN_HEADS, HGRN_K, HGRN_K), F32)] * 2,
        compiler_params=_params("parallel", "arbitrary"),
        name="gla",
    )(qh, kf, lgf, vh, qh, kb, lgb, vh)


def _outproj_kernel(*refs, lat_tiles):
    h, (a_ref, of_ref, ob_ref, hg_ref, gn_ref, w_ref, mod_ref, o_ref) = _rows_from(refs, lat_tiles)
    o = of_ref[...] + ob_ref[...]
    gate = _silu(hg_ref[...])
    ys = []
    for hd in range(N_HGRN_HEADS):
        hs = slice(hd * HGRN_K, (hd + 1) * HGRN_K)
        ys.append(_rms(o[:, hs], gn_ref[...]) * gate[:, hs])
    y = jnp.concatenate(ys, axis=-1).astype(BF16)
    mix = _dot(a_ref[...], w_ref[0:MLA_W, :]) + _dot(y, w_ref[MLA_W:, :])
    o_ref[...] = h + mod_ref[2:3, :] * mix


def _outproj(a, of, ob, hg, gn, w, h, mod, dims, n_rows):
    b, s, ctx = dims
    n, d = b * (s + ctx), D_MODEL
    tm = MLP_TILE
    row = lambda i: (i, 0)
    const = lambda i: (0, 0)
    half = pl.BlockSpec((tm, HGRN_W), row)
    h_specs, h_args, lat_tiles = _split_rows(h, tm)
    return pl.pallas_call(
        functools.partial(_outproj_kernel, lat_tiles=lat_tiles),
        grid=(n_rows // tm,),
        in_specs=h_specs + [pl.BlockSpec((tm, MLA_W), row), half, half, half,
                            pl.BlockSpec((1, HGRN_K), const), pl.BlockSpec(w.shape, const),
                            pl.BlockSpec((None, N_MOD, d), lambda i: (jnp.minimum(i * tm // s, b), 0, 0))],
        out_specs=pl.BlockSpec((tm, d), row),
        out_shape=jax.ShapeDtypeStruct((n, d), F32),
        input_output_aliases={} if lat_tiles is not None else {0: 0},
        compiler_params=_params("parallel"),
        name="outproj",
    )(*h_args, a, of, ob, hg, gn, w, mod)


def _ffn_kernel(h_ref, mod_ref, ng_ref, wg_ref, wu_ref, wd_ref, o_ref):
    h = h_ref[...]
    v = _norm_mod(h, ng_ref[...], mod_ref[3:4, :], mod_ref[4:5, :]).astype(BF16)
    hid = (_silu(_dot(v, wg_ref[...])) * _dot(v, wu_ref[...])).astype(BF16)
    o_ref[...] = h + mod_ref[5:6, :] * _dot(hid, wd_ref[...])


def _ffn(h, mod, ng, wg, wu, wd, dims, n_rows):
    n, d = h.shape
    b, s, ctx = dims
    tm = MLP_TILE
    row = lambda i: (i, 0)
    const = lambda i: (0, 0)
    return pl.pallas_call(
        _ffn_kernel,
        grid=(n_rows // tm,),
        in_specs=[pl.BlockSpec((tm, d), row),
                  pl.BlockSpec((None, N_MOD, d), lambda i: (jnp.minimum(i * tm // s, b), 0, 0)),
                  pl.BlockSpec((1, d), const),
                  pl.BlockSpec(wg.shape, const, pipeline_mode=pl.Buffered(1)),
                  pl.BlockSpec(wu.shape, const, pipeline_mode=pl.Buffered(1)),
                  pl.BlockSpec(wd.shape, const, pipeline_mode=pl.Buffered(1))],
        out_specs=pl.BlockSpec((tm, d), row),
        out_shape=jax.ShapeDtypeStruct((n, d), F32),
        input_output_aliases={0: 0},
        compiler_params=_params("parallel"),
        name="ffn",
    )(h, mod, ng, wg, wu, wd)


def _pool_kernel(h_ref, prev_ref, next_ref, mod_ref, ng_ref, w_ref, b_ref, sc_ref, o_ref,
                 *, tm, seq_tiles_lat, n_lat_tiles, seq_tiles_ctx):
    i = pl.program_id(0)
    halo = SUBLANES
    shift, scale = mod_ref[0:1, :], mod_ref[1:2, :]
    in_lat = i < n_lat_tiles
    j = jnp.where(in_lat, i % seq_tiles_lat, (i - n_lat_tiles) % seq_tiles_ctx)
    n_seq_tiles = jnp.where(in_lat, seq_tiles_lat, seq_tiles_ctx)
    h = h_ref[...]
    u = _norm_mod(h, ng_ref[...], shift, scale)
    up = _norm_mod(prev_ref[...], ng_ref[...], shift, scale)
    un = _norm_mod(next_ref[...], ng_ref[...], shift, scale)
    ext = jnp.concatenate([jnp.where(j > 0, up, 0.0), u, jnp.where(j < n_seq_tiles - 1, un, 0.0)], axis=0)
    ext_hi = ext.astype(BF16)
    ext_lo = (ext - ext_hi.astype(F32)).astype(BF16)
    off = (lax.broadcasted_iota(jnp.int32, (tm, tm + 2 * halo), 1)
           - lax.broadcasted_iota(jnp.int32, (tm, tm + 2 * halo), 0) - halo)
    pos = j * tm + lax.broadcasted_iota(jnp.int32, (tm, 1), 0)
    t_len = n_seq_tiles * tm
    css = [slice(g * POOL_C, (g + 1) * POOL_C) for g in range(len(POOL_WINDOWS))]
    bands = [((off >= -(win // 2)) & (off < win // 2)).astype(BF16) for win in POOL_WINDOWS]
    accs = [_dot(bd, ext_hi[:, cs]) + _dot(bd, ext_lo[:, cs]) for bd, cs in zip(bands, css)]
    pooled = []
    for win, cs, acc in zip(POOL_WINDOWS, css, accs):
        lo = jnp.maximum(pos - win // 2, 0)
        hi = jnp.minimum(pos - win // 2 + win, t_len)
        cnt = (hi - lo).astype(F32)
        pooled.append((acc / cnt - u[:, cs]).astype(BF16))
    outs = [_dot(pl_g, w_ref[g]) + b_ref[g:g + 1, :] for g, pl_g in enumerate(pooled)]
    y = jnp.concatenate(outs, axis=-1) * sc_ref[...]
    o_ref[...] = h + mod_ref[2:3, :] * y


def _pool(h, mod, ng, w, bias, scale, dims, n_rows):
    n, d = h.shape
    b, s, ctx = dims
    tm = ROW_TILE
    hb = tm // SUBLANES
    n_blk8 = n // SUBLANES
    row = lambda i: (i, 0)
    const = lambda i: (0, 0)
    kern = functools.partial(_pool_kernel, tm=tm, seq_tiles_lat=s // tm, n_lat_tiles=b * s // tm,
                             seq_tiles_ctx=ctx // tm)
    return pl.pallas_call(
        kern,
        grid=(n_rows // tm,),
        in_specs=[pl.BlockSpec((tm, d), row),
                  pl.BlockSpec((SUBLANES, d), lambda i: (jnp.maximum(i * hb - 1, 0), 0)),
                  pl.BlockSpec((SUBLANES, d), lambda i: (jnp.minimum((i + 1) * hb, n_blk8 - 1), 0)),
                  pl.BlockSpec((None, N_MOD, d), lambda i: (jnp.minimum(i * tm // s, b), 0, 0)),
                  pl.BlockSpec((1, d), const),
                  pl.BlockSpec(w.shape, lambda i: (0, 0, 0)),
                  pl.BlockSpec(bias.shape, const),
                  pl.BlockSpec((1, d), const)],
        out_specs=pl.BlockSpec((tm, d), row),
        out_shape=jax.ShapeDtypeStruct((n_rows, d), F32),
        compiler_params=_params("parallel"),
        name="pool",
    )(h, h, h, mod, ng, w, bias, scale)


def _route_kernel(h_ref, mod_ref, ng_ref, r_ref, v_ref, idx_ref, wt_ref, lst_ref, base_ref, run_ref):
    rt = ROW_TILE
    subs = [slice(k * rt, (k + 1) * rt) for k in range(h_ref.shape[0] // rt)]

    @pl.when(pl.program_id(0) == 0)
    def _():
        run_ref[...] = jnp.zeros_like(run_ref)

    lane = lax.broadcasted_iota(jnp.int32, (rt, LANES), 1)
    ti = lax.broadcasted_iota(jnp.int32, (rt, rt), 0)
    tj = lax.broadcasted_iota(jnp.int32, (rt, rt), 1)
    neg = jnp.float32(-jnp.inf)
    vs = [_norm_mod(h_ref[sl, :], ng_ref[...], mod_ref[3:4, :], mod_ref[4:5, :]) for sl in subs]
    v_his = [v.astype(BF16) for v in vs]
    half = vs[0].shape[-1] // 2
    for sl, v_hi in zip(subs, v_his):
        vb = pltpu.bitcast(v_hi.astype(F32), jnp.uint32)
        v_ref[sl, :] = (vb[:, :half] >> 16) | (vb[:, half:] & jnp.uint32(0xFFFF0000))
    logits = []
    for v, v_hi in zip(vs, v_his):
        v_lo = (v - v_hi.astype(F32)).astype(BF16)
        lg = _dot(v_hi, r_ref[0]) + (_dot(v_lo, r_ref[0]) + _dot(v_hi, r_ref[1]))
        logits.append(jnp.where(lane < N_EXPERTS, lg, neg))
    m1s = [jnp.max(lg, axis=-1, keepdims=True) for lg in logits]
    i1s = [jnp.min(jnp.where(lg == m1, lane, LANES), axis=-1, keepdims=True) for lg, m1 in zip(logits, m1s)]
    rests = [jnp.where(lane == i1, neg, lg) for lg, i1 in zip(logits, i1s)]
    m2s = [jnp.max(r, axis=-1, keepdims=True) for r in rests]
    i2s = [jnp.min(jnp.where(r == m2, lane, LANES), axis=-1, keepdims=True) for r, m2 in zip(rests, m2s)]
    for sl, m1, m2 in zip(subs, m1s, m2s):
        e2 = jnp.exp(m2 - m1)
        wt_ref[sl, :] = jnp.where(lane == 0, 1.0 / (1.0 + e2), jnp.where(lane == 1, e2 / (1.0 + e2), 0.0))
    ohs = [((lane == i1) | (lane == i2)).astype(BF16) for i1, i2 in zip(i1s, i2s)]
    befores = [_dot((tj < ti).astype(BF16), oh) for oh in ohs]
    befores_t = [_dot_tn(oh, (ti < tj).astype(BF16)) for oh in ohs]
    choses_t = [_dot_tn(oh, (ti == tj).astype(BF16)) for oh in ohs]
    run = run_ref[...]
    for k, sl in enumerate(subs):
        pos = run + befores[k]
        r1 = jnp.sum(jnp.where(lane == i1s[k], pos, 0.0), axis=-1, keepdims=True)
        r2 = jnp.sum(jnp.where(lane == i2s[k], pos, 0.0), axis=-1, keepdims=True)
        idx_ref[sl, :] = jnp.where(lane == 0, i1s[k], jnp.where(lane == 1, i2s[k], jnp.where(
            lane == 2, r1.astype(jnp.int32), jnp.where(lane == 3, r2.astype(jnp.int32), 0))))
        base_ref[k] = run.astype(jnp.int32)
        run = run + jnp.sum(ohs[k].astype(F32), axis=0, keepdims=True)
    run_ref[...] = run
    ids = lax.broadcasted_iota(jnp.int32, (rt, LANES), 0).astype(BF16)
    lsts = [jnp.zeros((rt, LANES), F32) for _ in subs]
    for e in range(N_EXPERTS):
        for k in range(len(subs)):
            sel = (befores_t[k][e:e + 1, :] == ti.astype(F32)) & (choses_t[k][e:e + 1, :] > 0.5)
            lsts[k] = jnp.where(lane == e, _dot(sel.astype(BF16), ids), lsts[k])
    for sl, lst in zip(subs, lsts):
        lst_ref[sl, :] = lst.astype(jnp.int32)


def _route(h, mod, ng, router, dims, n_rows):
    n, d = h.shape
    b, s, ctx = dims
    assert ROW_TILE <= 256
    nsub = ROUTE_SUBTILES
    while s % (nsub * ROW_TILE) or (n_rows - b * s) % (nsub * ROW_TILE):
        nsub //= 2
    tm = nsub * ROW_TILE
    nt = n_rows // ROW_TILE
    row = lambda i: (i, 0)
    const = lambda i: (0, 0)
    return pl.pallas_call(
        _route_kernel,
        grid=(n_rows // tm,),
        in_specs=[pl.BlockSpec((tm, d), row),
                  pl.BlockSpec((None, N_MOD, d), lambda i: (jnp.minimum(i * tm // s, b), 0, 0)),
                  pl.BlockSpec((1, d), const),
                  pl.BlockSpec((2, d, LANES), lambda i: (0, 0, 0))],
        out_specs=[pl.BlockSpec((tm, d // 2), row), pl.BlockSpec((tm, LANES), row),
                   pl.BlockSpec((tm, LANES), row), pl.BlockSpec((tm, LANES), row),
                   pl.BlockSpec((nsub, 1, LANES), lambda i: (i, 0, 0))],
        out_shape=[jax.ShapeDtypeStruct((n_rows, d // 2), jnp.uint32),
                   jax.ShapeDtypeStruct((n_rows, LANES), jnp.int32),
                   jax.ShapeDtypeStruct((n_rows, LANES), F32),
                   jax.ShapeDtypeStruct((n_rows, LANES), jnp.int32),
                   jax.ShapeDtypeStruct((nt, 1, LANES), jnp.int32)],
        scratch_shapes=[pltpu.VMEM((1, LANES), F32)],
        compiler_params=_params("arbitrary"),
        name="route",
    )(h, mod, ng, router)


def _moe_kernel(te_ref, nu_ref, x_ref, wg_ref, wu_ref, wd_ref, o_ref, xs_ref):
    i = pl.program_id(0)
    f = pl.program_id(1)

    @pl.when((i < nu_ref[0]) & (f == 0))
    def _():
        half = x_ref.shape[-1]
        xw = x_ref[...]
        xs_ref[:, :half] = pltpu.bitcast(xw << 16, F32).astype(BF16)
        xs_ref[:, half:] = pltpu.bitcast(xw & jnp.uint32(0xFFFF0000), F32).astype(BF16)
        o_ref[...] = jnp.zeros_like(o_ref)

    @pl.when(i < nu_ref[0])
    def _():
        x = xs_ref[...]
        w = wg_ref.shape[-1] // MOE_FF_SPLIT
        hids = []
        for c in range(MOE_FF_SPLIT):
            cs = slice(c * w, (c + 1) * w)
            hids.append((_silu(_dot(x, wg_ref[:, cs].astype(BF16))) *
                         _dot(x, wu_ref[:, cs].astype(BF16))).astype(BF16))
        part = None
        for c in range(MOE_FF_SPLIT):
            p = _dot(hids[c], wd_ref[c * w:(c + 1) * w, :].astype(BF16))
            part = p if part is None else part + p
        o_ref[...] += part


def _moe_experts(x_sorted, tile_expert, n_used, wg, wu, wd, layer):
    npad, dw = x_sorted.shape
    d = 2 * dw
    tm, tf = MOE_TILE, MOE_FF_TILE
    dff = wg.shape[-1]
    grid_spec = pltpu.PrefetchScalarGridSpec(
        num_scalar_prefetch=2,
        grid=(npad // tm, dff // tf),
        in_specs=[pl.BlockSpec((tm, dw), lambda i, f, te, nu: (i, 0)),
                  pl.BlockSpec((None, None, d, tf), lambda i, f, te, nu: (layer, te[i], 0, f)),
                  pl.BlockSpec((None, None, d, tf), lambda i, f, te, nu: (layer, te[i], 0, f)),
                  pl.BlockSpec((None, None, tf, d), lambda i, f, te, nu: (layer, te[i], f, 0))],
        out_specs=pl.BlockSpec((tm, d), lambda i, f, te, nu: (i, 0)),
        scratch_shapes=[pltpu.VMEM((tm, d), BF16)],
    )
    return pl.pallas_call(
        _moe_kernel,
        grid_spec=grid_spec,
        out_shape=jax.ShapeDtypeStruct((npad, d), F32),
        compiler_params=_params("parallel", "arbitrary"),
        name="moe_experts",
    )(tile_expert, n_used, x_sorted, wg, wu, wd)


def _combine_kernel(h_ref, y0_ref, y1_ref, wt_ref, mod_ref, *rest):
    y = wt_ref[:, 0:1] * y0_ref[...] + wt_ref[:, 1:2] * y1_ref[...]
    h = h_ref[...] + mod_ref[5:6, :] * y
    if len(rest) == 2:
        fg_ref, o_ref = rest
        o_ref[...] = _rms(h, fg_ref[...])
    else:
        rest[0][...] = h


def _combine(h, y0, y1, wt, mod, dims, n_rows, final_g=None):
    n, d = h.shape
    b, s, ctx = dims
    tm = ROW_TILE
    row = lambda i: (i, 0)
    in_specs = [pl.BlockSpec((tm, d), row), pl.BlockSpec((tm, d), row), pl.BlockSpec((tm, d), row),
                pl.BlockSpec((tm, LANES), row),
                pl.BlockSpec((None, N_MOD, d), lambda i: (jnp.minimum(i * tm // s, b), 0, 0))]
    args = [h, y0, y1, wt, mod]
    if final_g is not None:
        in_specs.append(pl.BlockSpec((1, d), lambda i: (0, 0)))
        args.append(final_g)
    return pl.pallas_call(
        _combine_kernel,
        grid=(n_rows // tm,),
        in_specs=in_specs,
        out_specs=pl.BlockSpec((tm, d), row),
        out_shape=jax.ShapeDtypeStruct((n_rows if final_g is not None else n, d), F32),
        input_output_aliases={} if final_g is not None else {0: 0},
        compiler_params=_params("parallel"),
        name="moe_combine",
    )(*args)


def _moe(h, mod, ng, router, wg, wu, wd, layer, dims, n_rows, final_g=None):
    tm, rt = MOE_TILE, ROW_TILE
    v, idx, wt, lst, base = _route(h, mod, ng, router, dims, n_rows)
    base = base[:, 0, :N_EXPERTS]
    e_sel, pos = idx[:, 0:2], idx[:, 2:4]
    last = e_sel[-rt:]
    counts = base[-1] + jnp.sum(last[:, :, None] == jnp.arange(N_EXPERTS), axis=(0, 1))
    tiles_per = (counts + tm - 1) // tm
    tile_end = jnp.cumsum(tiles_per)
    group_start = (tile_end - tiles_per) * tm
    slot = group_start[e_sel] + pos
    n_tiles = 2 * n_rows // tm + N_EXPERTS
    tile_expert = jnp.minimum(jnp.sum(jnp.arange(n_tiles)[:, None] >= tile_end[None, :], axis=1),
                              N_EXPERTS - 1).astype(jnp.int32)
    n_used = tile_end[-1:].astype(jnp.int32)
    rank = (jnp.arange(n_tiles * tm, dtype=jnp.int32).reshape(n_tiles, tm)
            - group_start[tile_expert][:, None])
    base_te = base.T[tile_expert]
    owns = base_te[:, None, :] <= rank[:, :, None]
    rtile = jnp.sum(owns, axis=-1) - 1
    local = rank - jnp.max(jnp.where(owns, base_te[:, None, :], 0), axis=-1)
    flat = jnp.clip((rtile * rt + local) * N_EXPERTS + tile_expert[:, None], 0, n_rows * N_EXPERTS - 1)
    src = rtile * rt + jnp.take(lst[:, :N_EXPERTS].reshape(-1), flat, mode="clip")
    spread = jnp.arange(n_tiles * tm, dtype=jnp.int32).reshape(n_tiles, tm) % n_rows
    src = jnp.where(rank < counts[tile_expert][:, None], src, spread).reshape(-1)
    x_sorted = jnp.take(v, src, axis=0, mode="clip")
    out_sorted = _moe_experts(x_sorted, tile_expert, n_used, wg, wu, wd, layer)
    y0 = jnp.take(out_sorted, slot[:, 0], axis=0, mode="clip")
    y1 = jnp.take(out_sorted, slot[:, 1], axis=0, mode="clip")
    return _combine(h, y0, y1, wt, mod, dims, n_rows, final_g)


def _final_kernel(h_ref, g_ref, o_ref):
    o_ref[...] = _rms(h_ref[...], g_ref[...])


def _final_norm(h, g, n_rows):
    n, d = h.shape
    tm = ROW_TILE
    row = lambda i: (i, 0)
    return pl.pallas_call(
        _final_kernel,
        grid=(n_rows // tm,),
        in_specs=[pl.BlockSpec((tm, d), row), pl.BlockSpec((1, d), lambda i: (0, 0))],
        out_specs=pl.BlockSpec((tm, d), row),
        out_shape=jax.ShapeDtypeStruct((n_rows, d), F32),
        compiler_params=_params("parallel"),
        name="final_norm",
    )(h, g)


def _pad_cols(w, width):
    return jnp.pad(w, ((0, 0), (0, width - w.shape[-1])))


def _pack_in(w_in):
    cq = w_in[:, :Q_LORA]
    ckv = w_in[:, Q_LORA:KR_OFF]
    kr = w_in[:, KR_OFF:KR_OFF + QK_ROPE]
    rest = w_in[:, KR_OFF + QK_ROPE:]
    return jnp.concatenate([cq, ckv, _pad_cols(kr, LANES), rest], axis=-1).astype(BF16)


def _pack_q(w_uq):
    w = w_uq.reshape(Q_LORA, N_HEADS, QK_DIM)
    z = jnp.zeros((Q_LORA, N_HEADS, HEAD_PAD - QK_DIM), w.dtype)
    return jnp.concatenate([w, z], axis=-1).reshape(Q_LORA, QKV_W).astype(BF16)


def _pack_kv(w_ukv):
    w = w_ukv.reshape(KV_LORA, N_HEADS, QK_NOPE + V_HEAD)
    z = jnp.zeros((KV_LORA, N_HEADS, HEAD_PAD - QK_NOPE), w.dtype)
    k_all = jnp.concatenate([w[..., :QK_NOPE], z], axis=-1).reshape(KV_LORA, QKV_W)
    v_all = jnp.concatenate([w[..., QK_NOPE:], z], axis=-1).reshape(KV_LORA, QKV_W)
    return jnp.concatenate([k_all, v_all], axis=-1).astype(BF16)


def _rope_tabs(s, tm):
    rows = s // GRID_W
    row = jnp.repeat(jnp.arange(rows, dtype=F32), GRID_W)
    col = jnp.tile(jnp.arange(GRID_W, dtype=F32), rows)
    half = QK_ROPE // 2
    inv = 1.0 / (ROPE_BASE ** (jnp.arange(0, half, 2, dtype=F32) / half))
    ar = row[:, None] * inv[None, :]
    ac = col[:, None] * inv[None, :]
    ang = jnp.concatenate([ar, ar, ac, ac], axis=-1)
    cos = jnp.concatenate([jnp.cos(ang), jnp.ones((tm, QK_ROPE), F32)], axis=0)
    sin = jnp.concatenate([jnp.sin(ang), jnp.zeros((tm, QK_ROPE), F32)], axis=0)
    t = cos.shape[0]
    first = (jnp.arange(QK_ROPE) % (QK_ROPE // 2)) < QK_ROPE // 4
    sin1 = jnp.where(first, -sin, 0.0)
    sin2 = jnp.where(first, 0.0, sin)
    scale = QK_DIM ** -0.5 * LOG2E

    def q_tab(rope_part, nope_val):
        return jnp.concatenate([jnp.full((t, QK_NOPE), nope_val, F32), rope_part * scale,
                                jnp.zeros((t, HEAD_PAD - QK_DIM), F32)], axis=-1)

    return (q_tab(cos, scale), q_tab(sin1, 0.0), q_tab(sin2, 0.0),
            _pad_cols(cos, LANES), _pad_cols(sin1, LANES), _pad_cols(sin2, LANES))


def kernel(x, c, ctx, c_ctx, w_mod, b_mod, norm_g, final_g, ab_w_in, ab_g_cq, ab_g_ckv, ab_w_uq, ab_w_ukv, hgrn_lb_logits, hgrn_g_norm, ab_w_out, ffn_w_gate, ffn_w_up, ffn_w_down, pool_w, pool_b, pool_scale, moe_router, moe_w_gate, moe_w_up, moe_w_down):
    b, s, d = x.shape
    ctx_len = ctx.shape[1]
    depth = w_mod.shape[0]
    dims = (b, s, ctx_len)
    n_lat, n_ctx = b * s, b * ctx_len
    assert d == D_MODEL and s % ROW_TILE == 0 and ctx_len % ROW_TILE == 0
    assert ctx_len % GLA_BLOCK == 0 and s % GLA_BLOCK == 0 and n_lat % ctx_len == 0

    h = (x.reshape(n_lat, d), ctx.reshape(n_ctx, d))
    n_groups = ((b + 1 + SUBLANES - 1) // SUBLANES) * SUBLANES
    cvec = jnp.concatenate([c, c_ctx[None, :], jnp.zeros((n_groups - b - 1, d), F32)], axis=0)
    mod_all = _mod_table(cvec, w_mod, b_mod).reshape(depth, n_groups, N_MOD, d)

    lb_p = jax.nn.softmax(hgrn_lb_logits.astype(F32), axis=0)
    lb_all = jnp.cumsum(lb_p, axis=0) - lb_p[:1]
    tabs = _rope_tabs(s, ROW_TILE)

    for l in range(depth):
        j = l // 2
        even = l % 2 == 0
        ctx_later = any(m % 2 == 0 for m in range(l + 1, depth))
        n_rows = n_lat + n_ctx if ctx_later else n_lat
        mod = mod_all[l]
        ng1, ng2 = norm_g[l, 0][None, :], norm_g[l, 1][None, :]
        if even:
            (q, k, v, qh, kf, lgf, kb, lgb, vh, hg) = _inproj(
                h, mod, ng1, _pack_in(ab_w_in[j]), ab_g_cq[j][None, :], ab_g_ckv[j][None, :],
                _pack_q(ab_w_uq[j]), _pack_kv(ab_w_ukv[j]), lb_all[j], tabs, dims)
            a = _attention_latent(q, k, v, dims)
            if ctx_later:
                a = jnp.concatenate([a, _attention_ctx(q, k, v, dims)], axis=0)
            o_f, o_b = _gla(qh, kf, lgf, kb, lgb, vh, dims)
            h = _outproj(a, o_f, o_b, hg, hgrn_g_norm[j][None, :], ab_w_out[j].astype(BF16), h, mod,
                         dims, n_rows)
            h = _ffn(h, mod, ng2, ffn_w_gate[j].astype(BF16), ffn_w_up[j].astype(BF16),
                     ffn_w_down[j].astype(BF16), dims, n_rows)
        else:
            h = _pool(h, mod, ng1, pool_w[j].astype(BF16), pool_b[j], pool_scale[j][None, :], dims, n_rows)
            r_full = _pad_cols(moe_router[j], LANES)
            r_hi = r_full.astype(BF16)
            router = jnp.stack([r_hi, (r_full - r_hi.astype(F32)).astype(BF16)])
            last = l == depth - 1
            h = _moe(h, mod, ng2, router, moe_w_gate, moe_w_up, moe_w_down, j, dims, n_rows,
                     final_g[None, :] if last else None)
            if last:
                return h.reshape(b, s, d)
    return _final_norm(h, final_g[None, :], n_lat).reshape(b, s, d)
```

```python
import functools

import jax
import jax.numpy as jnp
from jax import lax
from jax.experimental import pallas as pl
from jax.experimental.pallas import tpu as pltpu

F32 = jnp.float32
BF16 = jnp.bfloat16

D_MODEL = 1024
GRID_W = 64
N_MOD = 6
EPS = 1e-6
N_HEADS = 8
V_HEAD = 64
QK_NOPE = 64
QK_ROPE = 32
QK_DIM = QK_NOPE + QK_ROPE
Q_LORA = 384
KV_LORA = 256
ROPE_BASE = 10000.0
MLA_W = 512
HGRN_W = 512
HGRN_K = 128
N_HGRN_HEADS = 4
POOL_WINDOWS = (2, 4, 8, 16)
POOL_C = 256
N_EXPERTS = 8

LANES = 128
SUBLANES = 8
VMEM_LIMIT_BYTES = 56 * 1024 * 1024

HEAD_PAD = LANES
QKV_W = N_HEADS * HEAD_PAD
KR_OFF = Q_LORA + KV_LORA
H_OFF = KR_OFF + LANES
IN_COLS = H_OFF + 5 * HGRN_W
GLA_CHUNK = 64
GLA_BLOCK = 256
GLA_PHASE_GROUP = 2
ROW_TILE = 256
ROUTE_SUBTILES = 4
MLP_TILE = 512
ATTN_Q_TILE = 512
ATTN_HEAD_GROUP = 4
MOE_TILE = 1024
MOE_FF_TILE = 512
MOE_FF_SPLIT = 2
LOG2E = 1.4426950408889634


def _params(*sem):
    return pltpu.CompilerParams(dimension_semantics=sem, vmem_limit_bytes=VMEM_LIMIT_BYTES)


def _dot(a, b):
    return jnp.dot(a, b, preferred_element_type=F32)


def _dot_nt(a, b):
    return lax.dot_general(a, b, (((1,), (1,)), ((), ())), preferred_element_type=F32)


def _dot_tn(a, b):
    return lax.dot_general(a, b, (((0,), (0,)), ((), ())), preferred_element_type=F32)


def _rms(x, g):
    return x * lax.rsqrt(jnp.mean(x * x, axis=-1, keepdims=True) + EPS) * g


def _silu(x):
    return x * jax.nn.sigmoid(x)


def _norm_mod(h, g, shift, scale):
    return _rms(h, g) * (1.0 + scale) + shift


def _mod_kernel(c_ref, w_ref, b_ref, o_ref):
    c = c_ref[...]
    o_ref[...] = jnp.dot(_silu(c), w_ref[...], preferred_element_type=F32,
                         precision=lax.Precision.HIGHEST) + b_ref[...]


def _mod_table(cvec, w_mod, b_mod):
    depth, d, nd = w_mod.shape
    g = cvec.shape[0]
    tn = 1536
    return pl.pallas_call(
        _mod_kernel,
        grid=(depth, nd // tn),
        in_specs=[pl.BlockSpec((g, d), lambda l, j: (0, 0)),
                  pl.BlockSpec((None, d, tn), lambda l, j: (l, 0, j)),
                  pl.BlockSpec((None, 1, tn), lambda l, j: (l, 0, j))],
        out_specs=pl.BlockSpec((None, g, tn), lambda l, j: (l, 0, j)),
        out_shape=jax.ShapeDtypeStruct((depth, g, nd), F32),
        compiler_params=_params("parallel", "parallel"),
        name="mod_table",
    )(cvec, w_mod, b_mod.reshape(depth, 1, nd))


def _inproj_body(h, mod_ref, ng_ref, win_ref, gcq_ref, gckv_ref, wq_ref, wkv_ref,
                   lb_ref, cq_tab, s1q_tab, s2q_tab, ck_tab, s1k_tab, s2k_tab,
                   q_ref, k_ref, v_ref, qh_ref, kf_ref, lgf_ref, kb_ref, lgb_ref, vh_ref, hg_ref):
    half_rot = QK_ROPE // 4

    def rope(x, c_tab, s1_tab, s2_tab):
        return (x * c_tab[...] + pltpu.roll(x, LANES - half_rot, 1) * s1_tab[...]
                + pltpu.roll(x, half_rot, 1) * s2_tab[...])

    u = _norm_mod(h, ng_ref[...], mod_ref[0:1, :], mod_ref[1:2, :]).astype(BF16)
    p = _dot(u, win_ref[...])
    cqn = _rms(p[:, 0:Q_LORA], gcq_ref[...]).astype(BF16)
    a = _dot(cqn, wq_ref[...])
    q_ref[...] = jnp.concatenate(
        [rope(a[:, hd * HEAD_PAD:(hd + 1) * HEAD_PAD], cq_tab, s1q_tab, s2q_tab) for hd in range(N_HEADS)],
        axis=-1).astype(BF16)
    ckvn = _rms(p[:, Q_LORA:KR_OFF], gckv_ref[...]).astype(BF16)
    kv = _dot(ckvn, wkv_ref[...])
    kr = pltpu.roll(rope(p[:, KR_OFF:H_OFF], ck_tab, s1k_tab, s2k_tab), QK_NOPE, 1)
    k_ref[...] = jnp.concatenate(
        [kv[:, hd * HEAD_PAD:(hd + 1) * HEAD_PAD] + kr for hd in range(N_HEADS)], axis=-1).astype(BF16)
    lane = lax.broadcasted_iota(jnp.int32, (1, QKV_W), 1)
    ones_col = (lane % HEAD_PAD == V_HEAD).astype(F32)
    v_ref[...] = (kv[:, QKV_W:] + ones_col).astype(BF16)
    o = H_OFF
    qh_ref[...] = _silu(p[:, o:o + HGRN_W])
    kf = (1.0 - lb_ref[0:1, :]) * jax.nn.sigmoid(-p[:, o + HGRN_W:o + 2 * HGRN_W])
    kf_ref[...] = kf
    lgf_ref[...] = jnp.log1p(-kf)
    kb = (1.0 - lb_ref[1:2, :]) * jax.nn.sigmoid(-p[:, o + 2 * HGRN_W:o + 3 * HGRN_W])
    kb_ref[...] = kb
    lgb_ref[...] = jnp.log1p(-kb)
    vh_ref[...] = p[:, o + 3 * HGRN_W:o + 4 * HGRN_W]
    hg_ref[...] = p[:, o + 4 * HGRN_W:o + 5 * HGRN_W]


def _rows_from(refs, lat_tiles):
    if lat_tiles is None:
        return refs[0][...], refs[1:]
    return jnp.where(pl.program_id(0) < lat_tiles, refs[0][...], refs[1][...]), refs[2:]


def _inproj_kernel(*refs, lat_tiles, pending):
    h, rest = _rows_from(refs, lat_tiles)
    if pending:
        y0_ref, y1_ref, wt_ref, pmod_ref, *rest = rest
        *rest, hnew_ref = rest
        h = h + pmod_ref[5:6, :] * (wt_ref[:, 0:1] * y0_ref[...] + wt_ref[:, 1:2] * y1_ref[...])
        hnew_ref[...] = h
    _inproj_body(h, *rest)


def _split_rows(h, tm):
    if not isinstance(h, tuple):
        return [pl.BlockSpec((tm, h.shape[1]), lambda i: (i, 0))], [h], None
    x2, c2 = h
    lat_tiles = x2.shape[0] // tm
    specs = [pl.BlockSpec((tm, x2.shape[1]), lambda i: (jnp.minimum(i, lat_tiles - 1), 0)),
             pl.BlockSpec((tm, c2.shape[1]), lambda i: (jnp.maximum(i - lat_tiles, 0), 0))]
    return specs, [x2, c2], lat_tiles


def _inproj(h, mod, ng, win, gcq, gckv, wq, wkv, lb, tabs, dims, pending=None):
    b, s, ctx = dims
    n, d = b * (s + ctx), D_MODEL
    tm = ROW_TILE
    h_specs, h_args, lat_tiles = _split_rows(h, tm)
    extra_specs, extra_args, extra_out_specs, extra_out_shape, aliases = [], [], [], [], {}
    if pending is not None:
        extra_specs = [pl.BlockSpec((tm, d), lambda i: (i, 0)), pl.BlockSpec((tm, d), lambda i: (i, 0)),
                       pl.BlockSpec((tm, LANES), lambda i: (i, 0)),
                       pl.BlockSpec((None, N_MOD, d), lambda i: (jnp.minimum(i * tm // s, b), 0, 0))]
        extra_args = list(pending)
        extra_out_specs = [pl.BlockSpec((tm, d), lambda i: (i, 0))]
        extra_out_shape = [jax.ShapeDtypeStruct((n, d), F32)]
        aliases = {0: 10}
    n_lat_t = b * s // tm
    pos_blocks = s // tm

    def grp(i):
        return jnp.minimum(i * tm // s, b)

    def pos(i):
        return jnp.where(i < n_lat_t, i % pos_blocks, pos_blocks)

    row = lambda i: (i, 0)
    const = lambda i: (0, 0)
    tab_spec = pl.BlockSpec((tm, LANES), lambda i: (pos(i), 0))
    wide = jax.ShapeDtypeStruct((n, QKV_W), BF16)
    hg = jax.ShapeDtypeStruct((n, HGRN_W), F32)
    return pl.pallas_call(
        functools.partial(_inproj_kernel, lat_tiles=lat_tiles, pending=pending is not None),
        grid=(n // tm,),
        in_specs=h_specs + extra_specs + [
                  pl.BlockSpec((None, N_MOD, d), lambda i: (grp(i), 0, 0)),
                  pl.BlockSpec((1, d), const),
                  pl.BlockSpec(win.shape, const),
                  pl.BlockSpec((1, Q_LORA), const),
                  pl.BlockSpec((1, KV_LORA), const),
                  pl.BlockSpec(wq.shape, const),
                  pl.BlockSpec(wkv.shape, const),
                  pl.BlockSpec((2, HGRN_W), const),
                  tab_spec, tab_spec, tab_spec, tab_spec, tab_spec, tab_spec],
        out_specs=([pl.BlockSpec((tm, QKV_W), row)] * 3 + [pl.BlockSpec((tm, HGRN_W), row)] * 7
                   + extra_out_specs),
        out_shape=[wide] * 3 + [hg] * 7 + extra_out_shape,
        input_output_aliases=aliases,
        compiler_params=_params("parallel"),
        name="inproj",
    )(*h_args, *extra_args, mod, ng, win, gcq, gckv, wq, wkv, lb, *tabs)


def _attn_kernel(*refs, n_kv):
    q_ref = refs[0]
    k_refs = refs[1:1 + 2 * n_kv:2]
    v_refs = refs[2:2 + 2 * n_kv:2]
    o_ref = refs[-1]
    for h0 in range(0, N_HEADS, ATTN_HEAD_GROUP):
        heads = range(h0, h0 + ATTN_HEAD_GROUP)
        sls = {hd: slice(hd * HEAD_PAD, (hd + 1) * HEAD_PAD) for hd in heads}
        scores = {hd: [_dot_nt(q_ref[:, sls[hd]], k_ref[:, sls[hd]]) for k_ref in k_refs] for hd in heads}
        ms = {hd: functools.reduce(jnp.maximum, [jnp.max(sc, axis=-1, keepdims=True) for sc in scores[hd]])
              for hd in heads}
        for hd in heads:
            acc = None
            for sc, v_ref in zip(scores[hd], v_refs):
                part = _dot(jnp.exp2(sc - ms[hd]).astype(BF16), v_ref[:, sls[hd]])
                acc = part if acc is None else acc + part
            o = acc[:, :V_HEAD] / acc[:, V_HEAD:V_HEAD + 1]
            o_ref[:, hd * V_HEAD:(hd + 1) * V_HEAD] = o.astype(o_ref.dtype)


def _attention_latent(q, k, v, dims):
    b, s, ctx = dims
    tq = min(ATTN_Q_TILE, s)
    nq = s // tq
    lat = lambda bi, j: (bi, 0)
    cx = lambda bi, j: (b * s // ctx + bi, 0)
    return pl.pallas_call(
        functools.partial(_attn_kernel, n_kv=2),
        grid=(b, nq),
        in_specs=[pl.BlockSpec((tq, QKV_W), lambda bi, j: (bi * nq + j, 0)),
                  pl.BlockSpec((s, QKV_W), lat), pl.BlockSpec((s, QKV_W), lat),
                  pl.BlockSpec((ctx, QKV_W), cx), pl.BlockSpec((ctx, QKV_W), cx)],
        out_specs=pl.BlockSpec((tq, MLA_W), lambda bi, j: (bi * nq + j, 0)),
        out_shape=jax.ShapeDtypeStruct((b * s, MLA_W), BF16),
        compiler_params=_params("parallel", "arbitrary"),
        name="attn_latent",
    )(q, k, v, k, v)


def _attention_ctx(q, k, v, dims):
    b, s, ctx = dims
    cx = lambda bi: (b * s // ctx + bi, 0)
    return pl.pallas_call(
        functools.partial(_attn_kernel, n_kv=1),
        grid=(b,),
        in_specs=[pl.BlockSpec((ctx, QKV_W), cx), pl.BlockSpec((ctx, QKV_W), cx),
                  pl.BlockSpec((ctx, QKV_W), cx)],
        out_specs=pl.BlockSpec((ctx, MLA_W), lambda bi: (bi, 0)),
        out_shape=jax.ShapeDtypeStruct((b * ctx, MLA_W), BF16),
        compiler_params=_params("parallel"),
        name="attn_ctx",
    )(q, k, v)


def _bcast_block_row(g, bs, row):
    if bs == GLA_CHUNK:
        return g[row:row + 1, :]
    g3 = g.reshape(GLA_CHUNK // bs, bs, g.shape[-1])
    return jnp.broadcast_to(g3[:, row:row + 1, :], g3.shape).reshape(g.shape)


def _gla_consts(reverse):
    c = GLA_CHUNK
    t_idx = lax.broadcasted_iota(jnp.int32, (c, c), 0)
    s_idx = lax.broadcasted_iota(jnp.int32, (c, c), 1)
    pt = (c - 1 - t_idx) if reverse else t_idx
    ps = (c - 1 - s_idx) if reverse else s_idx
    tri = (ps <= pt).astype(BF16)
    levels = []
    for half in (32, 16, 8):
        bs = 2 * half
        mask = (t_idx // bs == s_idx // bs) & (pt % bs >= half) & (ps % bs < half)
        levels.append((bs, half if reverse else half - 1, False, mask))
    dmask = (t_idx // SUBLANES == s_idx // SUBLANES) & (ps <= pt)
    levels.append((SUBLANES, 4 if reverse else 3, True, dmask))
    end_row = 0 if reverse else c - 1
    return tri, levels, end_row


def _gla_kernel(qf_ref, kf_ref, lgf_ref, vf_ref, qb_ref, kb_ref, lgb_ref, vb_ref,
                of_ref, ob_ref, stf_ref, stb_ref):
    @pl.when(pl.program_id(1) == 0)
    def _():
        stf_ref[...] = jnp.zeros_like(stf_ref)
        stb_ref[...] = jnp.zeros_like(stb_ref)

    c = GLA_CHUNK
    n_chunks = GLA_BLOCK // c
    dirs = [(qf_ref, kf_ref, lgf_ref, vf_ref, of_ref, stf_ref, False),
            (qb_ref, kb_ref, lgb_ref, vb_ref, ob_ref, stb_ref, True)]
    consts = [_gla_consts(False), _gla_consts(True)]
    states = [[d[5][hd] for hd in range(N_HGRN_HEADS)] for d in dirs]
    for step0 in range(0, n_chunks, GLA_PHASE_GROUP):
      units = []
      for step in range(step0, step0 + GLA_PHASE_GROUP):
        for di, (q_ref, k_ref, lg_ref, v_ref, o_ref, _, reverse) in enumerate(dirs):
            tri, levels, end_row = consts[di]
            ci = n_chunks - 1 - step if reverse else step
            rows = slice(ci * c, (ci + 1) * c)
            lg = lg_ref[rows, :]
            hi = lg.astype(BF16)
            r1 = lg - hi.astype(F32)
            mid = r1.astype(BF16)
            lo = (r1 - mid.astype(F32)).astype(BF16)
            gc = (_dot(tri, hi) + _dot(tri, mid) + _dot(tri, lo)) * LOG2E
            for hd in range(N_HGRN_HEADS):
                hs = slice(hd * HGRN_K, (hd + 1) * HGRN_K)
                units.append(dict(di=di, hd=hd, rows=rows, hs=hs, g=gc[:, hs], o_ref=o_ref,
                                  qh=q_ref[rows, hs], kh=k_ref[rows, hs],
                                  vh=v_ref[rows, hs].astype(BF16), levels=levels, end_row=end_row,
                                  attn=jnp.zeros((c, c), F32)))
      if True:
        for li in range(4):
            for u in units:
                bs, row, diag, mask = u["levels"][li]
                g, qh, kh = u["g"], u["qh"], u["kh"]
                d = g - _bcast_block_row(g, bs, row)
                if diag:
                    qf = (qh * jnp.exp2(d)).astype(BF16)
                    kf = (kh * jnp.exp2(-d)).astype(BF16)
                else:
                    e = jnp.exp2(-jnp.abs(d))
                    qf = (qh * e).astype(BF16)
                    kf = (kh * e).astype(BF16)
                u["attn"] = jnp.where(mask, _dot_nt(qf, kf), u["attn"])
        for u in units:
            g, qh, kh, vh = u["g"], u["qh"], u["kh"], u["vh"]
            st = states[u["di"]][u["hd"]]
            g_end = g[u["end_row"]:u["end_row"] + 1, :]
            qe = (qh * jnp.exp2(g)).astype(BF16)
            ke = (kh * jnp.exp2(g_end - g)).astype(BF16)
            o = _dot_nt(qe, st.astype(BF16)) + _dot(u["attn"].astype(BF16), vh)
            u["o_ref"][u["rows"], u["hs"]] = o
            states[u["di"]][u["hd"]] = st * jnp.exp2(g_end) + _dot_tn(vh, ke)
    for di, d in enumerate(dirs):
        for hd in range(N_HGRN_HEADS):
            d[5][hd] = states[di][hd]


def _gla(qh, kf, lgf, kb, lgb, vh, dims):
    n = qh.shape[0]
    b, s, ctx = dims
    blk = GLA_BLOCK
    ncb, nsb, nlat = ctx // blk, s // blk, b * s // blk

    def fwd(bi, j):
        return (jnp.where(j < ncb, nlat + bi * ncb + j, bi * nsb + (j - ncb)), 0)

    def bwd(bi, j):
        return (jnp.where(j < ncb, nlat + bi * ncb + (ncb - 1 - j), bi * nsb + (nsb - 1 - (j - ncb))), 0)

    fs = pl.BlockSpec((blk, HGRN_W), fwd)
    bs = pl.BlockSpec((blk, HGRN_W), bwd)
    out = jax.ShapeDtypeStruct((n, HGRN_W), F32)
    return pl.pallas_call(
        _gla_kernel,
        grid=(b, ncb + nsb),
        in_specs=[fs, fs, fs, fs, bs, bs, bs, bs],
        out_specs=[fs, bs],
        out_shape=[out, out],
        scratch_shapes=[pltpu.VMEM((N_HGRN_HEADS, HGRN_K, HGRN_K), F32)] * 2,
        compiler_params=_params("parallel", "arbitrary"),
        name="gla",
    )(qh, kf, lgf, vh, qh, kb, lgb, vh)


def _mix_ffn_kernel(*refs, lat_tiles):
    h, (a_ref, of_ref, ob_ref, hg_ref, gn_ref, wo_ref, mod_ref, ng_ref, wg_ref, wu_ref, wd_ref,
        o_ref) = _rows_from(refs, lat_tiles)
    rows = h.shape[0] // 2
    halves = [slice(r * rows, (r + 1) * rows) for r in range(2)]
    mixed = []
    for rs in halves:
        o = of_ref[rs, :] + ob_ref[rs, :]
        gate = _silu(hg_ref[rs, :])
        ys = []
        for hd in range(N_HGRN_HEADS):
            hs = slice(hd * HGRN_K, (hd + 1) * HGRN_K)
            ys.append(_rms(o[:, hs], gn_ref[...]) * gate[:, hs])
        y = jnp.concatenate(ys, axis=-1).astype(BF16)
        mix = _dot(a_ref[rs, :], wo_ref[0:MLA_W, :]) + _dot(y, wo_ref[MLA_W:, :])
        mixed.append(h[rs, :] + mod_ref[2:3, :] * mix)
    for rs, h1 in zip(halves, mixed):
        v = _norm_mod(h1, ng_ref[...], mod_ref[3:4, :], mod_ref[4:5, :]).astype(BF16)
        hid = (_silu(_dot(v, wg_ref[...])) * _dot(v, wu_ref[...])).astype(BF16)
        o_ref[rs, :] = h1 + mod_ref[5:6, :] * _dot(hid, wd_ref[...])


def _mix_ffn(a, of, ob, hg, gn, wo, h, mod, ng, wg, wu, wd, dims, n_rows):
    b, s, ctx = dims
    n, d = b * (s + ctx), D_MODEL
    tm = MLP_TILE
    row = lambda i: (i, 0)
    const = lambda i: (0, 0)
    half = pl.BlockSpec((tm, HGRN_W), row)
    resident = lambda w: pl.BlockSpec(w.shape, const, pipeline_mode=pl.Buffered(1))
    h_specs, h_args, lat_tiles = _split_rows(h, tm)
    return pl.pallas_call(
        functools.partial(_mix_ffn_kernel, lat_tiles=lat_tiles),
        grid=(n_rows // tm,),
        in_specs=h_specs + [pl.BlockSpec((tm, MLA_W), row), half, half, half,
                            pl.BlockSpec((1, HGRN_K), const), resident(wo),
                            pl.BlockSpec((None, N_MOD, d), lambda i: (jnp.minimum(i * tm // s, b), 0, 0)),
                            pl.BlockSpec((1, d), const), resident(wg), resident(wu), resident(wd)],
        out_specs=pl.BlockSpec((tm, d), row),
        out_shape=jax.ShapeDtypeStruct((n, d), F32),
        input_output_aliases={} if lat_tiles is not None else {0: 0},
        compiler_params=_params("parallel"),
        name="mix_ffn",
    )(*h_args, a, of, ob, hg, gn, wo, mod, ng, wg, wu, wd)


def _pool_kernel(h_ref, prev_ref, next_ref, mod_ref, ng_ref, w_ref, b_ref, sc_ref, o_ref,
                 *, tm, seq_tiles_lat, n_lat_tiles, seq_tiles_ctx):
    i = pl.program_id(0)
    halo = SUBLANES
    shift, scale = mod_ref[0:1, :], mod_ref[1:2, :]
    in_lat = i < n_lat_tiles
    j = jnp.where(in_lat, i % seq_tiles_lat, (i - n_lat_tiles) % seq_tiles_ctx)
    n_seq_tiles = jnp.where(in_lat, seq_tiles_lat, seq_tiles_ctx)
    h = h_ref[...]
    u = _norm_mod(h, ng_ref[...], shift, scale)
    up = _norm_mod(prev_ref[...], ng_ref[...], shift, scale)
    un = _norm_mod(next_ref[...], ng_ref[...], shift, scale)
    ext = jnp.concatenate([jnp.where(j > 0, up, 0.0), u, jnp.where(j < n_seq_tiles - 1, un, 0.0)], axis=0)
    ext_hi = ext.astype(BF16)
    ext_lo = (ext - ext_hi.astype(F32)).astype(BF16)
    off = (lax.broadcasted_iota(jnp.int32, (tm, tm + 2 * halo), 1)
           - lax.broadcasted_iota(jnp.int32, (tm, tm + 2 * halo), 0) - halo)
    pos = j * tm + lax.broadcasted_iota(jnp.int32, (tm, 1), 0)
    t_len = n_seq_tiles * tm
    css = [slice(g * POOL_C, (g + 1) * POOL_C) for g in range(len(POOL_WINDOWS))]
    bands = [((off >= -(win // 2)) & (off < win // 2)).astype(BF16) for win in POOL_WINDOWS]
    accs = [_dot(bd, ext_hi[:, cs]) + _dot(bd, ext_lo[:, cs]) for bd, cs in zip(bands, css)]
    pooled = []
    for win, cs, acc in zip(POOL_WINDOWS, css, accs):
        lo = jnp.maximum(pos - win // 2, 0)
        hi = jnp.minimum(pos - win // 2 + win, t_len)
        cnt = (hi - lo).astype(F32)
        pooled.append((acc / cnt - u[:, cs]).astype(BF16))
    outs = [_dot(pl_g, w_ref[g]) + b_ref[g:g + 1, :] for g, pl_g in enumerate(pooled)]
    y = jnp.concatenate(outs, axis=-1) * sc_ref[...]
    o_ref[...] = h + mod_ref[2:3, :] * y


def _pool(h, mod, ng, w, bias, scale, dims, n_rows):
    n, d = h.shape
    b, s, ctx = dims
    tm = ROW_TILE
    hb = tm // SUBLANES
    n_blk8 = n // SUBLANES
    row = lambda i: (i, 0)
    const = lambda i: (0, 0)
    kern = functools.partial(_pool_kernel, tm=tm, seq_tiles_lat=s // tm, n_lat_tiles=b * s // tm,
                             seq_tiles_ctx=ctx // tm)
    return pl.pallas_call(
        kern,
        grid=(n_rows // tm,),
        in_specs=[pl.BlockSpec((tm, d), row),
                  pl.BlockSpec((SUBLANES, d), lambda i: (jnp.maximum(i * hb - 1, 0), 0)),
                  pl.BlockSpec((SUBLANES, d), lambda i: (jnp.minimum((i + 1) * hb, n_blk8 - 1), 0)),
                  pl.BlockSpec((None, N_MOD, d), lambda i: (jnp.minimum(i * tm // s, b), 0, 0)),
                  pl.BlockSpec((1, d), const),
                  pl.BlockSpec(w.shape, lambda i: (0, 0, 0)),
                  pl.BlockSpec(bias.shape, const),
                  pl.BlockSpec((1, d), const)],
        out_specs=pl.BlockSpec((tm, d), row),
        out_shape=jax.ShapeDtypeStruct((n_rows, d), F32),
        compiler_params=_params("parallel"),
        name="pool",
    )(h, h, h, mod, ng, w, bias, scale)


def _route_kernel(h_ref, mod_ref, ng_ref, r_ref, v_ref, idx_ref, wt_ref, lst_ref, base_ref, run_ref):
    rt = ROW_TILE
    subs = [slice(k * rt, (k + 1) * rt) for k in range(h_ref.shape[0] // rt)]

    @pl.when(pl.program_id(0) == 0)
    def _():
        run_ref[...] = jnp.zeros_like(run_ref)

    lane = lax.broadcasted_iota(jnp.int32, (rt, LANES), 1)
    ti = lax.broadcasted_iota(jnp.int32, (rt, rt), 0)
    tj = lax.broadcasted_iota(jnp.int32, (rt, rt), 1)
    neg = jnp.float32(-jnp.inf)
    vs = [_norm_mod(h_ref[sl, :], ng_ref[...], mod_ref[3:4, :], mod_ref[4:5, :]) for sl in subs]
    v_his = [v.astype(BF16) for v in vs]
    half = vs[0].shape[-1] // 2
    for sl, v_hi in zip(subs, v_his):
        vb = pltpu.bitcast(v_hi.astype(F32), jnp.uint32)
        v_ref[sl, :] = (vb[:, :half] >> 16) | (vb[:, half:] & jnp.uint32(0xFFFF0000))
    logits = []
    for v, v_hi in zip(vs, v_his):
        v_lo = (v - v_hi.astype(F32)).astype(BF16)
        lg = _dot(v_hi, r_ref[0]) + (_dot(v_lo, r_ref[0]) + _dot(v_hi, r_ref[1]))
        logits.append(jnp.where(lane < N_EXPERTS, lg, neg))
    m1s = [jnp.max(lg, axis=-1, keepdims=True) for lg in logits]
    i1s = [jnp.min(jnp.where(lg == m1, lane, LANES), axis=-1, keepdims=True) for lg, m1 in zip(logits, m1s)]
    rests = [jnp.where(lane == i1, neg, lg) for lg, i1 in zip(logits, i1s)]
    m2s = [jnp.max(r, axis=-1, keepdims=True) for r in rests]
    i2s = [jnp.min(jnp.where(r == m2, lane, LANES), axis=-1, keepdims=True) for r, m2 in zip(rests, m2s)]
    for sl, m1, m2 in zip(subs, m1s, m2s):
        e2 = jnp.exp(m2 - m1)
        wt_ref[sl, :] = jnp.where(lane == 0, 1.0 / (1.0 + e2), jnp.where(lane == 1, e2 / (1.0 + e2), 0.0))
    ohs = [((lane == i1) | (lane == i2)).astype(BF16) for i1, i2 in zip(i1s, i2s)]
    befores = [_dot((tj < ti).astype(BF16), oh) for oh in ohs]
    befores_t = [_dot_tn(oh, (ti < tj).astype(BF16)) for oh in ohs]
    choses_t = [_dot_tn(oh, (ti == tj).astype(BF16)) for oh in ohs]
    run = run_ref[...]
    for k, sl in enumerate(subs):
        pos = run + befores[k]
        r1 = jnp.sum(jnp.where(lane == i1s[k], pos, 0.0), axis=-1, keepdims=True)
        r2 = jnp.sum(jnp.where(lane == i2s[k], pos, 0.0), axis=-1, keepdims=True)
        idx_ref[sl, :] = jnp.where(lane == 0, i1s[k], jnp.where(lane == 1, i2s[k], jnp.where(
            lane == 2, r1.astype(jnp.int32), jnp.where(lane == 3, r2.astype(jnp.int32), 0))))
        base_ref[k] = run.astype(jnp.int32)
        run = run + jnp.sum(ohs[k].astype(F32), axis=0, keepdims=True)
    run_ref[...] = run
    ids = lax.broadcasted_iota(jnp.int32, (rt, LANES), 0).astype(BF16)
    lsts = [jnp.zeros((rt, LANES), F32) for _ in subs]
    for e in range(N_EXPERTS):
        for k in range(len(subs)):
            sel = (befores_t[k][e:e + 1, :] == ti.astype(F32)) & (choses_t[k][e:e + 1, :] > 0.5)
            lsts[k] = jnp.where(lane == e, _dot(sel.astype(BF16), ids), lsts[k])
    for sl, lst in zip(subs, lsts):
        lst_ref[sl, :] = lst.astype(jnp.int32)


def _route(h, mod, ng, router, dims, n_rows):
    n, d = h.shape
    b, s, ctx = dims
    assert ROW_TILE <= 256
    nsub = ROUTE_SUBTILES
    while s % (nsub * ROW_TILE) or (n_rows - b * s) % (nsub * ROW_TILE):
        nsub //= 2
    tm = nsub * ROW_TILE
    nt = n_rows // ROW_TILE
    row = lambda i: (i, 0)
    const = lambda i: (0, 0)
    return pl.pallas_call(
        _route_kernel,
        grid=(n_rows // tm,),
        in_specs=[pl.BlockSpec((tm, d), row),
                  pl.BlockSpec((None, N_MOD, d), lambda i: (jnp.minimum(i * tm // s, b), 0, 0)),
                  pl.BlockSpec((1, d), const),
                  pl.BlockSpec((2, d, LANES), lambda i: (0, 0, 0))],
        out_specs=[pl.BlockSpec((tm, d // 2), row), pl.BlockSpec((tm, LANES), row),
                   pl.BlockSpec((tm, LANES), row), pl.BlockSpec((tm, LANES), row),
                   pl.BlockSpec((nsub, 1, LANES), lambda i: (i, 0, 0))],
        out_shape=[jax.ShapeDtypeStruct((n_rows, d // 2), jnp.uint32),
                   jax.ShapeDtypeStruct((n_rows, LANES), jnp.int32),
                   jax.ShapeDtypeStruct((n_rows, LANES), F32),
                   jax.ShapeDtypeStruct((n_rows, LANES), jnp.int32),
                   jax.ShapeDtypeStruct((nt, 1, LANES), jnp.int32)],
        scratch_shapes=[pltpu.VMEM((1, LANES), F32)],
        compiler_params=_params("arbitrary"),
        name="route",
    )(h, mod, ng, router)


def _moe_kernel(te_ref, nu_ref, x_ref, wg_ref, wu_ref, wd_ref, o_ref, xs_ref):
    i = pl.program_id(0)
    f = pl.program_id(1)

    @pl.when((i < nu_ref[0]) & (f == 0))
    def _():
        half = x_ref.shape[-1]
        xw = x_ref[...]
        xs_ref[:, :half] = pltpu.bitcast(xw << 16, F32).astype(BF16)
        xs_ref[:, half:] = pltpu.bitcast(xw & jnp.uint32(0xFFFF0000), F32).astype(BF16)
        o_ref[...] = jnp.zeros_like(o_ref)

    @pl.when(i < nu_ref[0])
    def _():
        x = xs_ref[...]
        w = wg_ref.shape[-1] // MOE_FF_SPLIT
        hids = []
        for c in range(MOE_FF_SPLIT):
            cs = slice(c * w, (c + 1) * w)
            hids.append((_silu(_dot(x, wg_ref[:, cs].astype(BF16))) *
                         _dot(x, wu_ref[:, cs].astype(BF16))).astype(BF16))
        part = None
        for c in range(MOE_FF_SPLIT):
            p = _dot(hids[c], wd_ref[c * w:(c + 1) * w, :].astype(BF16))
            part = p if part is None else part + p
        o_ref[...] += part


def _moe_experts(x_sorted, tile_expert, n_used, wg, wu, wd, layer):
    npad, dw = x_sorted.shape
    d = 2 * dw
    tm, tf = MOE_TILE, MOE_FF_TILE
    dff = wg.shape[-1]
    grid_spec = pltpu.PrefetchScalarGridSpec(
        num_scalar_prefetch=2,
        grid=(npad // tm, dff // tf),
        in_specs=[pl.BlockSpec((tm, dw), lambda i, f, te, nu: (i, 0)),
                  pl.BlockSpec((None, None, d, tf), lambda i, f, te, nu: (layer, te[i], 0, f)),
                  pl.BlockSpec((None, None, d, tf), lambda i, f, te, nu: (layer, te[i], 0, f)),
                  pl.BlockSpec((None, None, tf, d), lambda i, f, te, nu: (layer, te[i], f, 0))],
        out_specs=pl.BlockSpec((tm, d), lambda i, f, te, nu: (i, 0)),
        scratch_shapes=[pltpu.VMEM((tm, d), BF16)],
    )
    return pl.pallas_call(
        _moe_kernel,
        grid_spec=grid_spec,
        out_shape=jax.ShapeDtypeStruct((npad, d), F32),
        compiler_params=_params("parallel", "arbitrary"),
        name="moe_experts",
    )(tile_expert, n_used, x_sorted, wg, wu, wd)


def _combine_kernel(h_ref, y0_ref, y1_ref, wt_ref, mod_ref, *rest):
    y = wt_ref[:, 0:1] * y0_ref[...] + wt_ref[:, 1:2] * y1_ref[...]
    h = h_ref[...] + mod_ref[5:6, :] * y
    if len(rest) == 2:
        fg_ref, o_ref = rest
        o_ref[...] = _rms(h, fg_ref[...])
    else:
        rest[0][...] = h


def _combine(h, y0, y1, wt, mod, dims, n_rows, final_g=None):
    n, d = h.shape
    b, s, ctx = dims
    tm = ROW_TILE
    row = lambda i: (i, 0)
    in_specs = [pl.BlockSpec((tm, d), row), pl.BlockSpec((tm, d), row), pl.BlockSpec((tm, d), row),
                pl.BlockSpec((tm, LANES), row),
                pl.BlockSpec((None, N_MOD, d), lambda i: (jnp.minimum(i * tm // s, b), 0, 0))]
    args = [h, y0, y1, wt, mod]
    if final_g is not None:
        in_specs.append(pl.BlockSpec((1, d), lambda i: (0, 0)))
        args.append(final_g)
    return pl.pallas_call(
        _combine_kernel,
        grid=(n_rows // tm,),
        in_specs=in_specs,
        out_specs=pl.BlockSpec((tm, d), row),
        out_shape=jax.ShapeDtypeStruct((n_rows if final_g is not None else n, d), F32),
        input_output_aliases={} if final_g is not None else {0: 0},
        compiler_params=_params("parallel"),
        name="moe_combine",
    )(*args)


def _moe(h, mod, ng, router, wg, wu, wd, layer, dims, n_rows, final_g=None, defer_combine=False):
    tm, rt = MOE_TILE, ROW_TILE
    v, idx, wt, lst, base = _route(h, mod, ng, router, dims, n_rows)
    base = base[:, 0, :N_EXPERTS]
    e_sel, pos = idx[:, 0:2], idx[:, 2:4]
    last = e_sel[-rt:]
    counts = base[-1] + jnp.sum(last[:, :, None] == jnp.arange(N_EXPERTS), axis=(0, 1))
    tiles_per = (counts + tm - 1) // tm
    tile_end = jnp.cumsum(tiles_per)
    group_start = (tile_end - tiles_per) * tm
    slot = group_start[e_sel] + pos
    n_tiles = 2 * n_rows // tm + N_EXPERTS
    tile_expert = jnp.minimum(jnp.sum(jnp.arange(n_tiles)[:, None] >= tile_end[None, :], axis=1),
                              N_EXPERTS - 1).astype(jnp.int32)
    n_used = tile_end[-1:].astype(jnp.int32)
    rank = (jnp.arange(n_tiles * tm, dtype=jnp.int32).reshape(n_tiles, tm)
            - group_start[tile_expert][:, None])
    base_te = base.T[tile_expert]
    owns = base_te[:, None, :] <= rank[:, :, None]
    rtile = jnp.sum(owns, axis=-1) - 1
    local = rank - jnp.max(jnp.where(owns, base_te[:, None, :], 0), axis=-1)
    flat = jnp.clip((rtile * rt + local) * N_EXPERTS + tile_expert[:, None], 0, n_rows * N_EXPERTS - 1)
    src = rtile * rt + jnp.take(lst[:, :N_EXPERTS].reshape(-1), flat, mode="clip")
    spread = jnp.arange(n_tiles * tm, dtype=jnp.int32).reshape(n_tiles, tm) % n_rows
    src = jnp.where(rank < counts[tile_expert][:, None], src, spread).reshape(-1)
    x_sorted = jnp.take(v, src, axis=0, mode="clip")
    out_sorted = _moe_experts(x_sorted, tile_expert, n_used, wg, wu, wd, layer)
    y0 = jnp.take(out_sorted, slot[:, 0], axis=0, mode="clip")
    y1 = jnp.take(out_sorted, slot[:, 1], axis=0, mode="clip")
    if defer_combine:
        return h, (y0, y1, wt, mod)
    return _combine(h, y0, y1, wt, mod, dims, n_rows, final_g), None


def _final_kernel(h_ref, g_ref, o_ref):
    o_ref[...] = _rms(h_ref[...], g_ref[...])


def _final_norm(h, g, n_rows):
    n, d = h.shape
    tm = ROW_TILE
    row = lambda i: (i, 0)
    return pl.pallas_call(
        _final_kernel,
        grid=(n_rows // tm,),
        in_specs=[pl.BlockSpec((tm, d), row), pl.BlockSpec((1, d), lambda i: (0, 0))],
        out_specs=pl.BlockSpec((tm, d), row),
        out_shape=jax.ShapeDtypeStruct((n_rows, d), F32),
        compiler_params=_params("parallel"),
        name="final_norm",
    )(h, g)


def _pad_cols(w, width):
    return jnp.pad(w, ((0, 0), (0, width - w.shape[-1])))


def _pack_in(w_in):
    cq = w_in[:, :Q_LORA]
    ckv = w_in[:, Q_LORA:KR_OFF]
    kr = w_in[:, KR_OFF:KR_OFF + QK_ROPE]
    rest = w_in[:, KR_OFF + QK_ROPE:]
    return jnp.concatenate([cq, ckv, _pad_cols(kr, LANES), rest], axis=-1).astype(BF16)


def _pack_q(w_uq):
    w = w_uq.reshape(Q_LORA, N_HEADS, QK_DIM)
    z = jnp.zeros((Q_LORA, N_HEADS, HEAD_PAD - QK_DIM), w.dtype)
    return jnp.concatenate([w, z], axis=-1).reshape(Q_LORA, QKV_W).astype(BF16)


def _pack_kv(w_ukv):
    w = w_ukv.reshape(KV_LORA, N_HEADS, QK_NOPE + V_HEAD)
    z = jnp.zeros((KV_LORA, N_HEADS, HEAD_PAD - QK_NOPE), w.dtype)
    k_all = jnp.concatenate([w[..., :QK_NOPE], z], axis=-1).reshape(KV_LORA, QKV_W)
    v_all = jnp.concatenate([w[..., QK_NOPE:], z], axis=-1).reshape(KV_LORA, QKV_W)
    return jnp.concatenate([k_all, v_all], axis=-1).astype(BF16)


def _rope_tabs(s, tm):
    rows = s // GRID_W
    row = jnp.repeat(jnp.arange(rows, dtype=F32), GRID_W)
    col = jnp.tile(jnp.arange(GRID_W, dtype=F32), rows)
    half = QK_ROPE // 2
    inv = 1.0 / (ROPE_BASE ** (jnp.arange(0, half, 2, dtype=F32) / half))
    ar = row[:, None] * inv[None, :]
    ac = col[:, None] * inv[None, :]
    ang = jnp.concatenate([ar, ar, ac, ac], axis=-1)
    cos = jnp.concatenate([jnp.cos(ang), jnp.ones((tm, QK_ROPE), F32)], axis=0)
    sin = jnp.concatenate([jnp.sin(ang), jnp.zeros((tm, QK_ROPE), F32)], axis=0)
    t = cos.shape[0]
    first = (jnp.arange(QK_ROPE) % (QK_ROPE // 2)) < QK_ROPE // 4
    sin1 = jnp.where(first, -sin, 0.0)
    sin2 = jnp.where(first, 0.0, sin)
    scale = QK_DIM ** -0.5 * LOG2E

    def q_tab(rope_part, nope_val):
        return jnp.concatenate([jnp.full((t, QK_NOPE), nope_val, F32), rope_part * scale,
                                jnp.zeros((t, HEAD_PAD - QK_DIM), F32)], axis=-1)

    return (q_tab(cos, scale), q_tab(sin1, 0.0), q_tab(sin2, 0.0),
            _pad_cols(cos, LANES), _pad_cols(sin1, LANES), _pad_cols(sin2, LANES))


def kernel(x, c, ctx, c_ctx, w_mod, b_mod, norm_g, final_g, ab_w_in, ab_g_cq, ab_g_ckv, ab_w_uq, ab_w_ukv, hgrn_lb_logits, hgrn_g_norm, ab_w_out, ffn_w_gate, ffn_w_up, ffn_w_down, pool_w, pool_b, pool_scale, moe_router, moe_w_gate, moe_w_up, moe_w_down):
    b, s, d = x.shape
    ctx_len = ctx.shape[1]
    depth = w_mod.shape[0]
    dims = (b, s, ctx_len)
    n_lat, n_ctx = b * s, b * ctx_len
    assert d == D_MODEL and s % ROW_TILE == 0 and ctx_len % ROW_TILE == 0
    assert ctx_len % GLA_BLOCK == 0 and s % GLA_BLOCK == 0 and n_lat % ctx_len == 0

    h = (x.reshape(n_lat, d), ctx.reshape(n_ctx, d))
    n_groups = ((b + 1 + SUBLANES - 1) // SUBLANES) * SUBLANES
    cvec = jnp.concatenate([c, c_ctx[None, :], jnp.zeros((n_groups - b - 1, d), F32)], axis=0)
    mod_all = _mod_table(cvec, w_mod, b_mod).reshape(depth, n_groups, N_MOD, d)

    lb_p = jax.nn.softmax(hgrn_lb_logits.astype(F32), axis=0)
    lb_all = jnp.cumsum(lb_p, axis=0) - lb_p[:1]
    tabs = _rope_tabs(s, ROW_TILE)

    pending = None
    for l in range(depth):
        j = l // 2
        even = l % 2 == 0
        ctx_later = any(m % 2 == 0 for m in range(l + 1, depth))
        n_rows = n_lat + n_ctx if ctx_later else n_lat
        mod = mod_all[l]
        ng1, ng2 = norm_g[l, 0][None, :], norm_g[l, 1][None, :]
        if even:
            outs = _inproj(
                h, mod, ng1, _pack_in(ab_w_in[j]), ab_g_cq[j][None, :], ab_g_ckv[j][None, :],
                _pack_q(ab_w_uq[j]), _pack_kv(ab_w_ukv[j]), lb_all[j], tabs, dims, pending)
            (q, k, v, qh, kf, lgf, kb, lgb, vh, hg) = outs[:10]
            if pending is not None:
                h, pending = outs[10], None
            a = _attention_latent(q, k, v, dims)
            if ctx_later:
                a = jnp.concatenate([a, _attention_ctx(q, k, v, dims)], axis=0)
            o_f, o_b = _gla(qh, kf, lgf, kb, lgb, vh, dims)
            h = _mix_ffn(a, o_f, o_b, hg, hgrn_g_norm[j][None, :], ab_w_out[j].astype(BF16), h, mod, ng2,
                         ffn_w_gate[j].astype(BF16), ffn_w_up[j].astype(BF16),
                         ffn_w_down[j].astype(BF16), dims, n_rows)
        else:
            h = _pool(h, mod, ng1, pool_w[j].astype(BF16), pool_b[j], pool_scale[j][None, :], dims, n_rows)
            r_full = _pad_cols(moe_router[j], LANES)
            r_hi = r_full.astype(BF16)
            router = jnp.stack([r_hi, (r_full - r_hi.astype(F32)).astype(BF16)])
            last = l == depth - 1
            defer = not last and n_rows == n_lat + n_ctx
            h, pending = _moe(h, mod, ng2, router, moe_w_gate, moe_w_up, moe_w_down, j, dims, n_rows,
                              final_g[None, :] if last else None, defer)
            if last:
                return h.reshape(b, s, d)
    return _final_norm(h, final_g[None, :], n_lat).reshape(b, s, d)
```

```python
import functools

import jax
import jax.numpy as jnp
from jax import lax
from jax.experimental import pallas as pl
from jax.experimental.pallas import tpu as pltpu

F32 = jnp.float32
BF16 = jnp.bfloat16

D_MODEL = 1024
GRID_W = 64
N_MOD = 6
EPS = 1e-6
N_HEADS = 8
V_HEAD = 64
QK_NOPE = 64
QK_ROPE = 32
QK_DIM = QK_NOPE + QK_ROPE
Q_LORA = 384
KV_LORA = 256
ROPE_BASE = 10000.0
MLA_W = 512
HGRN_W = 512
HGRN_K = 128
N_HGRN_HEADS = 4
POOL_WINDOWS = (2, 4, 8, 16)
POOL_C = 256
N_EXPERTS = 8

LANES = 128
SUBLANES = 8
VMEM_LIMIT_BYTES = 56 * 1024 * 1024

HEAD_PAD = LANES
QKV_W = N_HEADS * HEAD_PAD
KR_OFF = Q_LORA + KV_LORA
H_OFF = KR_OFF + LANES
IN_COLS = H_OFF + 5 * HGRN_W
GLA_CHUNK = 64
GLA_BLOCK = 256
GLA_PHASE_GROUP = 2
ROW_TILE = 256
ROUTE_SUBTILES = 4
MLP_TILE = 512
ATTN_Q_TILE = 512
ATTN_HEAD_GROUP = 4
MOE_TILE = 1024
MOE_FF_TILE = 512
MOE_FF_SPLIT = 2
LOG2E = 1.4426950408889634


def _params(*sem):
    return pltpu.CompilerParams(dimension_semantics=sem, vmem_limit_bytes=VMEM_LIMIT_BYTES)


def _dot(a, b):
    return jnp.dot(a, b, preferred_element_type=F32)


def _dot_nt(a, b):
    return lax.dot_general(a, b, (((1,), (1,)), ((), ())), preferred_element_type=F32)


def _dot_tn(a, b):
    return lax.dot_general(a, b, (((0,), (0,)), ((), ())), preferred_element_type=F32)


def _rms(x, g):
    return x * lax.rsqrt(jnp.mean(x * x, axis=-1, keepdims=True) + EPS) * g


def _silu(x):
    return x * jax.nn.sigmoid(x)


def _norm_mod(h, g, shift, scale):
    return _rms(h, g) * (1.0 + scale) + shift


def _mod_kernel(c_ref, w_ref, b_ref, o_ref):
    c = c_ref[...]
    o_ref[...] = jnp.dot(_silu(c), w_ref[...], preferred_element_type=F32,
                         precision=lax.Precision.HIGHEST) + b_ref[...]


def _mod_table(cvec, w_mod, b_mod):
    depth, d, nd = w_mod.shape
    g = cvec.shape[0]
    tn = 1536
    return pl.pallas_call(
        _mod_kernel,
        grid=(depth, nd // tn),
        in_specs=[pl.BlockSpec((g, d), lambda l, j: (0, 0)),
                  pl.BlockSpec((None, d, tn), lambda l, j: (l, 0, j)),
                  pl.BlockSpec((None, 1, tn), lambda l, j: (l, 0, j))],
        out_specs=pl.BlockSpec((None, g, tn), lambda l, j: (l, 0, j)),
        out_shape=jax.ShapeDtypeStruct((depth, g, nd), F32),
        compiler_params=_params("parallel", "parallel"),
        name="mod_table",
    )(cvec, w_mod, b_mod.reshape(depth, 1, nd))


def _inproj_body(h, mod_ref, ng_ref, win_ref, gcq_ref, gckv_ref, wq_ref, wkv_ref,
                   lb_ref, cq_tab, s1q_tab, s2q_tab, ck_tab, s1k_tab, s2k_tab,
                   q_ref, k_ref, v_ref, qh_ref, kf_ref, lgf_ref, kb_ref, lgb_ref, vh_ref, hg_ref):
    half_rot = QK_ROPE // 4

    def rope(x, c_tab, s1_tab, s2_tab):
        return (x * c_tab[...] + pltpu.roll(x, LANES - half_rot, 1) * s1_tab[...]
                + pltpu.roll(x, half_rot, 1) * s2_tab[...])

    u = _norm_mod(h, ng_ref[...], mod_ref[0:1, :], mod_ref[1:2, :]).astype(BF16)
    p = _dot(u, win_ref[...])
    cqn = _rms(p[:, 0:Q_LORA], gcq_ref[...]).astype(BF16)
    a = _dot(cqn, wq_ref[...])
    q_ref[...] = jnp.concatenate(
        [rope(a[:, hd * HEAD_PAD:(hd + 1) * HEAD_PAD], cq_tab, s1q_tab, s2q_tab) for hd in range(N_HEADS)],
        axis=-1).astype(BF16)
    ckvn = _rms(p[:, Q_LORA:KR_OFF], gckv_ref[...]).astype(BF16)
    kv = _dot(ckvn, wkv_ref[...])
    kr = pltpu.roll(rope(p[:, KR_OFF:H_OFF], ck_tab, s1k_tab, s2k_tab), QK_NOPE, 1)
    k_ref[...] = jnp.concatenate(
        [kv[:, hd * HEAD_PAD:(hd + 1) * HEAD_PAD] + kr for hd in range(N_HEADS)], axis=-1).astype(BF16)
    lane = lax.broadcasted_iota(jnp.int32, (1, QKV_W), 1)
    ones_col = (lane % HEAD_PAD == V_HEAD).astype(F32)
    v_ref[...] = (kv[:, QKV_W:] + ones_col).astype(BF16)
    o = H_OFF
    qh_ref[...] = _silu(p[:, o:o + HGRN_W])
    kf = (1.0 - lb_ref[0:1, :]) * jax.nn.sigmoid(-p[:, o + HGRN_W:o + 2 * HGRN_W])
    kf_ref[...] = kf
    lgf_ref[...] = jnp.log1p(-kf)
    kb = (1.0 - lb_ref[1:2, :]) * jax.nn.sigmoid(-p[:, o + 2 * HGRN_W:o + 3 * HGRN_W])
    kb_ref[...] = kb
    lgb_ref[...] = jnp.log1p(-kb)
    vh_ref[...] = p[:, o + 3 * HGRN_W:o + 4 * HGRN_W]
    hg_ref[...] = p[:, o + 4 * HGRN_W:o + 5 * HGRN_W]


def _rows_from(refs, lat_tiles):
    if lat_tiles is None:
        return refs[0][...], refs[1:]
    return jnp.where(pl.program_id(0) < lat_tiles, refs[0][...], refs[1][...]), refs[2:]


def _inproj_kernel(*refs, lat_tiles, pending):
    h, rest = _rows_from(refs, lat_tiles)
    if pending:
        y0_ref, y1_ref, wt_ref, pmod_ref, *rest = rest
        *rest, hnew_ref = rest
        h = h + pmod_ref[5:6, :] * (wt_ref[:, 0:1] * y0_ref[...] + wt_ref[:, 1:2] * y1_ref[...])
        hnew_ref[...] = h
    _inproj_body(h, *rest)


def _split_rows(h, tm):
    if not isinstance(h, tuple):
        return [pl.BlockSpec((tm, h.shape[1]), lambda i: (i, 0))], [h], None
    x2, c2 = h
    lat_tiles = x2.shape[0] // tm
    specs = [pl.BlockSpec((tm, x2.shape[1]), lambda i: (jnp.minimum(i, lat_tiles - 1), 0)),
             pl.BlockSpec((tm, c2.shape[1]), lambda i: (jnp.maximum(i - lat_tiles, 0), 0))]
    return specs, [x2, c2], lat_tiles


def _inproj(h, mod, ng, win, gcq, gckv, wq, wkv, lb, tabs, dims, pending=None):
    b, s, ctx = dims
    n, d = b * (s + ctx), D_MODEL
    tm = ROW_TILE
    h_specs, h_args, lat_tiles = _split_rows(h, tm)
    extra_specs, extra_args, extra_out_specs, extra_out_shape, aliases = [], [], [], [], {}
    if pending is not None:
        extra_specs = [pl.BlockSpec((tm, d), lambda i: (i, 0)), pl.BlockSpec((tm, d), lambda i: (i, 0)),
                       pl.BlockSpec((tm, LANES), lambda i: (i, 0)),
                       pl.BlockSpec((None, N_MOD, d), lambda i: (jnp.minimum(i * tm // s, b), 0, 0))]
        extra_args = list(pending)
        extra_out_specs = [pl.BlockSpec((tm, d), lambda i: (i, 0))]
        extra_out_shape = [jax.ShapeDtypeStruct((n, d), F32)]
        aliases = {0: 10}
    n_lat_t = b * s // tm
    pos_blocks = s // tm

    def grp(i):
        return jnp.minimum(i * tm // s, b)

    def pos(i):
        return jnp.where(i < n_lat_t, i % pos_blocks, pos_blocks)

    row = lambda i: (i, 0)
    const = lambda i: (0, 0)
    tab_spec = pl.BlockSpec((tm, LANES), lambda i: (pos(i), 0))
    wide = jax.ShapeDtypeStruct((n, QKV_W), BF16)
    hg = jax.ShapeDtypeStruct((n, HGRN_W), F32)
    return pl.pallas_call(
        functools.partial(_inproj_kernel, lat_tiles=lat_tiles, pending=pending is not None),
        grid=(n // tm,),
        in_specs=h_specs + extra_specs + [
                  pl.BlockSpec((None, N_MOD, d), lambda i: (grp(i), 0, 0)),
                  pl.BlockSpec((1, d), const),
                  pl.BlockSpec(win.shape, const),
                  pl.BlockSpec((1, Q_LORA), const),
                  pl.BlockSpec((1, KV_LORA), const),
                  pl.BlockSpec(wq.shape, const),
                  pl.BlockSpec(wkv.shape, const),
                  pl.BlockSpec((2, HGRN_W), const),
                  tab_spec, tab_spec, tab_spec, tab_spec, tab_spec, tab_spec],
        out_specs=([pl.BlockSpec((tm, QKV_W), row)] * 3 + [pl.BlockSpec((tm, HGRN_W), row)] * 7
                   + extra_out_specs),
        out_shape=[wide] * 3 + [hg] * 7 + extra_out_shape,
        input_output_aliases=aliases,
        compiler_params=_params("parallel"),
        name="inproj",
    )(*h_args, *extra_args, mod, ng, win, gcq, gckv, wq, wkv, lb, *tabs)


def _attn_kernel(*refs, n_kv):
    q_ref = refs[0]
    k_refs = refs[1:1 + 2 * n_kv:2]
    v_refs = refs[2:2 + 2 * n_kv:2]
    o_ref = refs[-1]
    for h0 in range(0, N_HEADS, ATTN_HEAD_GROUP):
        heads = range(h0, h0 + ATTN_HEAD_GROUP)
        sls = {hd: slice(hd * HEAD_PAD, (hd + 1) * HEAD_PAD) for hd in heads}
        scores = {hd: [_dot_nt(q_ref[:, sls[hd]], k_ref[:, sls[hd]]) for k_ref in k_refs] for hd in heads}
        ms = {hd: functools.reduce(jnp.maximum, [jnp.max(sc, axis=-1, keepdims=True) for sc in scores[hd]])
              for hd in heads}
        for hd in heads:
            acc = None
            for sc, v_ref in zip(scores[hd], v_refs):
                part = _dot(jnp.exp2(sc - ms[hd]).astype(BF16), v_ref[:, sls[hd]])
                acc = part if acc is None else acc + part
            o = acc[:, :V_HEAD] / acc[:, V_HEAD:V_HEAD + 1]
            o_ref[:, hd * V_HEAD:(hd + 1) * V_HEAD] = o.astype(o_ref.dtype)


def _attention_latent(q, k, v, dims):
    b, s, ctx = dims
    tq = min(ATTN_Q_TILE, s)
    nq = s // tq
    lat = lambda bi, j: (bi, 0)
    cx = lambda bi, j: (b * s // ctx + bi, 0)
    return pl.pallas_call(
        functools.partial(_attn_kernel, n_kv=2),
        grid=(b, nq),
        in_specs=[pl.BlockSpec((tq, QKV_W), lambda bi, j: (bi * nq + j, 0)),
                  pl.BlockSpec((s, QKV_W), lat), pl.BlockSpec((s, QKV_W), lat),
                  pl.BlockSpec((ctx, QKV_W), cx), pl.BlockSpec((ctx, QKV_W), cx)],
        out_specs=pl.BlockSpec((tq, MLA_W), lambda bi, j: (bi * nq + j, 0)),
        out_shape=jax.ShapeDtypeStruct((b * s, MLA_W), BF16),
        compiler_params=_params("parallel", "arbitrary"),
        name="attn_latent",
    )(q, k, v, k, v)


def _attention_ctx(q, k, v, dims):
    b, s, ctx = dims
    cx = lambda bi: (b * s // ctx + bi, 0)
    return pl.pallas_call(
        functools.partial(_attn_kernel, n_kv=1),
        grid=(b,),
        in_specs=[pl.BlockSpec((ctx, QKV_W), cx), pl.BlockSpec((ctx, QKV_W), cx),
                  pl.BlockSpec((ctx, QKV_W), cx)],
        out_specs=pl.BlockSpec((ctx, MLA_W), lambda bi: (bi, 0)),
        out_shape=jax.ShapeDtypeStruct((b * ctx, MLA_W), BF16),
        compiler_params=_params("parallel"),
        name="attn_ctx",
    )(q, k, v)


def _bcast_block_row(g, bs, row):
    if bs == GLA_CHUNK:
        return g[row:row + 1, :]
    g3 = g.reshape(GLA_CHUNK // bs, bs, g.shape[-1])
    return jnp.broadcast_to(g3[:, row:row + 1, :], g3.shape).reshape(g.shape)


def _gla_consts(reverse):
    c = GLA_CHUNK
    t_idx = lax.broadcasted_iota(jnp.int32, (c, c), 0)
    s_idx = lax.broadcasted_iota(jnp.int32, (c, c), 1)
    pt = (c - 1 - t_idx) if reverse else t_idx
    ps = (c - 1 - s_idx) if reverse else s_idx
    tri = (ps <= pt).astype(BF16)
    levels = []
    for half in (32, 16, 8):
        bs = 2 * half
        mask = (t_idx // bs == s_idx // bs) & (pt % bs >= half) & (ps % bs < half)
        levels.append((bs, half if reverse else half - 1, False, mask))
    dmask = (t_idx // SUBLANES == s_idx // SUBLANES) & (ps <= pt)
    levels.append((SUBLANES, 4 if reverse else 3, True, dmask))
    end_row = 0 if reverse else c - 1
    return tri, levels, end_row


def _gla_kernel(qf_ref, kf_ref, lgf_ref, vf_ref, qb_ref, kb_ref, lgb_ref, vb_ref,
                of_ref, ob_ref, stf_ref, stb_ref):
    @pl.when(pl.program_id(1) == 0)
    def _():
        stf_ref[...] = jnp.zeros_like(stf_ref)
        stb_ref[...] = jnp.zeros_like(stb_ref)

    c = GLA_CHUNK
    n_chunks = GLA_BLOCK // c
    dirs = [(qf_ref, kf_ref, lgf_ref, vf_ref, of_ref, stf_ref, False),
            (qb_ref, kb_ref, lgb_ref, vb_ref, ob_ref, stb_ref, True)]
    consts = [_gla_consts(False), _gla_consts(True)]
    states = [[d[5][hd] for hd in range(N_HGRN_HEADS)] for d in dirs]
    for step0 in range(0, n_chunks, GLA_PHASE_GROUP):
      units = []
      for step in range(step0, step0 + GLA_PHASE_GROUP):
        for di, (q_ref, k_ref, lg_ref, v_ref, o_ref, _, reverse) in enumerate(dirs):
            tri, levels, end_row = consts[di]
            ci = n_chunks - 1 - step if reverse else step
            rows = slice(ci * c, (ci + 1) * c)
            lg = lg_ref[rows, :]
            hi = lg.astype(BF16)
            r1 = lg - hi.astype(F32)
            mid = r1.astype(BF16)
            lo = (r1 - mid.astype(F32)).astype(BF16)
            gc = (_dot(tri, hi) + _dot(tri, mid) + _dot(tri, lo)) * LOG2E
            for hd in range(N_HGRN_HEADS):
                hs = slice(hd * HGRN_K, (hd + 1) * HGRN_K)
                units.append(dict(di=di, hd=hd, rows=rows, hs=hs, g=gc[:, hs], o_ref=o_ref,
                                  qh=q_ref[rows, hs], kh=k_ref[rows, hs],
                                  vh=v_ref[rows, hs].astype(BF16), levels=levels, end_row=end_row,
                                  attn=jnp.zeros((c, c), F32)))
      if True:
        for li in range(4):
            for u in units:
                bs, row, diag, mask = u["levels"][li]
                g, qh, kh = u["g"], u["qh"], u["kh"]
                d = g - _bcast_block_row(g, bs, row)
                if diag:
                    qf = (qh * jnp.exp2(d)).astype(BF16)
                    kf = (kh * jnp.exp2(-d)).astype(BF16)
                else:
                    e = jnp.exp2(-jnp.abs(d))
                    qf = (qh * e).astype(BF16)
                    kf = (kh * e).astype(BF16)
                u["attn"] = jnp.where(mask, _dot_nt(qf, kf), u["attn"])
        for u in units:
            g, qh, kh, vh = u["g"], u["qh"], u["kh"], u["vh"]
            st = states[u["di"]][u["hd"]]
            g_end = g[u["end_row"]:u["end_row"] + 1, :]
            qe = (qh * jnp.exp2(g)).astype(BF16)
            ke = (kh * jnp.exp2(g_end - g)).astype(BF16)
            o = _dot_nt(qe, st.astype(BF16)) + _dot(u["attn"].astype(BF16), vh)
            u["o_ref"][u["rows"], u["hs"]] = o
            states[u["di"]][u["hd"]] = st * jnp.exp2(g_end) + _dot_tn(vh, ke)
    for di, d in enumerate(dirs):
        for hd in range(N_HGRN_HEADS):
            d[5][hd] = states[di][hd]


def _gla(qh, kf, lgf, kb, lgb, vh, dims):
    n = qh.shape[0]
    b, s, ctx = dims
    blk = GLA_BLOCK
    ncb, nsb, nlat = ctx // blk, s // blk, b * s // blk

    def fwd(bi, j):
        return (jnp.where(j < ncb, nlat + bi * ncb + j, bi * nsb + (j - ncb)), 0)

    def bwd(bi, j):
        return (jnp.where(j < ncb, nlat + bi * ncb + (ncb - 1 - j), bi * nsb + (nsb - 1 - (j - ncb))), 0)

    fs = pl.BlockSpec((blk, HGRN_W), fwd)
    bs = pl.BlockSpec((blk, HGRN_W), bwd)
    out = jax.ShapeDtypeStruct((n, HGRN_W), F32)
    return pl.pallas_call(
        _gla_kernel,
        grid=(b, ncb + nsb),
        in_specs=[fs, fs, fs, fs, bs, bs, bs, bs],
        out_specs=[fs, bs],
        out_shape=[out, out],
        scratch_shapes=[pltpu.VMEM((N_HGRN_HEADS, HGRN_K, HGRN_K), F32)] * 2,
        compiler_params=_params("parallel", "arbitrary"),
        name="gla",
    )(qh, kf, lgf, vh, qh, kb, lgb, vh)


def _mix_ffn_kernel(*refs, lat_tiles):
    h, (a_ref, of_ref, ob_ref, hg_ref, gn_ref, wo_ref, mod_ref, ng_ref, wg_ref, wu_ref, wd_ref,
        o_ref) = _rows_from(refs, lat_tiles)
    rows = h.shape[0] // 2
    halves = [slice(r * rows, (r + 1) * rows) for r in range(2)]
    mixed = []
    for rs in halves:
        o = of_ref[rs, :] + ob_ref[rs, :]
        gate = _silu(hg_ref[rs, :])
        ys = []
        for hd in range(N_HGRN_HEADS):
            hs = slice(hd * HGRN_K, (hd + 1) * HGRN_K)
            ys.append(_rms(o[:, hs], gn_ref[...]) * gate[:, hs])
        y = jnp.concatenate(ys, axis=-1).astype(BF16)
        mix = _dot(a_ref[rs, :], wo_ref[0:MLA_W, :]) + _dot(y, wo_ref[MLA_W:, :])
        mixed.append(h[rs, :] + mod_ref[2:3, :] * mix)
    for rs, h1 in zip(halves, mixed):
        v = _norm_mod(h1, ng_ref[...], mod_ref[3:4, :], mod_ref[4:5, :]).astype(BF16)
        hid = (_silu(_dot(v, wg_ref[...])) * _dot(v, wu_ref[...])).astype(BF16)
        o_ref[rs, :] = h1 + mod_ref[5:6, :] * _dot(hid, wd_ref[...])


def _mix_ffn(a, of, ob, hg, gn, wo, h, mod, ng, wg, wu, wd, dims, n_rows):
    b, s, ctx = dims
    n, d = b * (s + ctx), D_MODEL
    tm = MLP_TILE
    row = lambda i: (i, 0)
    const = lambda i: (0, 0)
    half = pl.BlockSpec((tm, HGRN_W), row)
    resident = lambda w: pl.BlockSpec(w.shape, const, pipeline_mode=pl.Buffered(1))
    h_specs, h_args, lat_tiles = _split_rows(h, tm)
    return pl.pallas_call(
        functools.partial(_mix_ffn_kernel, lat_tiles=lat_tiles),
        grid=(n_rows // tm,),
        in_specs=h_specs + [pl.BlockSpec((tm, MLA_W), row), half, half, half,
                            pl.BlockSpec((1, HGRN_K), const), resident(wo),
                            pl.BlockSpec((None, N_MOD, d), lambda i: (jnp.minimum(i * tm // s, b), 0, 0)),
                            pl.BlockSpec((1, d), const), resident(wg), resident(wu), resident(wd)],
        out_specs=pl.BlockSpec((tm, d), row),
        out_shape=jax.ShapeDtypeStruct((n, d), F32),
        input_output_aliases={} if lat_tiles is not None else {0: 0},
        compiler_params=_params("parallel"),
        name="mix_ffn",
    )(*h_args, a, of, ob, hg, gn, wo, mod, ng, wg, wu, wd)


def _pool_kernel(h_ref, prev_ref, next_ref, mod_ref, ng_ref, w_ref, b_ref, sc_ref, o_ref,
                 *, tm, seq_tiles_lat, n_lat_tiles, seq_tiles_ctx):
    i = pl.program_id(0)
    halo = SUBLANES
    shift, scale = mod_ref[0:1, :], mod_ref[1:2, :]
    in_lat = i < n_lat_tiles
    j = jnp.where(in_lat, i % seq_tiles_lat, (i - n_lat_tiles) % seq_tiles_ctx)
    n_seq_tiles = jnp.where(in_lat, seq_tiles_lat, seq_tiles_ctx)
    h = h_ref[...]
    u = _norm_mod(h, ng_ref[...], shift, scale)
    up = _norm_mod(prev_ref[...], ng_ref[...], shift, scale)
    un = _norm_mod(next_ref[...], ng_ref[...], shift, scale)
    ext = jnp.concatenate([jnp.where(j > 0, up, 0.0), u, jnp.where(j < n_seq_tiles - 1, un, 0.0)], axis=0)
    ext_hi = ext.astype(BF16)
    ext_lo = (ext - ext_hi.astype(F32)).astype(BF16)
    off = (lax.broadcasted_iota(jnp.int32, (tm, tm + 2 * halo), 1)
           - lax.broadcasted_iota(jnp.int32, (tm, tm + 2 * halo), 0) - halo)
    pos = j * tm + lax.broadcasted_iota(jnp.int32, (tm, 1), 0)
    t_len = n_seq_tiles * tm
    css = [slice(g * POOL_C, (g + 1) * POOL_C) for g in range(len(POOL_WINDOWS))]
    bands = [((off >= -(win // 2)) & (off < win // 2)).astype(BF16) for win in POOL_WINDOWS]
    accs = [_dot(bd, ext_hi[:, cs]) + _dot(bd, ext_lo[:, cs]) for bd, cs in zip(bands, css)]
    pooled = []
    for win, cs, acc in zip(POOL_WINDOWS, css, accs):
        lo = jnp.maximum(pos - win // 2, 0)
        hi = jnp.minimum(pos - win // 2 + win, t_len)
        cnt = (hi - lo).astype(F32)
        pooled.append((acc / cnt - u[:, cs]).astype(BF16))
    outs = [_dot(pl_g, w_ref[g]) + b_ref[g:g + 1, :] for g, pl_g in enumerate(pooled)]
    y = jnp.concatenate(outs, axis=-1) * sc_ref[...]
    o_ref[...] = h + mod_ref[2:3, :] * y


def _pool(h, mod, ng, w, bias, scale, dims, n_rows):
    n, d = h.shape
    b, s, ctx = dims
    tm = ROW_TILE
    hb = tm // SUBLANES
    n_blk8 = n // SUBLANES
    row = lambda i: (i, 0)
    const = lambda i: (0, 0)
    kern = functools.partial(_pool_kernel, tm=tm, seq_tiles_lat=s // tm, n_lat_tiles=b * s // tm,
                             seq_tiles_ctx=ctx // tm)
    return pl.pallas_call(
        kern,
        grid=(n_rows // tm,),
        in_specs=[pl.BlockSpec((tm, d), row),
                  pl.BlockSpec((SUBLANES, d), lambda i: (jnp.maximum(i * hb - 1, 0), 0)),
                  pl.BlockSpec((SUBLANES, d), lambda i: (jnp.minimum((i + 1) * hb, n_blk8 - 1), 0)),
                  pl.BlockSpec((None, N_MOD, d), lambda i: (jnp.minimum(i * tm // s, b), 0, 0)),
                  pl.BlockSpec((1, d), const),
                  pl.BlockSpec(w.shape, lambda i: (0, 0, 0)),
                  pl.BlockSpec(bias.shape, const),
                  pl.BlockSpec((1, d), const)],
        out_specs=pl.BlockSpec((tm, d), row),
        out_shape=jax.ShapeDtypeStruct((n_rows, d), F32),
        compiler_params=_params("parallel"),
        name="pool",
    )(h, h, h, mod, ng, w, bias, scale)


def _route_kernel(h_ref, mod_ref, ng_ref, r_ref, v_ref, idx_ref, wt_ref, lst_ref, base_ref, run_ref):
    rt = ROW_TILE
    subs = [slice(k * rt, (k + 1) * rt) for k in range(h_ref.shape[0] // rt)]

    @pl.when(pl.program_id(0) == 0)
    def _():
        run_ref[...] = jnp.zeros_like(run_ref)

    lane = lax.broadcasted_iota(jnp.int32, (rt, LANES), 1)
    ti = lax.broadcasted_iota(jnp.int32, (rt, rt), 0)
    tj = lax.broadcasted_iota(jnp.int32, (rt, rt), 1)
    neg = jnp.float32(-jnp.inf)
    vs = [_norm_mod(h_ref[sl, :], ng_ref[...], mod_ref[3:4, :], mod_ref[4:5, :]) for sl in subs]
    v_his = [v.astype(BF16) for v in vs]
    half = vs[0].shape[-1] // 2
    for sl, v_hi in zip(subs, v_his):
        vb = pltpu.bitcast(v_hi.astype(F32), jnp.uint32)
        v_ref[sl, :] = (vb[:, :half] >> 16) | (vb[:, half:] & jnp.uint32(0xFFFF0000))
    logits = []
    for v, v_hi in zip(vs, v_his):
        v_lo = (v - v_hi.astype(F32)).astype(BF16)
        lg = _dot(v_hi, r_ref[0]) + (_dot(v_lo, r_ref[0]) + _dot(v_hi, r_ref[1]))
        logits.append(jnp.where(lane < N_EXPERTS, lg, neg))
    m1s = [jnp.max(lg, axis=-1, keepdims=True) for lg in logits]
    i1s = [jnp.min(jnp.where(lg == m1, lane, LANES), axis=-1, keepdims=True) for lg, m1 in zip(logits, m1s)]
    rests = [jnp.where(lane == i1, neg, lg) for lg, i1 in zip(logits, i1s)]
    m2s = [jnp.max(r, axis=-1, keepdims=True) for r in rests]
    i2s = [jnp.min(jnp.where(r == m2, lane, LANES), axis=-1, keepdims=True) for r, m2 in zip(rests, m2s)]
    for sl, m1, m2 in zip(subs, m1s, m2s):
        e2 = jnp.exp(m2 - m1)
        wt_ref[sl, :] = jnp.where(lane == 0, 1.0 / (1.0 + e2), jnp.where(lane == 1, e2 / (1.0 + e2), 0.0))
    ohs = [((lane == i1) | (lane == i2)).astype(BF16) for i1, i2 in zip(i1s, i2s)]
    befores = [_dot((tj < ti).astype(BF16), oh) for oh in ohs]
    befores_t = [_dot_tn(oh, (ti < tj).astype(BF16)) for oh in ohs]
    choses_t = [_dot_tn(oh, (ti == tj).astype(BF16)) for oh in ohs]
    run = run_ref[...]
    for k, sl in enumerate(subs):
        pos = run + befores[k]
        r1 = jnp.sum(jnp.where(lane == i1s[k], pos, 0.0), axis=-1, keepdims=True)
        r2 = jnp.sum(jnp.where(lane == i2s[k], pos, 0.0), axis=-1, keepdims=True)
        idx_ref[sl, :] = jnp.where(lane == 0, i1s[k], jnp.where(lane == 1, i2s[k], jnp.where(
            lane == 2, r1.astype(jnp.int32), jnp.where(lane == 3, r2.astype(jnp.int32), 0))))
        base_ref[k] = run.astype(jnp.int32)
        run = run + jnp.sum(ohs[k].astype(F32), axis=0, keepdims=True)
    run_ref[...] = run
    ids = lax.broadcasted_iota(jnp.int32, (rt, LANES), 0).astype(BF16)
    lsts = [jnp.zeros((rt, LANES), F32) for _ in subs]
    for e in range(N_EXPERTS):
        for k in range(len(subs)):
            sel = (befores_t[k][e:e + 1, :] == ti.astype(F32)) & (choses_t[k][e:e + 1, :] > 0.5)
            lsts[k] = jnp.where(lane == e, _dot(sel.astype(BF16), ids), lsts[k])
    for sl, lst in zip(subs, lsts):
        lst_ref[sl, :] = lst.astype(jnp.int32)


def _route(h, mod, ng, router, dims, n_rows):
    n, d = h.shape
    b, s, ctx = dims
    assert ROW_TILE <= 256
    nsub = ROUTE_SUBTILES
    while s % (nsub * ROW_TILE) or (n_rows - b * s) % (nsub * ROW_TILE):
        nsub //= 2
    tm = nsub * ROW_TILE
    nt = n_rows // ROW_TILE
    row = lambda i: (i, 0)
    const = lambda i: (0, 0)
    return pl.pallas_call(
        _route_kernel,
        grid=(n_rows // tm,),
        in_specs=[pl.BlockSpec((tm, d), row),
                  pl.BlockSpec((None, N_MOD, d), lambda i: (jnp.minimum(i * tm // s, b), 0, 0)),
                  pl.BlockSpec((1, d), const),
                  pl.BlockSpec((2, d, LANES), lambda i: (0, 0, 0))],
        out_specs=[pl.BlockSpec((tm, d // 2), row), pl.BlockSpec((tm, LANES), row),
                   pl.BlockSpec((tm, LANES), row), pl.BlockSpec((tm, LANES), row),
                   pl.BlockSpec((nsub, 1, LANES), lambda i: (i, 0, 0))],
        out_shape=[jax.ShapeDtypeStruct((n_rows, d // 2), jnp.uint32),
                   jax.ShapeDtypeStruct((n_rows, LANES), jnp.int32),
                   jax.ShapeDtypeStruct((n_rows, LANES), F32),
                   jax.ShapeDtypeStruct((n_rows, LANES), jnp.int32),
                   jax.ShapeDtypeStruct((nt, 1, LANES), jnp.int32)],
        scratch_shapes=[pltpu.VMEM((1, LANES), F32)],
        compiler_params=_params("arbitrary"),
        name="route",
    )(h, mod, ng, router)


def _moe_kernel(te_ref, nu_ref, x_ref, wg_ref, wu_ref, wd_ref, o_ref, xs_ref, acc_ref):
    i = pl.program_id(0)
    f = pl.program_id(1)

    @pl.when((i < nu_ref[0]) & (f == 0))
    def _():
        half = x_ref.shape[-1]
        xw = x_ref[...]
        xs_ref[:, :half] = pltpu.bitcast(xw << 16, F32).astype(BF16)
        xs_ref[:, half:] = pltpu.bitcast(xw & jnp.uint32(0xFFFF0000), F32).astype(BF16)
        acc_ref[...] = jnp.zeros_like(acc_ref)

    @pl.when(i < nu_ref[0])
    def _():
        x = xs_ref[...]
        w = wg_ref.shape[-1] // MOE_FF_SPLIT
        hids = []
        for c in range(MOE_FF_SPLIT):
            cs = slice(c * w, (c + 1) * w)
            hids.append((_silu(_dot(x, wg_ref[:, cs].astype(BF16))) *
                         _dot(x, wu_ref[:, cs].astype(BF16))).astype(BF16))
        part = None
        for c in range(MOE_FF_SPLIT):
            p = _dot(hids[c], wd_ref[c * w:(c + 1) * w, :].astype(BF16))
            part = p if part is None else part + p
        acc = acc_ref[...] + part
        acc_ref[...] = acc
        o_ref[...] = acc.astype(o_ref.dtype)


def _moe_experts(x_sorted, tile_expert, n_used, wg, wu, wd, layer):
    npad, dw = x_sorted.shape
    d = 2 * dw
    tm, tf = MOE_TILE, MOE_FF_TILE
    dff = wg.shape[-1]

    def ff(i, f, nu):
        return jnp.where(i < nu[0], f, 0)

    grid_spec = pltpu.PrefetchScalarGridSpec(
        num_scalar_prefetch=2,
        grid=(npad // tm, dff // tf),
        in_specs=[pl.BlockSpec((tm, dw), lambda i, f, te, nu: (i, 0)),
                  pl.BlockSpec((None, None, d, tf), lambda i, f, te, nu: (layer, te[i], 0, ff(i, f, nu))),
                  pl.BlockSpec((None, None, d, tf), lambda i, f, te, nu: (layer, te[i], 0, ff(i, f, nu))),
                  pl.BlockSpec((None, None, tf, d), lambda i, f, te, nu: (layer, te[i], ff(i, f, nu), 0))],
        out_specs=pl.BlockSpec((tm, d), lambda i, f, te, nu: (i, 0)),
        scratch_shapes=[pltpu.VMEM((tm, d), BF16), pltpu.VMEM((tm, d), F32)],
    )
    return pl.pallas_call(
        _moe_kernel,
        grid_spec=grid_spec,
        out_shape=jax.ShapeDtypeStruct((npad, d), BF16),
        compiler_params=_params("parallel", "arbitrary"),
        name="moe_experts",
    )(tile_expert, n_used, x_sorted, wg, wu, wd)


def _combine_kernel(h_ref, y0_ref, y1_ref, wt_ref, mod_ref, *rest):
    y = wt_ref[:, 0:1] * y0_ref[...] + wt_ref[:, 1:2] * y1_ref[...]
    h = h_ref[...] + mod_ref[5:6, :] * y
    if len(rest) == 2:
        fg_ref, o_ref = rest
        o_ref[...] = _rms(h, fg_ref[...])
    else:
        rest[0][...] = h


def _combine(h, y0, y1, wt, mod, dims, n_rows, final_g=None):
    n, d = h.shape
    b, s, ctx = dims
    tm = ROW_TILE
    row = lambda i: (i, 0)
    in_specs = [pl.BlockSpec((tm, d), row), pl.BlockSpec((tm, d), row), pl.BlockSpec((tm, d), row),
                pl.BlockSpec((tm, LANES), row),
                pl.BlockSpec((None, N_MOD, d), lambda i: (jnp.minimum(i * tm // s, b), 0, 0))]
    args = [h, y0, y1, wt, mod]
    if final_g is not None:
        in_specs.append(pl.BlockSpec((1, d), lambda i: (0, 0)))
        args.append(final_g)
    return pl.pallas_call(
        _combine_kernel,
        grid=(n_rows // tm,),
        in_specs=in_specs,
        out_specs=pl.BlockSpec((tm, d), row),
        out_shape=jax.ShapeDtypeStruct((n_rows if final_g is not None else n, d), F32),
        input_output_aliases={} if final_g is not None else {0: 0},
        compiler_params=_params("parallel"),
        name="moe_combine",
    )(*args)


def _moe(h, mod, ng, router, wg, wu, wd, layer, dims, n_rows, final_g=None, defer_combine=False):
    tm, rt = MOE_TILE, ROW_TILE
    v, idx, wt, lst, base = _route(h, mod, ng, router, dims, n_rows)
    base = base[:, 0, :N_EXPERTS]
    e_sel, pos = idx[:, 0:2], idx[:, 2:4]
    last = e_sel[-rt:]
    counts = base[-1] + jnp.sum(last[:, :, None] == jnp.arange(N_EXPERTS), axis=(0, 1))
    tiles_per = (counts + tm - 1) // tm
    tile_end = jnp.cumsum(tiles_per)
    group_start = (tile_end - tiles_per) * tm
    slot = group_start[e_sel] + pos
    n_tiles = 2 * n_rows // tm + N_EXPERTS
    tile_expert = jnp.minimum(jnp.sum(jnp.arange(n_tiles)[:, None] >= tile_end[None, :], axis=1),
                              N_EXPERTS - 1).astype(jnp.int32)
    n_used = tile_end[-1:].astype(jnp.int32)
    rank = (jnp.arange(n_tiles * tm, dtype=jnp.int32).reshape(n_tiles, tm)
            - group_start[tile_expert][:, None])
    base_te = base.T[tile_expert]
    owns = base_te[:, None, :] <= rank[:, :, None]
    rtile = jnp.sum(owns, axis=-1) - 1
    local = rank - jnp.max(jnp.where(owns, base_te[:, None, :], 0), axis=-1)
    flat = jnp.clip((rtile * rt + local) * N_EXPERTS + tile_expert[:, None], 0, n_rows * N_EXPERTS - 1)
    src = rtile * rt + jnp.take(lst[:, :N_EXPERTS].reshape(-1), flat, mode="clip")
    spread = jnp.arange(n_tiles * tm, dtype=jnp.int32).reshape(n_tiles, tm) % n_rows
    src = jnp.where(rank < counts[tile_expert][:, None], src, spread).reshape(-1)
    x_sorted = jnp.take(v, src, axis=0, mode="clip")
    out_sorted = _moe_experts(x_sorted, tile_expert, n_used, wg, wu, wd, layer)
    y0 = jnp.take(out_sorted, slot[:, 0], axis=0, mode="clip")
    y1 = jnp.take(out_sorted, slot[:, 1], axis=0, mode="clip")
    if defer_combine:
        return h, (y0, y1, wt, mod)
    return _combine(h, y0, y1, wt, mod, dims, n_rows, final_g), None


def _final_kernel(h_ref, g_ref, o_ref):
    o_ref[...] = _rms(h_ref[...], g_ref[...])


def _final_norm(h, g, n_rows):
    n, d = h.shape
    tm = ROW_TILE
    row = lambda i: (i, 0)
    return pl.pallas_call(
        _final_kernel,
        grid=(n_rows // tm,),
        in_specs=[pl.BlockSpec((tm, d), row), pl.BlockSpec((1, d), lambda i: (0, 0))],
        out_specs=pl.BlockSpec((tm, d), row),
        out_shape=jax.ShapeDtypeStruct((n_rows, d), F32),
        compiler_params=_params("parallel"),
        name="final_norm",
    )(h, g)


def _pad_cols(w, width):
    return jnp.pad(w, ((0, 0), (0, width - w.shape[-1])))


def _pack_in(w_in):
    cq = w_in[:, :Q_LORA]
    ckv = w_in[:, Q_LORA:KR_OFF]
    kr = w_in[:, KR_OFF:KR_OFF + QK_ROPE]
    rest = w_in[:, KR_OFF + QK_ROPE:]
    return jnp.concatenate([cq, ckv, _pad_cols(kr, LANES), rest], axis=-1).astype(BF16)


def _pack_q(w_uq):
    w = w_uq.reshape(Q_LORA, N_HEADS, QK_DIM)
    z = jnp.zeros((Q_LORA, N_HEADS, HEAD_PAD - QK_DIM), w.dtype)
    return jnp.concatenate([w, z], axis=-1).reshape(Q_LORA, QKV_W).astype(BF16)


def _pack_kv(w_ukv):
    w = w_ukv.reshape(KV_LORA, N_HEADS, QK_NOPE + V_HEAD)
    z = jnp.zeros((KV_LORA, N_HEADS, HEAD_PAD - QK_NOPE), w.dtype)
    k_all = jnp.concatenate([w[..., :QK_NOPE], z], axis=-1).reshape(KV_LORA, QKV_W)
    v_all = jnp.concatenate([w[..., QK_NOPE:], z], axis=-1).reshape(KV_LORA, QKV_W)
    return jnp.concatenate([k_all, v_all], axis=-1).astype(BF16)


def _rope_tabs(s, tm):
    rows = s // GRID_W
    row = jnp.repeat(jnp.arange(rows, dtype=F32), GRID_W)
    col = jnp.tile(jnp.arange(GRID_W, dtype=F32), rows)
    half = QK_ROPE // 2
    inv = 1.0 / (ROPE_BASE ** (jnp.arange(0, half, 2, dtype=F32) / half))
    ar = row[:, None] * inv[None, :]
    ac = col[:, None] * inv[None, :]
    ang = jnp.concatenate([ar, ar, ac, ac], axis=-1)
    cos = jnp.concatenate([jnp.cos(ang), jnp.ones((tm, QK_ROPE), F32)], axis=0)
    sin = jnp.concatenate([jnp.sin(ang), jnp.zeros((tm, QK_ROPE), F32)], axis=0)
    t = cos.shape[0]
    first = (jnp.arange(QK_ROPE) % (QK_ROPE // 2)) < QK_ROPE // 4
    sin1 = jnp.where(first, -sin, 0.0)
    sin2 = jnp.where(first, 0.0, sin)
    scale = QK_DIM ** -0.5 * LOG2E

    def q_tab(rope_part, nope_val):
        return jnp.concatenate([jnp.full((t, QK_NOPE), nope_val, F32), rope_part * scale,
                                jnp.zeros((t, HEAD_PAD - QK_DIM), F32)], axis=-1)

    return (q_tab(cos, scale), q_tab(sin1, 0.0), q_tab(sin2, 0.0),
            _pad_cols(cos, LANES), _pad_cols(sin1, LANES), _pad_cols(sin2, LANES))


def kernel(x, c, ctx, c_ctx, w_mod, b_mod, norm_g, final_g, ab_w_in, ab_g_cq, ab_g_ckv, ab_w_uq, ab_w_ukv, hgrn_lb_logits, hgrn_g_norm, ab_w_out, ffn_w_gate, ffn_w_up, ffn_w_down, pool_w, pool_b, pool_scale, moe_router, moe_w_gate, moe_w_up, moe_w_down):
    b, s, d = x.shape
    ctx_len = ctx.shape[1]
    depth = w_mod.shape[0]
    dims = (b, s, ctx_len)
    n_lat, n_ctx = b * s, b * ctx_len
    assert d == D_MODEL and s % ROW_TILE == 0 and ctx_len % ROW_TILE == 0
    assert ctx_len % GLA_BLOCK == 0 and s % GLA_BLOCK == 0 and n_lat % ctx_len == 0

    h = (x.reshape(n_lat, d), ctx.reshape(n_ctx, d))
    n_groups = ((b + 1 + SUBLANES - 1) // SUBLANES) * SUBLANES
    cvec = jnp.concatenate([c, c_ctx[None, :], jnp.zeros((n_groups - b - 1, d), F32)], axis=0)
    mod_all = _mod_table(cvec, w_mod, b_mod).reshape(depth, n_groups, N_MOD, d)

    lb_p = jax.nn.softmax(hgrn_lb_logits.astype(F32), axis=0)
    lb_all = jnp.cumsum(lb_p, axis=0) - lb_p[:1]
    tabs = _rope_tabs(s, ROW_TILE)

    pending = None
    for l in range(depth):
        j = l // 2
        even = l % 2 == 0
        ctx_later = any(m % 2 == 0 for m in range(l + 1, depth))
        n_rows = n_lat + n_ctx if ctx_later else n_lat
        mod = mod_all[l]
        ng1, ng2 = norm_g[l, 0][None, :], norm_g[l, 1][None, :]
        if even:
            outs = _inproj(
                h, mod, ng1, _pack_in(ab_w_in[j]), ab_g_cq[j][None, :], ab_g_ckv[j][None, :],
                _pack_q(ab_w_uq[j]), _pack_kv(ab_w_ukv[j]), lb_all[j], tabs, dims, pending)
            (q, k, v, qh, kf, lgf, kb, lgb, vh, hg) = outs[:10]
            if pending is not None:
                h, pending = outs[10], None
            a = _attention_latent(q, k, v, dims)
            if ctx_later:
                a = jnp.concatenate([a, _attention_ctx(q, k, v, dims)], axis=0)
            o_f, o_b = _gla(qh, kf, lgf, kb, lgb, vh, dims)
            h = _mix_ffn(a, o_f, o_b, hg, hgrn_g_norm[j][None, :], ab_w_out[j].astype(BF16), h, mod, ng2,
                         ffn_w_gate[j].astype(BF16), ffn_w_up[j].astype(BF16),
                         ffn_w_down[j].astype(BF16), dims, n_rows)
        else:
            h = _pool(h, mod, ng1, pool_w[j].astype(BF16), pool_b[j], pool_scale[j][None, :], dims, n_rows)
            r_full = _pad_cols(moe_router[j], LANES)
            r_hi = r_full.astype(BF16)
            router = jnp.stack([r_hi, (r_full - r_hi.astype(F32)).astype(BF16)])
            last = l == depth - 1
            defer = not last and n_rows == n_lat + n_ctx
            h, pending = _moe(h, mod, ng2, router, moe_w_gate, moe_w_up, moe_w_down, j, dims, n_rows,
                              final_g[None, :] if last else None, defer)
            if last:
                return h.reshape(b, s, d)
    return _final_norm(h, final_g[None, :], n_lat).reshape(b, s, d)
```

```python
import functools

import jax
import jax.numpy as jnp
from jax import lax
from jax.experimental import pallas as pl
from jax.experimental.pallas import tpu as pltpu

F32 = jnp.float32
BF16 = jnp.bfloat16

D_MODEL = 1024
GRID_W = 64
N_MOD = 6
EPS = 1e-6
N_HEADS = 8
V_HEAD = 64
QK_NOPE = 64
QK_ROPE = 32
QK_DIM = QK_NOPE + QK_ROPE
Q_LORA = 384
KV_LORA = 256
ROPE_BASE = 10000.0
MLA_W = 512
HGRN_W = 512
HGRN_K = 128
N_HGRN_HEADS = 4
POOL_WINDOWS = (2, 4, 8, 16)
POOL_C = 256
N_EXPERTS = 8

LANES = 128
SUBLANES = 8
VMEM_LIMIT_BYTES = 56 * 1024 * 1024

HEAD_PAD = LANES
QKV_W = N_HEADS * HEAD_PAD
KR_OFF = Q_LORA + KV_LORA
H_OFF = KR_OFF + LANES
IN_COLS = H_OFF + 5 * HGRN_W
GLA_CHUNK = 64
GLA_BLOCK = 256
GLA_PHASE_GROUP = 2
ROW_TILE = 256
ROUTE_SUBTILES = 4
MLP_TILE = 512
ATTN_Q_TILE = 512
ATTN_HEAD_GROUP = 4
MOE_TILE = 1024
MOE_FF_TILE = 512
MOE_FF_SPLIT = 2
LOG2E = 1.4426950408889634


def _params(*sem):
    return pltpu.CompilerParams(dimension_semantics=sem, vmem_limit_bytes=VMEM_LIMIT_BYTES)


def _dot(a, b):
    return jnp.dot(a, b, preferred_element_type=F32)


def _dot_nt(a, b):
    return lax.dot_general(a, b, (((1,), (1,)), ((), ())), preferred_element_type=F32)


def _dot_tn(a, b):
    return lax.dot_general(a, b, (((0,), (0,)), ((), ())), preferred_element_type=F32)


def _rms(x, g):
    return x * lax.rsqrt(jnp.mean(x * x, axis=-1, keepdims=True) + EPS) * g


def _silu(x):
    return x * jax.nn.sigmoid(x)


def _norm_mod(h, g, shift, scale):
    return _rms(h, g) * (1.0 + scale) + shift


def _mod_kernel(c_ref, w_ref, b_ref, o_ref):
    c = c_ref[...]
    o_ref[...] = jnp.dot(_silu(c), w_ref[...], preferred_element_type=F32,
                         precision=lax.Precision.HIGHEST) + b_ref[...]


def _mod_table(cvec, w_mod, b_mod):
    depth, d, nd = w_mod.shape
    g = cvec.shape[0]
    tn = 1536
    return pl.pallas_call(
        _mod_kernel,
        grid=(depth, nd // tn),
        in_specs=[pl.BlockSpec((g, d), lambda l, j: (0, 0)),
                  pl.BlockSpec((None, d, tn), lambda l, j: (l, 0, j)),
                  pl.BlockSpec((None, 1, tn), lambda l, j: (l, 0, j))],
        out_specs=pl.BlockSpec((None, g, tn), lambda l, j: (l, 0, j)),
        out_shape=jax.ShapeDtypeStruct((depth, g, nd), F32),
        compiler_params=_params("parallel", "parallel"),
        name="mod_table",
    )(cvec, w_mod, b_mod.reshape(depth, 1, nd))


def _inproj_body(h, mod_ref, ng_ref, win_ref, gcq_ref, gckv_ref, wq_ref, wkv_ref,
                   lb_ref, cq_tab, s1q_tab, s2q_tab, ck_tab, s1k_tab, s2k_tab,
                   q_ref, k_ref, v_ref, qh_ref, kf_ref, lgf_ref, kb_ref, lgb_ref, vh_ref, hg_ref):
    half_rot = QK_ROPE // 4

    def rope(x, c_tab, s1_tab, s2_tab):
        return (x * c_tab[...] + pltpu.roll(x, LANES - half_rot, 1) * s1_tab[...]
                + pltpu.roll(x, half_rot, 1) * s2_tab[...])

    u = _norm_mod(h, ng_ref[...], mod_ref[0:1, :], mod_ref[1:2, :]).astype(BF16)
    p = _dot(u, win_ref[...])
    cqn = _rms(p[:, 0:Q_LORA], gcq_ref[...]).astype(BF16)
    a = _dot(cqn, wq_ref[...])
    q_ref[...] = jnp.concatenate(
        [rope(a[:, hd * HEAD_PAD:(hd + 1) * HEAD_PAD], cq_tab, s1q_tab, s2q_tab) for hd in range(N_HEADS)],
        axis=-1).astype(BF16)
    ckvn = _rms(p[:, Q_LORA:KR_OFF], gckv_ref[...]).astype(BF16)
    kv = _dot(ckvn, wkv_ref[...])
    kr = pltpu.roll(rope(p[:, KR_OFF:H_OFF], ck_tab, s1k_tab, s2k_tab), QK_NOPE, 1)
    k_ref[...] = jnp.concatenate(
        [kv[:, hd * HEAD_PAD:(hd + 1) * HEAD_PAD] + kr for hd in range(N_HEADS)], axis=-1).astype(BF16)
    lane = lax.broadcasted_iota(jnp.int32, (1, QKV_W), 1)
    ones_col = (lane % HEAD_PAD == V_HEAD).astype(F32)
    v_ref[...] = (kv[:, QKV_W:] + ones_col).astype(BF16)
    o = H_OFF
    qh_ref[...] = _silu(p[:, o:o + HGRN_W])
    kf = (1.0 - lb_ref[0:1, :]) * jax.nn.sigmoid(-p[:, o + HGRN_W:o + 2 * HGRN_W])
    kf_ref[...] = kf
    lgf_ref[...] = jnp.log1p(-kf)
    kb = (1.0 - lb_ref[1:2, :]) * jax.nn.sigmoid(-p[:, o + 2 * HGRN_W:o + 3 * HGRN_W])
    kb_ref[...] = kb
    lgb_ref[...] = jnp.log1p(-kb)
    vh_ref[...] = p[:, o + 3 * HGRN_W:o + 4 * HGRN_W]
    hg_ref[...] = p[:, o + 4 * HGRN_W:o + 5 * HGRN_W]


def _rows_from(refs, lat_tiles):
    if lat_tiles is None:
        return refs[0][...], refs[1:]
    return jnp.where(pl.program_id(0) < lat_tiles, refs[0][...], refs[1][...]), refs[2:]


def _inproj_kernel(*refs, lat_tiles, pending):
    h, rest = _rows_from(refs, lat_tiles)
    if pending:
        y0_ref, y1_ref, wt_ref, pmod_ref, *rest = rest
        *rest, hnew_ref = rest
        h = h + pmod_ref[5:6, :] * (wt_ref[:, 0:1] * y0_ref[...] + wt_ref[:, 1:2] * y1_ref[...])
        hnew_ref[...] = h
    _inproj_body(h, *rest)


def _split_rows(h, tm):
    if not isinstance(h, tuple):
        return [pl.BlockSpec((tm, h.shape[1]), lambda i: (i, 0))], [h], None
    x2, c2 = h
    lat_tiles = x2.shape[0] // tm
    specs = [pl.BlockSpec((tm, x2.shape[1]), lambda i: (jnp.minimum(i, lat_tiles - 1), 0)),
             pl.BlockSpec((tm, c2.shape[1]), lambda i: (jnp.maximum(i - lat_tiles, 0), 0))]
    return specs, [x2, c2], lat_tiles


def _inproj(h, mod, ng, win, gcq, gckv, wq, wkv, lb, tabs, dims, pending=None):
    b, s, ctx = dims
    n, d = b * (s + ctx), D_MODEL
    tm = ROW_TILE
    h_specs, h_args, lat_tiles = _split_rows(h, tm)
    extra_specs, extra_args, extra_out_specs, extra_out_shape, aliases = [], [], [], [], {}
    if pending is not None:
        extra_specs = [pl.BlockSpec((tm, d), lambda i: (i, 0)), pl.BlockSpec((tm, d), lambda i: (i, 0)),
                       pl.BlockSpec((tm, LANES), lambda i: (i, 0)),
                       pl.BlockSpec((None, N_MOD, d), lambda i: (jnp.minimum(i * tm // s, b), 0, 0))]
        extra_args = list(pending)
        extra_out_specs = [pl.BlockSpec((tm, d), lambda i: (i, 0))]
        extra_out_shape = [jax.ShapeDtypeStruct((n, d), F32)]
        aliases = {0: 10}
    n_lat_t = b * s // tm
    pos_blocks = s // tm

    def grp(i):
        return jnp.minimum(i * tm // s, b)

    def pos(i):
        return jnp.where(i < n_lat_t, i % pos_blocks, pos_blocks)

    row = lambda i: (i, 0)
    const = lambda i: (0, 0)
    tab_spec = pl.BlockSpec((tm, LANES), lambda i: (pos(i), 0))
    wide = jax.ShapeDtypeStruct((n, QKV_W), BF16)
    hg = jax.ShapeDtypeStruct((n, HGRN_W), F32)
    return pl.pallas_call(
        functools.partial(_inproj_kernel, lat_tiles=lat_tiles, pending=pending is not None),
        grid=(n // tm,),
        in_specs=h_specs + extra_specs + [
                  pl.BlockSpec((None, N_MOD, d), lambda i: (grp(i), 0, 0)),
                  pl.BlockSpec((1, d), const),
                  pl.BlockSpec(win.shape, const),
                  pl.BlockSpec((1, Q_LORA), const),
                  pl.BlockSpec((1, KV_LORA), const),
                  pl.BlockSpec(wq.shape, const),
                  pl.BlockSpec(wkv.shape, const),
                  pl.BlockSpec((2, HGRN_W), const),
                  tab_spec, tab_spec, tab_spec, tab_spec, tab_spec, tab_spec],
        out_specs=([pl.BlockSpec((tm, QKV_W), row)] * 3 + [pl.BlockSpec((tm, HGRN_W), row)] * 7
                   + extra_out_specs),
        out_shape=[wide] * 3 + [hg] * 7 + extra_out_shape,
        input_output_aliases=aliases,
        compiler_params=_params("parallel"),
        name="inproj",
    )(*h_args, *extra_args, mod, ng, win, gcq, gckv, wq, wkv, lb, *tabs)


def _attn_kernel(*refs, n_kv):
    q_ref = refs[0]
    k_refs = refs[1:1 + 2 * n_kv:2]
    v_refs = refs[2:2 + 2 * n_kv:2]
    o_ref = refs[-1]
    for h0 in range(0, N_HEADS, ATTN_HEAD_GROUP):
        heads = range(h0, h0 + ATTN_HEAD_GROUP)
        sls = {hd: slice(hd * HEAD_PAD, (hd + 1) * HEAD_PAD) for hd in heads}
        scores = {hd: [_dot_nt(q_ref[:, sls[hd]], k_ref[:, sls[hd]]) for k_ref in k_refs] for hd in heads}
        ms = {hd: functools.reduce(jnp.maximum, [jnp.max(sc, axis=-1, keepdims=True) for sc in scores[hd]])
              for hd in heads}
        for hd in heads:
            acc = None
            for sc, v_ref in zip(scores[hd], v_refs):
                part = _dot(jnp.exp2(sc - ms[hd]).astype(BF16), v_ref[:, sls[hd]])
                acc = part if acc is None else acc + part
            o = acc[:, :V_HEAD] / acc[:, V_HEAD:V_HEAD + 1]
            o_ref[:, hd * V_HEAD:(hd + 1) * V_HEAD] = o.astype(o_ref.dtype)


def _attention_latent(q, k, v, dims):
    b, s, ctx = dims
    tq = min(ATTN_Q_TILE, s)
    nq = s // tq
    lat = lambda bi, j: (bi, 0)
    cx = lambda bi, j: (b * s // ctx + bi, 0)
    return pl.pallas_call(
        functools.partial(_attn_kernel, n_kv=2),
        grid=(b, nq),
        in_specs=[pl.BlockSpec((tq, QKV_W), lambda bi, j: (bi * nq + j, 0)),
                  pl.BlockSpec((s, QKV_W), lat), pl.BlockSpec((s, QKV_W), lat),
                  pl.BlockSpec((ctx, QKV_W), cx), pl.BlockSpec((ctx, QKV_W), cx)],
        out_specs=pl.BlockSpec((tq, MLA_W), lambda bi, j: (bi * nq + j, 0)),
        out_shape=jax.ShapeDtypeStruct((b * s, MLA_W), BF16),
        compiler_params=_params("parallel", "arbitrary"),
        name="attn_latent",
    )(q, k, v, k, v)


def _attention_ctx(q, k, v, dims):
    b, s, ctx = dims
    cx = lambda bi: (b * s // ctx + bi, 0)
    return pl.pallas_call(
        functools.partial(_attn_kernel, n_kv=1),
        grid=(b,),
        in_specs=[pl.BlockSpec((ctx, QKV_W), cx), pl.BlockSpec((ctx, QKV_W), cx),
                  pl.BlockSpec((ctx, QKV_W), cx)],
        out_specs=pl.BlockSpec((ctx, MLA_W), lambda bi: (bi, 0)),
        out_shape=jax.ShapeDtypeStruct((b * ctx, MLA_W), BF16),
        compiler_params=_params("parallel"),
        name="attn_ctx",
    )(q, k, v)


def _bcast_block_row(g, bs, row):
    if bs == GLA_CHUNK:
        return g[row:row + 1, :]
    g3 = g.reshape(GLA_CHUNK // bs, bs, g.shape[-1])
    return jnp.broadcast_to(g3[:, row:row + 1, :], g3.shape).reshape(g.shape)


def _gla_consts(reverse):
    c = GLA_CHUNK
    t_idx = lax.broadcasted_iota(jnp.int32, (c, c), 0)
    s_idx = lax.broadcasted_iota(jnp.int32, (c, c), 1)
    pt = (c - 1 - t_idx) if reverse else t_idx
    ps = (c - 1 - s_idx) if reverse else s_idx
    tri = (ps <= pt).astype(BF16)
    levels = []
    for half in (32, 16, 8):
        bs = 2 * half
        mask = (t_idx // bs == s_idx // bs) & (pt % bs >= half) & (ps % bs < half)
        levels.append((bs, half if reverse else half - 1, False, mask))
    dmask = (t_idx // SUBLANES == s_idx // SUBLANES) & (ps <= pt)
    levels.append((SUBLANES, 4 if reverse else 3, True, dmask))
    end_row = 0 if reverse else c - 1
    return tri, levels, end_row


def _gla_group(dirs, consts, states, steps):
    c = GLA_CHUNK
    n_chunks = GLA_BLOCK // c
    units = []
    for step in steps:
        for di, (q_ref, k_ref, lg_ref, v_ref, o_ref, _, reverse) in enumerate(dirs):
            tri, levels, end_row = consts[di]
            ci = n_chunks - 1 - step if reverse else step
            rows = slice(ci * c, (ci + 1) * c)
            lg = lg_ref[rows, :]
            hi = lg.astype(BF16)
            r1 = lg - hi.astype(F32)
            mid = r1.astype(BF16)
            lo = (r1 - mid.astype(F32)).astype(BF16)
            gc = (_dot(tri, hi) + _dot(tri, mid) + _dot(tri, lo)) * LOG2E
            for hd in range(N_HGRN_HEADS):
                hs = slice(hd * HGRN_K, (hd + 1) * HGRN_K)
                units.append(dict(di=di, hd=hd, rows=rows, hs=hs, g=gc[:, hs], o_ref=o_ref,
                                  qh=q_ref[rows, hs], kh=k_ref[rows, hs],
                                  vh=v_ref[rows, hs].astype(BF16), levels=levels, end_row=end_row,
                                  attn=jnp.zeros((c, c), F32)))
    for li in range(len(units[0]["levels"])):
        for u in units:
            bs, row, diag, mask = u["levels"][li]
            g, qh, kh = u["g"], u["qh"], u["kh"]
            d = g - _bcast_block_row(g, bs, row)
            if diag:
                qf = (qh * jnp.exp2(d)).astype(BF16)
                kf = (kh * jnp.exp2(-d)).astype(BF16)
            else:
                e = jnp.exp2(-jnp.abs(d))
                qf = (qh * e).astype(BF16)
                kf = (kh * e).astype(BF16)
            u["attn"] = jnp.where(mask, _dot_nt(qf, kf), u["attn"])
    for u in units:
        g, qh, kh, vh = u["g"], u["qh"], u["kh"], u["vh"]
        st = states[u["di"]][u["hd"]]
        g_end = g[u["end_row"]:u["end_row"] + 1, :]
        qe = (qh * jnp.exp2(g)).astype(BF16)
        ke = (kh * jnp.exp2(g_end - g)).astype(BF16)
        o = _dot_nt(qe, st.astype(BF16)) + _dot(u["attn"].astype(BF16), vh)
        u["o_ref"][u["rows"], u["hs"]] = o
        states[u["di"]][u["hd"]] = st * jnp.exp2(g_end) + _dot_tn(vh, ke)


def _gla_kernel(qf_ref, kf_ref, lgf_ref, vf_ref, qb_ref, kb_ref, lgb_ref, vb_ref,
                of_ref, ob_ref, stf_ref, stb_ref):
    @pl.when(pl.program_id(1) == 0)
    def _():
        stf_ref[...] = jnp.zeros_like(stf_ref)
        stb_ref[...] = jnp.zeros_like(stb_ref)

    dirs = [(qf_ref, kf_ref, lgf_ref, vf_ref, of_ref, stf_ref, False),
            (qb_ref, kb_ref, lgb_ref, vb_ref, ob_ref, stb_ref, True)]
    consts = [_gla_consts(False), _gla_consts(True)]
    states = [[d[5][hd] for hd in range(N_HGRN_HEADS)] for d in dirs]
    for step0 in range(0, GLA_BLOCK // GLA_CHUNK, GLA_PHASE_GROUP):
        _gla_group(dirs, consts, states, range(step0, step0 + GLA_PHASE_GROUP))
    for di, d in enumerate(dirs):
        for hd in range(N_HGRN_HEADS):
            d[5][hd] = states[di][hd]


def _gla(qh, kf, lgf, kb, lgb, vh, dims):
    n = qh.shape[0]
    b, s, ctx = dims
    blk = GLA_BLOCK
    ncb, nsb, nlat = ctx // blk, s // blk, b * s // blk

    def fwd(bi, j):
        return (jnp.where(j < ncb, nlat + bi * ncb + j, bi * nsb + (j - ncb)), 0)

    def bwd(bi, j):
        return (jnp.where(j < ncb, nlat + bi * ncb + (ncb - 1 - j), bi * nsb + (nsb - 1 - (j - ncb))), 0)

    fs = pl.BlockSpec((blk, HGRN_W), fwd)
    bs = pl.BlockSpec((blk, HGRN_W), bwd)
    out = jax.ShapeDtypeStruct((n, HGRN_W), F32)
    return pl.pallas_call(
        _gla_kernel,
        grid=(b, ncb + nsb),
        in_specs=[fs, fs, fs, fs, bs, bs, bs, bs],
        out_specs=[fs, bs],
        out_shape=[out, out],
        scratch_shapes=[pltpu.VMEM((N_HGRN_HEADS, HGRN_K, HGRN_K), F32)] * 2,
        compiler_params=_params("parallel", "arbitrary"),
        name="gla",
    )(qh, kf, lgf, vh, qh, kb, lgb, vh)


def _mix_ffn_kernel(*refs, lat_tiles):
    h, (a_ref, of_ref, ob_ref, hg_ref, gn_ref, wo_ref, mod_ref, ng_ref, wg_ref, wu_ref, wd_ref,
        o_ref) = _rows_from(refs, lat_tiles)
    rows = h.shape[0] // 2
    halves = [slice(r * rows, (r + 1) * rows) for r in range(2)]
    mixed = []
    for rs in halves:
        o = of_ref[rs, :] + ob_ref[rs, :]
        gate = _silu(hg_ref[rs, :])
        ys = []
        for hd in range(N_HGRN_HEADS):
            hs = slice(hd * HGRN_K, (hd + 1) * HGRN_K)
            ys.append(_rms(o[:, hs], gn_ref[...]) * gate[:, hs])
        y = jnp.concatenate(ys, axis=-1).astype(BF16)
        mix = _dot(a_ref[rs, :], wo_ref[0:MLA_W, :]) + _dot(y, wo_ref[MLA_W:, :])
        mixed.append(h[rs, :] + mod_ref[2:3, :] * mix)
    for rs, h1 in zip(halves, mixed):
        v = _norm_mod(h1, ng_ref[...], mod_ref[3:4, :], mod_ref[4:5, :]).astype(BF16)
        hid = (_silu(_dot(v, wg_ref[...])) * _dot(v, wu_ref[...])).astype(BF16)
        o_ref[rs, :] = h1 + mod_ref[5:6, :] * _dot(hid, wd_ref[...])


def _mix_ffn(a, of, ob, hg, gn, wo, h, mod, ng, wg, wu, wd, dims, n_rows):
    b, s, ctx = dims
    n, d = b * (s + ctx), D_MODEL
    tm = MLP_TILE
    row = lambda i: (i, 0)
    const = lambda i: (0, 0)
    half = pl.BlockSpec((tm, HGRN_W), row)
    resident = lambda w: pl.BlockSpec(w.shape, const, pipeline_mode=pl.Buffered(1))
    h_specs, h_args, lat_tiles = _split_rows(h, tm)
    return pl.pallas_call(
        functools.partial(_mix_ffn_kernel, lat_tiles=lat_tiles),
        grid=(n_rows // tm,),
        in_specs=h_specs + [pl.BlockSpec((tm, MLA_W), row), half, half, half,
                            pl.BlockSpec((1, HGRN_K), const), resident(wo),
                            pl.BlockSpec((None, N_MOD, d), lambda i: (jnp.minimum(i * tm // s, b), 0, 0)),
                            pl.BlockSpec((1, d), const), resident(wg), resident(wu), resident(wd)],
        out_specs=pl.BlockSpec((tm, d), row),
        out_shape=jax.ShapeDtypeStruct((n, d), F32),
        input_output_aliases={} if lat_tiles is not None else {0: 0},
        compiler_params=_params("parallel"),
        name="mix_ffn",
    )(*h_args, a, of, ob, hg, gn, wo, mod, ng, wg, wu, wd)


def _pool_kernel(h_ref, prev_ref, next_ref, mod_ref, ng_ref, w_ref, b_ref, sc_ref, o_ref,
                 *, tm, seq_tiles_lat, n_lat_tiles, seq_tiles_ctx):
    i = pl.program_id(0)
    halo = SUBLANES
    shift, scale = mod_ref[0:1, :], mod_ref[1:2, :]
    in_lat = i < n_lat_tiles
    j = jnp.where(in_lat, i % seq_tiles_lat, (i - n_lat_tiles) % seq_tiles_ctx)
    n_seq_tiles = jnp.where(in_lat, seq_tiles_lat, seq_tiles_ctx)
    h = h_ref[...]
    u = _norm_mod(h, ng_ref[...], shift, scale)
    up = _norm_mod(prev_ref[...], ng_ref[...], shift, scale)
    un = _norm_mod(next_ref[...], ng_ref[...], shift, scale)
    ext = jnp.concatenate([jnp.where(j > 0, up, 0.0), u, jnp.where(j < n_seq_tiles - 1, un, 0.0)], axis=0)
    ext_hi = ext.astype(BF16)
    ext_lo = (ext - ext_hi.astype(F32)).astype(BF16)
    off = (lax.broadcasted_iota(jnp.int32, (tm, tm + 2 * halo), 1)
           - lax.broadcasted_iota(jnp.int32, (tm, tm + 2 * halo), 0) - halo)
    pos = j * tm + lax.broadcasted_iota(jnp.int32, (tm, 1), 0)
    t_len = n_seq_tiles * tm
    css = [slice(g * POOL_C, (g + 1) * POOL_C) for g in range(len(POOL_WINDOWS))]
    bands = [((off >= -(win // 2)) & (off < win // 2)).astype(BF16) for win in POOL_WINDOWS]
    accs = [_dot(bd, ext_hi[:, cs]) + _dot(bd, ext_lo[:, cs]) for bd, cs in zip(bands, css)]
    pooled = []
    for win, cs, acc in zip(POOL_WINDOWS, css, accs):
        lo = jnp.maximum(pos - win // 2, 0)
        hi = jnp.minimum(pos - win // 2 + win, t_len)
        cnt = (hi - lo).astype(F32)
        pooled.append((acc / cnt - u[:, cs]).astype(BF16))
    outs = [_dot(pl_g, w_ref[g]) + b_ref[g:g + 1, :] for g, pl_g in enumerate(pooled)]
    y = jnp.concatenate(outs, axis=-1) * sc_ref[...]
    o_ref[...] = h + mod_ref[2:3, :] * y


def _pool(h, mod, ng, w, bias, scale, dims, n_rows):
    n, d = h.shape
    b, s, ctx = dims
    tm = ROW_TILE
    hb = tm // SUBLANES
    n_blk8 = n // SUBLANES
    row = lambda i: (i, 0)
    const = lambda i: (0, 0)
    kern = functools.partial(_pool_kernel, tm=tm, seq_tiles_lat=s // tm, n_lat_tiles=b * s // tm,
                             seq_tiles_ctx=ctx // tm)
    return pl.pallas_call(
        kern,
        grid=(n_rows // tm,),
        in_specs=[pl.BlockSpec((tm, d), row),
                  pl.BlockSpec((SUBLANES, d), lambda i: (jnp.maximum(i * hb - 1, 0), 0)),
                  pl.BlockSpec((SUBLANES, d), lambda i: (jnp.minimum((i + 1) * hb, n_blk8 - 1), 0)),
                  pl.BlockSpec((None, N_MOD, d), lambda i: (jnp.minimum(i * tm // s, b), 0, 0)),
                  pl.BlockSpec((1, d), const),
                  pl.BlockSpec(w.shape, lambda i: (0, 0, 0)),
                  pl.BlockSpec(bias.shape, const),
                  pl.BlockSpec((1, d), const)],
        out_specs=pl.BlockSpec((tm, d), row),
        out_shape=jax.ShapeDtypeStruct((n_rows, d), F32),
        compiler_params=_params("parallel"),
        name="pool",
    )(h, h, h, mod, ng, w, bias, scale)


def _route_kernel(h_ref, mod_ref, ng_ref, r_ref, v_ref, idx_ref, wt_ref, lst_ref, base_ref, run_ref):
    rt = ROW_TILE
    subs = [slice(k * rt, (k + 1) * rt) for k in range(h_ref.shape[0] // rt)]

    @pl.when(pl.program_id(0) == 0)
    def _():
        run_ref[...] = jnp.zeros_like(run_ref)

    lane = lax.broadcasted_iota(jnp.int32, (rt, LANES), 1)
    ti = lax.broadcasted_iota(jnp.int32, (rt, rt), 0)
    tj = lax.broadcasted_iota(jnp.int32, (rt, rt), 1)
    neg = jnp.float32(-jnp.inf)
    vs = [_norm_mod(h_ref[sl, :], ng_ref[...], mod_ref[3:4, :], mod_ref[4:5, :]) for sl in subs]
    v_his = [v.astype(BF16) for v in vs]
    for sl, v_hi in zip(subs, v_his):
        v_ref[sl, :] = v_hi
    logits = []
    for v, v_hi in zip(vs, v_his):
        v_lo = (v - v_hi.astype(F32)).astype(BF16)
        lg = _dot(v_hi, r_ref[0]) + (_dot(v_lo, r_ref[0]) + _dot(v_hi, r_ref[1]))
        logits.append(jnp.where(lane < N_EXPERTS, lg, neg))
    m1s = [jnp.max(lg, axis=-1, keepdims=True) for lg in logits]
    i1s = [jnp.min(jnp.where(lg == m1, lane, LANES), axis=-1, keepdims=True) for lg, m1 in zip(logits, m1s)]
    rests = [jnp.where(lane == i1, neg, lg) for lg, i1 in zip(logits, i1s)]
    m2s = [jnp.max(r, axis=-1, keepdims=True) for r in rests]
    i2s = [jnp.min(jnp.where(r == m2, lane, LANES), axis=-1, keepdims=True) for r, m2 in zip(rests, m2s)]
    for sl, m1, m2 in zip(subs, m1s, m2s):
        e2 = jnp.exp(m2 - m1)
        wt_ref[sl, :] = jnp.where(lane == 0, 1.0 / (1.0 + e2), jnp.where(lane == 1, e2 / (1.0 + e2), 0.0))
    ohs = [((lane == i1) | (lane == i2)).astype(BF16) for i1, i2 in zip(i1s, i2s)]
    befores = [_dot((tj < ti).astype(BF16), oh) for oh in ohs]
    befores_t = [_dot_tn(oh, (ti < tj).astype(BF16)) for oh in ohs]
    choses_t = [_dot_tn(oh, (ti == tj).astype(BF16)) for oh in ohs]
    run = run_ref[...]
    for k, sl in enumerate(subs):
        pos = run + befores[k]
        r1 = jnp.sum(jnp.where(lane == i1s[k], pos, 0.0), axis=-1, keepdims=True)
        r2 = jnp.sum(jnp.where(lane == i2s[k], pos, 0.0), axis=-1, keepdims=True)
        idx_ref[sl, :] = jnp.where(lane == 0, i1s[k], jnp.where(lane == 1, i2s[k], jnp.where(
            lane == 2, r1.astype(jnp.int32), jnp.where(lane == 3, r2.astype(jnp.int32), 0))))
        base_ref[k] = run.astype(jnp.int32)
        run = run + jnp.sum(ohs[k].astype(F32), axis=0, keepdims=True)
    run_ref[...] = run
    ids = lax.broadcasted_iota(jnp.int32, (rt, LANES), 0).astype(BF16)
    lsts = [jnp.zeros((rt, LANES), F32) for _ in subs]
    for e in range(N_EXPERTS):
        for k in range(len(subs)):
            sel = (befores_t[k][e:e + 1, :] == ti.astype(F32)) & (choses_t[k][e:e + 1, :] > 0.5)
            lsts[k] = jnp.where(lane == e, _dot(sel.astype(BF16), ids), lsts[k])
    for sl, lst in zip(subs, lsts):
        lst_ref[sl, :] = lst.astype(jnp.int32)


def _route(h, mod, ng, router, dims, n_rows):
    n, d = h.shape
    b, s, ctx = dims
    assert ROW_TILE <= 256
    nsub = ROUTE_SUBTILES
    while s % (nsub * ROW_TILE) or (n_rows - b * s) % (nsub * ROW_TILE):
        nsub //= 2
    tm = nsub * ROW_TILE
    nt = n_rows // ROW_TILE
    row = lambda i: (i, 0)
    const = lambda i: (0, 0)
    return pl.pallas_call(
        _route_kernel,
        grid=(n_rows // tm,),
        in_specs=[pl.BlockSpec((tm, d), row),
                  pl.BlockSpec((None, N_MOD, d), lambda i: (jnp.minimum(i * tm // s, b), 0, 0)),
                  pl.BlockSpec((1, d), const),
                  pl.BlockSpec((2, d, LANES), lambda i: (0, 0, 0))],
        out_specs=[pl.BlockSpec((tm, d), row), pl.BlockSpec((tm, LANES), row),
                   pl.BlockSpec((tm, LANES), row), pl.BlockSpec((tm, LANES), row),
                   pl.BlockSpec((nsub, 1, LANES), lambda i: (i, 0, 0))],
        out_shape=[jax.ShapeDtypeStruct((n_rows, d), BF16),
                   jax.ShapeDtypeStruct((n_rows, LANES), jnp.int32),
                   jax.ShapeDtypeStruct((n_rows, LANES), F32),
                   jax.ShapeDtypeStruct((n_rows, LANES), jnp.int32),
                   jax.ShapeDtypeStruct((nt, 1, LANES), jnp.int32)],
        scratch_shapes=[pltpu.VMEM((1, LANES), F32)],
        compiler_params=_params("arbitrary"),
        name="route",
    )(h, mod, ng, router)


def _moe_kernel(te_ref, nu_ref, x_ref, wg_ref, wu_ref, wd_ref, o_ref, acc_ref):
    i = pl.program_id(0)
    f = pl.program_id(1)

    @pl.when((i < nu_ref[0]) & (f == 0))
    def _():
        acc_ref[...] = jnp.zeros_like(acc_ref)

    @pl.when((i >= nu_ref[0]) & (f == 0))
    def _():
        o_ref[...] = jnp.zeros_like(o_ref)

    @pl.when(i < nu_ref[0])
    def _():
        x = x_ref[...]
        w = wg_ref.shape[-1] // MOE_FF_SPLIT
        hids = []
        for c in range(MOE_FF_SPLIT):
            cs = slice(c * w, (c + 1) * w)
            hids.append((_silu(_dot(x, wg_ref[:, cs].astype(BF16))) *
                         _dot(x, wu_ref[:, cs].astype(BF16))).astype(BF16))
        part = None
        for c in range(MOE_FF_SPLIT):
            p = _dot(hids[c], wd_ref[c * w:(c + 1) * w, :].astype(BF16))
            part = p if part is None else part + p
        acc = acc_ref[...] + part
        acc_ref[...] = acc
        o_ref[...] = acc.astype(o_ref.dtype)


def _moe_experts(x_sorted, tile_expert, n_used, wg, wu, wd, layer):
    npad, d = x_sorted.shape
    tm, tf = MOE_TILE, MOE_FF_TILE
    dff = wg.shape[-1]

    def ff(i, f, nu):
        return jnp.where(i < nu[0], f, 0)

    grid_spec = pltpu.PrefetchScalarGridSpec(
        num_scalar_prefetch=2,
        grid=(npad // tm, dff // tf),
        in_specs=[pl.BlockSpec((tm, d), lambda i, f, te, nu: (i, 0)),
                  pl.BlockSpec((None, None, d, tf), lambda i, f, te, nu: (layer, te[i], 0, ff(i, f, nu))),
                  pl.BlockSpec((None, None, d, tf), lambda i, f, te, nu: (layer, te[i], 0, ff(i, f, nu))),
                  pl.BlockSpec((None, None, tf, d), lambda i, f, te, nu: (layer, te[i], ff(i, f, nu), 0))],
        out_specs=pl.BlockSpec((tm, d), lambda i, f, te, nu: (i, 0)),
        scratch_shapes=[pltpu.VMEM((tm, d), F32)],
    )
    return pl.pallas_call(
        _moe_kernel,
        grid_spec=grid_spec,
        out_shape=jax.ShapeDtypeStruct((npad, d), BF16),
        compiler_params=_params("parallel", "arbitrary"),
        name="moe_experts",
    )(tile_expert, n_used, x_sorted, wg, wu, wd)


def _combine_kernel(h_ref, y0_ref, y1_ref, wt_ref, mod_ref, *rest):
    y = wt_ref[:, 0:1] * y0_ref[...] + wt_ref[:, 1:2] * y1_ref[...]
    h = h_ref[...] + mod_ref[5:6, :] * y
    if len(rest) == 2:
        fg_ref, o_ref = rest
        o_ref[...] = _rms(h, fg_ref[...])
    else:
        rest[0][...] = h


def _combine(h, y0, y1, wt, mod, dims, n_rows, final_g=None):
    n, d = h.shape
    b, s, ctx = dims
    tm = ROW_TILE
    row = lambda i: (i, 0)
    in_specs = [pl.BlockSpec((tm, d), row), pl.BlockSpec((tm, d), row), pl.BlockSpec((tm, d), row),
                pl.BlockSpec((tm, LANES), row),
                pl.BlockSpec((None, N_MOD, d), lambda i: (jnp.minimum(i * tm // s, b), 0, 0))]
    args = [h, y0, y1, wt, mod]
    if final_g is not None:
        in_specs.append(pl.BlockSpec((1, d), lambda i: (0, 0)))
        args.append(final_g)
    return pl.pallas_call(
        _combine_kernel,
        grid=(n_rows // tm,),
        in_specs=in_specs,
        out_specs=pl.BlockSpec((tm, d), row),
        out_shape=jax.ShapeDtypeStruct((n_rows if final_g is not None else n, d), F32),
        input_output_aliases={} if final_g is not None else {0: 0},
        compiler_params=_params("parallel"),
        name="moe_combine",
    )(*args)


def _moe(h, mod, ng, router, wg, wu, wd, layer, dims, n_rows, final_g=None, defer_combine=False):
    tm, rt = MOE_TILE, ROW_TILE
    v, idx, wt, lst, base = _route(h, mod, ng, router, dims, n_rows)
    base = base[:, 0, :N_EXPERTS]
    e_sel, pos = idx[:, 0:2], idx[:, 2:4]
    last = e_sel[-rt:]
    counts = base[-1] + jnp.sum(last[:, :, None] == jnp.arange(N_EXPERTS), axis=(0, 1))
    tiles_per = (counts + tm - 1) // tm
    tile_end = jnp.cumsum(tiles_per)
    group_start = (tile_end - tiles_per) * tm
    slot = group_start[e_sel] + pos
    n_tiles = 2 * n_rows // tm + N_EXPERTS
    tile_expert = jnp.minimum(jnp.sum(jnp.arange(n_tiles)[:, None] >= tile_end[None, :], axis=1),
                              N_EXPERTS - 1).astype(jnp.int32)
    n_used = tile_end[-1:].astype(jnp.int32)
    rank = (jnp.arange(n_tiles * tm, dtype=jnp.int32).reshape(n_tiles, tm)
            - group_start[tile_expert][:, None])
    base_te = base.T[tile_expert]
    owns = base_te[:, None, :] <= rank[:, :, None]
    rtile = jnp.sum(owns, axis=-1) - 1
    local = rank - jnp.max(jnp.where(owns, base_te[:, None, :], 0), axis=-1)
    flat = jnp.clip((rtile * rt + local) * N_EXPERTS + tile_expert[:, None], 0, n_rows * N_EXPERTS - 1)
    src = rtile * rt + jnp.take(lst[:, :N_EXPERTS].reshape(-1), flat, mode="clip")
    spread = jnp.arange(n_tiles * tm, dtype=jnp.int32).reshape(n_tiles, tm) % n_rows
    src = jnp.where(rank < counts[tile_expert][:, None], src, spread).reshape(-1)
    x_sorted = jnp.take(v, src, axis=0, mode="clip")
    out_sorted = _moe_experts(x_sorted, tile_expert, n_used, wg, wu, wd, layer)
    y0 = jnp.take(out_sorted, slot[:, 0], axis=0, mode="clip")
    y1 = jnp.take(out_sorted, slot[:, 1], axis=0, mode="clip")
    if defer_combine:
        return h, (y0, y1, wt, mod)
    return _combine(h, y0, y1, wt, mod, dims, n_rows, final_g), None


def _final_kernel(h_ref, g_ref, o_ref):
    o_ref[...] = _rms(h_ref[...], g_ref[...])


def _final_norm(h, g, n_rows):
    n, d = h.shape
    tm = ROW_TILE
    row = lambda i: (i, 0)
    return pl.pallas_call(
        _final_kernel,
        grid=(n_rows // tm,),
        in_specs=[pl.BlockSpec((tm, d), row), pl.BlockSpec((1, d), lambda i: (0, 0))],
        out_specs=pl.BlockSpec((tm, d), row),
        out_shape=jax.ShapeDtypeStruct((n_rows, d), F32),
        compiler_params=_params("parallel"),
        name="final_norm",
    )(h, g)


def _pad_cols(w, width):
    return jnp.pad(w, ((0, 0), (0, width - w.shape[-1])))


def _pack_in(w_in):
    cq = w_in[:, :Q_LORA]
    ckv = w_in[:, Q_LORA:KR_OFF]
    kr = w_in[:, KR_OFF:KR_OFF + QK_ROPE]
    rest = w_in[:, KR_OFF + QK_ROPE:]
    return jnp.concatenate([cq, ckv, _pad_cols(kr, LANES), rest], axis=-1).astype(BF16)


def _pack_q(w_uq):
    w = w_uq.reshape(Q_LORA, N_HEADS, QK_DIM)
    z = jnp.zeros((Q_LORA, N_HEADS, HEAD_PAD - QK_DIM), w.dtype)
    return jnp.concatenate([w, z], axis=-1).reshape(Q_LORA, QKV_W).astype(BF16)


def _pack_kv(w_ukv):
    w = w_ukv.reshape(KV_LORA, N_HEADS, QK_NOPE + V_HEAD)
    z = jnp.zeros((KV_LORA, N_HEADS, HEAD_PAD - QK_NOPE), w.dtype)
    k_all = jnp.concatenate([w[..., :QK_NOPE], z], axis=-1).reshape(KV_LORA, QKV_W)
    v_all = jnp.concatenate([w[..., QK_NOPE:], z], axis=-1).reshape(KV_LORA, QKV_W)
    return jnp.concatenate([k_all, v_all], axis=-1).astype(BF16)


def _rope_tabs(s, tm):
    rows = s // GRID_W
    row = jnp.repeat(jnp.arange(rows, dtype=F32), GRID_W)
    col = jnp.tile(jnp.arange(GRID_W, dtype=F32), rows)
    half = QK_ROPE // 2
    inv = 1.0 / (ROPE_BASE ** (jnp.arange(0, half, 2, dtype=F32) / half))
    ar = row[:, None] * inv[None, :]
    ac = col[:, None] * inv[None, :]
    ang = jnp.concatenate([ar, ar, ac, ac], axis=-1)
    cos = jnp.concatenate([jnp.cos(ang), jnp.ones((tm, QK_ROPE), F32)], axis=0)
    sin = jnp.concatenate([jnp.sin(ang), jnp.zeros((tm, QK_ROPE), F32)], axis=0)
    t = cos.shape[0]
    first = (jnp.arange(QK_ROPE) % (QK_ROPE // 2)) < QK_ROPE // 4
    sin1 = jnp.where(first, -sin, 0.0)
    sin2 = jnp.where(first, 0.0, sin)
    scale = QK_DIM ** -0.5 * LOG2E

    def q_tab(rope_part, nope_val):
        return jnp.concatenate([jnp.full((t, QK_NOPE), nope_val, F32), rope_part * scale,
                                jnp.zeros((t, HEAD_PAD - QK_DIM), F32)], axis=-1)

    return (q_tab(cos, scale), q_tab(sin1, 0.0), q_tab(sin2, 0.0),
            _pad_cols(cos, LANES), _pad_cols(sin1, LANES), _pad_cols(sin2, LANES))


def kernel(x, c, ctx, c_ctx, w_mod, b_mod, norm_g, final_g, ab_w_in, ab_g_cq, ab_g_ckv, ab_w_uq, ab_w_ukv, hgrn_lb_logits, hgrn_g_norm, ab_w_out, ffn_w_gate, ffn_w_up, ffn_w_down, pool_w, pool_b, pool_scale, moe_router, moe_w_gate, moe_w_up, moe_w_down):
    b, s, d = x.shape
    ctx_len = ctx.shape[1]
    depth = w_mod.shape[0]
    dims = (b, s, ctx_len)
    n_lat, n_ctx = b * s, b * ctx_len
    assert d == D_MODEL and s % ROW_TILE == 0 and ctx_len % ROW_TILE == 0
    assert ctx_len % GLA_BLOCK == 0 and s % GLA_BLOCK == 0 and n_lat % ctx_len == 0

    h = (x.reshape(n_lat, d), ctx.reshape(n_ctx, d))
    n_groups = ((b + 1 + SUBLANES - 1) // SUBLANES) * SUBLANES
    cvec = jnp.concatenate([c, c_ctx[None, :], jnp.zeros((n_groups - b - 1, d), F32)], axis=0)
    mod_all = _mod_table(cvec, w_mod, b_mod).reshape(depth, n_groups, N_MOD, d)

    lb_p = jax.nn.softmax(hgrn_lb_logits.astype(F32), axis=0)
    lb_all = jnp.cumsum(lb_p, axis=0) - lb_p[:1]
    tabs = _rope_tabs(s, ROW_TILE)

    pending = None
    for l in range(depth):
        j = l // 2
        even = l % 2 == 0
        ctx_later = any(m % 2 == 0 for m in range(l + 1, depth))
        n_rows = n_lat + n_ctx if ctx_later else n_lat
        mod = mod_all[l]
        ng1, ng2 = norm_g[l, 0][None, :], norm_g[l, 1][None, :]
        if even:
            outs = _inproj(
                h, mod, ng1, _pack_in(ab_w_in[j]), ab_g_cq[j][None, :], ab_g_ckv[j][None, :],
                _pack_q(ab_w_uq[j]), _pack_kv(ab_w_ukv[j]), lb_all[j], tabs, dims, pending)
            (q, k, v, qh, kf, lgf, kb, lgb, vh, hg) = outs[:10]
            if pending is not None:
                h, pending = outs[10], None
            a = _attention_latent(q, k, v, dims)
            if ctx_later:
                a = jnp.concatenate([a, _attention_ctx(q, k, v, dims)], axis=0)
            o_f, o_b = _gla(qh, kf, lgf, kb, lgb, vh, dims)
            h = _mix_ffn(a, o_f, o_b, hg, hgrn_g_norm[j][None, :], ab_w_out[j].astype(BF16), h, mod, ng2,
                         ffn_w_gate[j].astype(BF16), ffn_w_up[j].astype(BF16),
                         ffn_w_down[j].astype(BF16), dims, n_rows)
        else:
            h = _pool(h, mod, ng1, pool_w[j].astype(BF16), pool_b[j], pool_scale[j][None, :], dims, n_rows)
            r_full = _pad_cols(moe_router[j], LANES)
            r_hi = r_full.astype(BF16)
            router = jnp.stack([r_hi, (r_full - r_hi.astype(F32)).astype(BF16)])
            last = l == depth - 1
            defer = not last and n_rows == n_lat + n_ctx
            h, pending = _moe(h, mod, ng2, router, moe_w_gate, moe_w_up, moe_w_down, j, dims, n_rows,
                              final_g[None, :] if last else None, defer)
            if last:
                return h.reshape(b, s, d)
    return _final_norm(h, final_g[None, :], n_lat).reshape(b, s, d)
```

```python
import functools

import jax
import jax.numpy as jnp
from jax import lax
from jax.experimental import pallas as pl
from jax.experimental.pallas import tpu as pltpu

F32 = jnp.float32
BF16 = jnp.bfloat16

D_MODEL = 1024
GRID_W = 64
N_MOD = 6
EPS = 1e-6
N_HEADS = 8
V_HEAD = 64
QK_NOPE = 64
QK_ROPE = 32
QK_DIM = QK_NOPE + QK_ROPE
Q_LORA = 384
KV_LORA = 256
ROPE_BASE = 10000.0
MLA_W = 512
HGRN_W = 512
HGRN_K = 128
N_HGRN_HEADS = 4
POOL_WINDOWS = (2, 4, 8, 16)
POOL_C = 256
N_EXPERTS = 8

LANES = 128
SUBLANES = 8
VMEM_LIMIT_BYTES = 56 * 1024 * 1024

HEAD_PAD = LANES
QKV_W = N_HEADS * HEAD_PAD
KR_OFF = Q_LORA + KV_LORA
H_OFF = KR_OFF + LANES
IN_COLS = H_OFF + 5 * HGRN_W
GLA_CHUNK = 64
GLA_BLOCK = 256
GLA_PHASE_GROUP = 2
ROW_TILE = 256
ROUTE_SUBTILES = 8
MLP_TILE = 512
ATTN_Q_TILE = 512
ATTN_HEAD_GROUP = 4
MOE_TILE = 1024
MOE_FF_TILE = 512
MOE_FF_SPLIT = 2
LOG2E = 1.4426950408889634


def _params(*sem):
    return pltpu.CompilerParams(dimension_semantics=sem, vmem_limit_bytes=VMEM_LIMIT_BYTES)


def _dot(a, b):
    return jnp.dot(a, b, preferred_element_type=F32)


def _dot_nt(a, b):
    return lax.dot_general(a, b, (((1,), (1,)), ((), ())), preferred_element_type=F32)


def _dot_tn(a, b):
    return lax.dot_general(a, b, (((0,), (0,)), ((), ())), preferred_element_type=F32)


def _rms(x, g):
    return x * lax.rsqrt(jnp.mean(x * x, axis=-1, keepdims=True) + EPS) * g


def _silu(x):
    return x * jax.nn.sigmoid(x)


def _norm_mod(h, g, shift, scale):
    return _rms(h, g) * (1.0 + scale) + shift


def _mod_kernel(c_ref, w_ref, b_ref, o_ref):
    c = c_ref[...]
    o_ref[...] = jnp.dot(_silu(c), w_ref[...], preferred_element_type=F32,
                         precision=lax.Precision.HIGHEST) + b_ref[...]


def _mod_table(cvec, w_mod, b_mod):
    depth, d, nd = w_mod.shape
    g = cvec.shape[0]
    tn = 1536
    return pl.pallas_call(
        _mod_kernel,
        grid=(depth, nd // tn),
        in_specs=[pl.BlockSpec((g, d), lambda l, j: (0, 0)),
                  pl.BlockSpec((None, d, tn), lambda l, j: (l, 0, j)),
                  pl.BlockSpec((None, 1, tn), lambda l, j: (l, 0, j))],
        out_specs=pl.BlockSpec((None, g, tn), lambda l, j: (l, 0, j)),
        out_shape=jax.ShapeDtypeStruct((depth, g, nd), F32),
        compiler_params=_params("parallel", "parallel"),
        name="mod_table",
    )(cvec, w_mod, b_mod.reshape(depth, 1, nd))


def _inproj_body(h, mod_ref, ng_ref, win_ref, gcq_ref, gckv_ref, wq_ref, wkv_ref,
                   lb_ref, cq_tab, s1q_tab, s2q_tab, ck_tab, s1k_tab, s2k_tab,
                   q_ref, k_ref, v_ref, qh_ref, kf_ref, lgf_ref, kb_ref, lgb_ref, vh_ref, hg_ref):
    half_rot = QK_ROPE // 4

    def rope(x, c_tab, s1_tab, s2_tab):
        return (x * c_tab[...] + pltpu.roll(x, LANES - half_rot, 1) * s1_tab[...]
                + pltpu.roll(x, half_rot, 1) * s2_tab[...])

    u = _norm_mod(h, ng_ref[...], mod_ref[0:1, :], mod_ref[1:2, :]).astype(BF16)
    p = _dot(u, win_ref[...])
    cqn = _rms(p[:, 0:Q_LORA], gcq_ref[...]).astype(BF16)
    a = _dot(cqn, wq_ref[...])
    q_ref[...] = jnp.concatenate(
        [rope(a[:, hd * HEAD_PAD:(hd + 1) * HEAD_PAD], cq_tab, s1q_tab, s2q_tab) for hd in range(N_HEADS)],
        axis=-1).astype(BF16)
    ckvn = _rms(p[:, Q_LORA:KR_OFF], gckv_ref[...]).astype(BF16)
    kv = _dot(ckvn, wkv_ref[...])
    kr = pltpu.roll(rope(p[:, KR_OFF:H_OFF], ck_tab, s1k_tab, s2k_tab), QK_NOPE, 1)
    k_ref[...] = jnp.concatenate(
        [kv[:, hd * HEAD_PAD:(hd + 1) * HEAD_PAD] + kr for hd in range(N_HEADS)], axis=-1).astype(BF16)
    lane = lax.broadcasted_iota(jnp.int32, (1, QKV_W), 1)
    ones_col = (lane % HEAD_PAD == V_HEAD).astype(F32)
    v_ref[...] = (kv[:, QKV_W:] + ones_col).astype(BF16)
    o = H_OFF
    qh_ref[...] = _silu(p[:, o:o + HGRN_W])
    kf = (1.0 - lb_ref[0:1, :]) * jax.nn.sigmoid(-p[:, o + HGRN_W:o + 2 * HGRN_W])
    kf_ref[...] = kf
    lgf_ref[...] = jnp.log1p(-kf)
    kb = (1.0 - lb_ref[1:2, :]) * jax.nn.sigmoid(-p[:, o + 2 * HGRN_W:o + 3 * HGRN_W])
    kb_ref[...] = kb
    lgb_ref[...] = jnp.log1p(-kb)
    vh_ref[...] = p[:, o + 3 * HGRN_W:o + 4 * HGRN_W]
    hg_ref[...] = p[:, o + 4 * HGRN_W:o + 5 * HGRN_W]


def _rows_from(refs, lat_tiles):
    if lat_tiles is None:
        return refs[0][...], refs[1:]
    return jnp.where(pl.program_id(0) < lat_tiles, refs[0][...], refs[1][...]), refs[2:]


def _inproj_kernel(*refs, lat_tiles, pending):
    h, rest = _rows_from(refs, lat_tiles)
    if pending:
        y0_ref, y1_ref, wt_ref, pmod_ref, *rest = rest
        *rest, hnew_ref = rest
        h = h + pmod_ref[5:6, :] * (wt_ref[:, 0:1] * y0_ref[...] + wt_ref[:, 1:2] * y1_ref[...])
        hnew_ref[...] = h
    _inproj_body(h, *rest)


def _split_rows(h, tm):
    if not isinstance(h, tuple):
        return [pl.BlockSpec((tm, h.shape[1]), lambda i: (i, 0))], [h], None
    x2, c2 = h
    lat_tiles = x2.shape[0] // tm
    specs = [pl.BlockSpec((tm, x2.shape[1]), lambda i: (jnp.minimum(i, lat_tiles - 1), 0)),
             pl.BlockSpec((tm, c2.shape[1]), lambda i: (jnp.maximum(i - lat_tiles, 0), 0))]
    return specs, [x2, c2], lat_tiles


def _inproj(h, mod, ng, win, gcq, gckv, wq, wkv, lb, tabs, dims, pending=None):
    b, s, ctx = dims
    n, d = b * (s + ctx), D_MODEL
    tm = ROW_TILE
    h_specs, h_args, lat_tiles = _split_rows(h, tm)
    extra_specs, extra_args, extra_out_specs, extra_out_shape, aliases = [], [], [], [], {}
    if pending is not None:
        extra_specs = [pl.BlockSpec((tm, d), lambda i: (i, 0)), pl.BlockSpec((tm, d), lambda i: (i, 1)),
                       pl.BlockSpec((tm, LANES), lambda i: (i, 0)),
                       pl.BlockSpec((None, N_MOD, d), lambda i: (jnp.minimum(i * tm // s, b), 0, 0))]
        extra_args = list(pending)
        extra_out_specs = [pl.BlockSpec((tm, d), lambda i: (i, 0))]
        extra_out_shape = [jax.ShapeDtypeStruct((n, d), F32)]
        aliases = {0: 10}
    n_lat_t = b * s // tm
    pos_blocks = s // tm

    def grp(i):
        return jnp.minimum(i * tm // s, b)

    def pos(i):
        return jnp.where(i < n_lat_t, i % pos_blocks, pos_blocks)

    row = lambda i: (i, 0)
    const = lambda i: (0, 0)
    tab_spec = pl.BlockSpec((tm, LANES), lambda i: (pos(i), 0))
    wide = jax.ShapeDtypeStruct((n, QKV_W), BF16)
    hg = jax.ShapeDtypeStruct((n, HGRN_W), F32)
    return pl.pallas_call(
        functools.partial(_inproj_kernel, lat_tiles=lat_tiles, pending=pending is not None),
        grid=(n // tm,),
        in_specs=h_specs + extra_specs + [
                  pl.BlockSpec((None, N_MOD, d), lambda i: (grp(i), 0, 0)),
                  pl.BlockSpec((1, d), const),
                  pl.BlockSpec(win.shape, const),
                  pl.BlockSpec((1, Q_LORA), const),
                  pl.BlockSpec((1, KV_LORA), const),
                  pl.BlockSpec(wq.shape, const),
                  pl.BlockSpec(wkv.shape, const),
                  pl.BlockSpec((2, HGRN_W), const),
                  tab_spec, tab_spec, tab_spec, tab_spec, tab_spec, tab_spec],
        out_specs=([pl.BlockSpec((tm, QKV_W), row)] * 3 + [pl.BlockSpec((tm, HGRN_W), row)] * 7
                   + extra_out_specs),
        out_shape=[wide] * 3 + [hg] * 7 + extra_out_shape,
        input_output_aliases=aliases,
        compiler_params=_params("parallel"),
        name="inproj",
    )(*h_args, *extra_args, mod, ng, win, gcq, gckv, wq, wkv, lb, *tabs)


def _attn_kernel(*refs, n_kv):
    q_ref = refs[0]
    k_refs = refs[1:1 + 2 * n_kv:2]
    v_refs = refs[2:2 + 2 * n_kv:2]
    o_ref = refs[-1]
    for h0 in range(0, N_HEADS, ATTN_HEAD_GROUP):
        heads = range(h0, h0 + ATTN_HEAD_GROUP)
        sls = {hd: slice(hd * HEAD_PAD, (hd + 1) * HEAD_PAD) for hd in heads}
        scores = {hd: [_dot_nt(q_ref[:, sls[hd]], k_ref[:, sls[hd]]) for k_ref in k_refs] for hd in heads}
        ms = {hd: functools.reduce(jnp.maximum, [jnp.max(sc, axis=-1, keepdims=True) for sc in scores[hd]])
              for hd in heads}
        for hd in heads:
            acc = None
            for sc, v_ref in zip(scores[hd], v_refs):
                part = _dot(jnp.exp2(sc - ms[hd]).astype(BF16), v_ref[:, sls[hd]])
                acc = part if acc is None else acc + part
            o = acc[:, :V_HEAD] / acc[:, V_HEAD:V_HEAD + 1]
            o_ref[:, hd * V_HEAD:(hd + 1) * V_HEAD] = o.astype(o_ref.dtype)


def _attention_latent(q, k, v, dims):
    b, s, ctx = dims
    tq = min(ATTN_Q_TILE, s)
    nq = s // tq
    lat = lambda bi, j: (bi, 0)
    cx = lambda bi, j: (b * s // ctx + bi, 0)
    return pl.pallas_call(
        functools.partial(_attn_kernel, n_kv=2),
        grid=(b, nq),
        in_specs=[pl.BlockSpec((tq, QKV_W), lambda bi, j: (bi * nq + j, 0)),
                  pl.BlockSpec((s, QKV_W), lat), pl.BlockSpec((s, QKV_W), lat),
                  pl.BlockSpec((ctx, QKV_W), cx), pl.BlockSpec((ctx, QKV_W), cx)],
        out_specs=pl.BlockSpec((tq, MLA_W), lambda bi, j: (bi * nq + j, 0)),
        out_shape=jax.ShapeDtypeStruct((b * s, MLA_W), BF16),
        compiler_params=_params("parallel", "arbitrary"),
        name="attn_latent",
    )(q, k, v, k, v)


def _attention_ctx(q, k, v, dims):
    b, s, ctx = dims
    cx = lambda bi: (b * s // ctx + bi, 0)
    return pl.pallas_call(
        functools.partial(_attn_kernel, n_kv=1),
        grid=(b,),
        in_specs=[pl.BlockSpec((ctx, QKV_W), cx), pl.BlockSpec((ctx, QKV_W), cx),
                  pl.BlockSpec((ctx, QKV_W), cx)],
        out_specs=pl.BlockSpec((ctx, MLA_W), lambda bi: (bi, 0)),
        out_shape=jax.ShapeDtypeStruct((b * ctx, MLA_W), BF16),
        compiler_params=_params("parallel"),
        name="attn_ctx",
    )(q, k, v)


def _bcast_block_row(g, bs, row):
    if bs == GLA_CHUNK:
        return g[row:row + 1, :]
    g3 = g.reshape(GLA_CHUNK // bs, bs, g.shape[-1])
    return jnp.broadcast_to(g3[:, row:row + 1, :], g3.shape).reshape(g.shape)


def _gla_consts(reverse):
    c = GLA_CHUNK
    t_idx = lax.broadcasted_iota(jnp.int32, (c, c), 0)
    s_idx = lax.broadcasted_iota(jnp.int32, (c, c), 1)
    pt = (c - 1 - t_idx) if reverse else t_idx
    ps = (c - 1 - s_idx) if reverse else s_idx
    tri = (ps <= pt).astype(BF16)
    levels = []
    for half in (32, 16, 8):
        bs = 2 * half
        mask = (t_idx // bs == s_idx // bs) & (pt % bs >= half) & (ps % bs < half)
        levels.append((bs, half if reverse else half - 1, False, mask))
    dmask = (t_idx // SUBLANES == s_idx // SUBLANES) & (ps <= pt)
    levels.append((SUBLANES, 4 if reverse else 3, True, dmask))
    end_row = 0 if reverse else c - 1
    return tri, levels, end_row


def _gla_group(dirs, consts, states, steps):
    c = GLA_CHUNK
    n_chunks = GLA_BLOCK // c
    units = []
    for step in steps:
        for di, (q_ref, k_ref, lg_ref, v_ref, o_ref, _, reverse) in enumerate(dirs):
            tri, levels, end_row = consts[di]
            ci = n_chunks - 1 - step if reverse else step
            rows = slice(ci * c, (ci + 1) * c)
            lg = lg_ref[rows, :]
            hi = lg.astype(BF16)
            r1 = lg - hi.astype(F32)
            mid = r1.astype(BF16)
            lo = (r1 - mid.astype(F32)).astype(BF16)
            gc = (_dot(tri, hi) + _dot(tri, mid) + _dot(tri, lo)) * LOG2E
            for hd in range(N_HGRN_HEADS):
                hs = slice(hd * HGRN_K, (hd + 1) * HGRN_K)
                units.append(dict(di=di, hd=hd, rows=rows, hs=hs, g=gc[:, hs], o_ref=o_ref,
                                  qh=q_ref[rows, hs], kh=k_ref[rows, hs],
                                  vh=v_ref[rows, hs].astype(BF16), levels=levels, end_row=end_row,
                                  attn=jnp.zeros((c, c), F32)))
    for li in range(len(units[0]["levels"])):
        for u in units:
            bs, row, diag, mask = u["levels"][li]
            g, qh, kh = u["g"], u["qh"], u["kh"]
            d = g - _bcast_block_row(g, bs, row)
            if diag:
                qf = (qh * jnp.exp2(d)).astype(BF16)
                kf = (kh * jnp.exp2(-d)).astype(BF16)
            else:
                e = jnp.exp2(-jnp.abs(d))
                qf = (qh * e).astype(BF16)
                kf = (kh * e).astype(BF16)
            u["attn"] = jnp.where(mask, _dot_nt(qf, kf), u["attn"])
    for u in units:
        g, qh, kh, vh = u["g"], u["qh"], u["kh"], u["vh"]
        st = states[u["di"]][u["hd"]]
        g_end = g[u["end_row"]:u["end_row"] + 1, :]
        qe = (qh * jnp.exp2(g)).astype(BF16)
        ke = (kh * jnp.exp2(g_end - g)).astype(BF16)
        o = _dot_nt(qe, st.astype(BF16)) + _dot(u["attn"].astype(BF16), vh)
        u["o_ref"][u["rows"], u["hs"]] = o
        states[u["di"]][u["hd"]] = st * jnp.exp2(g_end) + _dot_tn(vh, ke)


def _gla_kernel(qf_ref, kf_ref, lgf_ref, vf_ref, qb_ref, kb_ref, lgb_ref, vb_ref,
                of_ref, ob_ref, stf_ref, stb_ref):
    @pl.when(pl.program_id(1) == 0)
    def _():
        stf_ref[...] = jnp.zeros_like(stf_ref)
        stb_ref[...] = jnp.zeros_like(stb_ref)

    dirs = [(qf_ref, kf_ref, lgf_ref, vf_ref, of_ref, stf_ref, False),
            (qb_ref, kb_ref, lgb_ref, vb_ref, ob_ref, stb_ref, True)]
    consts = [_gla_consts(False), _gla_consts(True)]
    states = [[d[5][hd] for hd in range(N_HGRN_HEADS)] for d in dirs]
    for step0 in range(0, GLA_BLOCK // GLA_CHUNK, GLA_PHASE_GROUP):
        _gla_group(dirs, consts, states, range(step0, step0 + GLA_PHASE_GROUP))
    for di, d in enumerate(dirs):
        for hd in range(N_HGRN_HEADS):
            d[5][hd] = states[di][hd]


def _gla(qh, kf, lgf, kb, lgb, vh, dims):
    n = qh.shape[0]
    b, s, ctx = dims
    blk = GLA_BLOCK
    ncb, nsb, nlat = ctx // blk, s // blk, b * s // blk

    def fwd(bi, j):
        return (jnp.where(j < ncb, nlat + bi * ncb + j, bi * nsb + (j - ncb)), 0)

    def bwd(bi, j):
        return (jnp.where(j < ncb, nlat + bi * ncb + (ncb - 1 - j), bi * nsb + (nsb - 1 - (j - ncb))), 0)

    fs = pl.BlockSpec((blk, HGRN_W), fwd)
    bs = pl.BlockSpec((blk, HGRN_W), bwd)
    out = jax.ShapeDtypeStruct((n, HGRN_W), F32)
    return pl.pallas_call(
        _gla_kernel,
        grid=(b, ncb + nsb),
        in_specs=[fs, fs, fs, fs, bs, bs, bs, bs],
        out_specs=[fs, bs],
        out_shape=[out, out],
        scratch_shapes=[pltpu.VMEM((N_HGRN_HEADS, HGRN_K, HGRN_K), F32)] * 2,
        compiler_params=_params("parallel", "arbitrary"),
        name="gla",
    )(qh, kf, lgf, vh, qh, kb, lgb, vh)


def _mix_ffn_kernel(*refs, lat_tiles):
    h, (a_ref, of_ref, ob_ref, hg_ref, gn_ref, wo_ref, mod_ref, ng_ref, wg_ref, wu_ref, wd_ref,
        o_ref) = _rows_from(refs, lat_tiles)
    rows = h.shape[0] // 2
    halves = [slice(r * rows, (r + 1) * rows) for r in range(2)]
    mixed = []
    for rs in halves:
        o = of_ref[rs, :] + ob_ref[rs, :]
        gate = _silu(hg_ref[rs, :])
        ys = []
        for hd in range(N_HGRN_HEADS):
            hs = slice(hd * HGRN_K, (hd + 1) * HGRN_K)
            ys.append(_rms(o[:, hs], gn_ref[...]) * gate[:, hs])
        y = jnp.concatenate(ys, axis=-1).astype(BF16)
        mix = _dot(a_ref[rs, :], wo_ref[0:MLA_W, :]) + _dot(y, wo_ref[MLA_W:, :])
        mixed.append(h[rs, :] + mod_ref[2:3, :] * mix)
    for rs, h1 in zip(halves, mixed):
        v = _norm_mod(h1, ng_ref[...], mod_ref[3:4, :], mod_ref[4:5, :]).astype(BF16)
        hid = (_silu(_dot(v, wg_ref[...])) * _dot(v, wu_ref[...])).astype(BF16)
        o_ref[rs, :] = h1 + mod_ref[5:6, :] * _dot(hid, wd_ref[...])


def _mix_ffn(a, of, ob, hg, gn, wo, h, mod, ng, wg, wu, wd, dims, n_rows):
    b, s, ctx = dims
    n, d = b * (s + ctx), D_MODEL
    tm = MLP_TILE
    row = lambda i: (i, 0)
    const = lambda i: (0, 0)
    half = pl.BlockSpec((tm, HGRN_W), row)
    resident = lambda w: pl.BlockSpec(w.shape, const, pipeline_mode=pl.Buffered(1))
    h_specs, h_args, lat_tiles = _split_rows(h, tm)
    return pl.pallas_call(
        functools.partial(_mix_ffn_kernel, lat_tiles=lat_tiles),
        grid=(n_rows // tm,),
        in_specs=h_specs + [pl.BlockSpec((tm, MLA_W), row), half, half, half,
                            pl.BlockSpec((1, HGRN_K), const), resident(wo),
                            pl.BlockSpec((None, N_MOD, d), lambda i: (jnp.minimum(i * tm // s, b), 0, 0)),
                            pl.BlockSpec((1, d), const), resident(wg), resident(wu), resident(wd)],
        out_specs=pl.BlockSpec((tm, d), row),
        out_shape=jax.ShapeDtypeStruct((n, d), F32),
        input_output_aliases={} if lat_tiles is not None else {0: 0},
        compiler_params=_params("parallel"),
        name="mix_ffn",
    )(*h_args, a, of, ob, hg, gn, wo, mod, ng, wg, wu, wd)


def _pool_kernel(h_ref, prev_ref, next_ref, mod_ref, ng_ref, w_ref, b_ref, sc_ref, o_ref,
                 *, tm, seq_tiles_lat, n_lat_tiles, seq_tiles_ctx):
    i = pl.program_id(0)
    halo = SUBLANES
    shift, scale = mod_ref[0:1, :], mod_ref[1:2, :]
    in_lat = i < n_lat_tiles
    j = jnp.where(in_lat, i % seq_tiles_lat, (i - n_lat_tiles) % seq_tiles_ctx)
    n_seq_tiles = jnp.where(in_lat, seq_tiles_lat, seq_tiles_ctx)
    h = h_ref[...]
    u = _norm_mod(h, ng_ref[...], shift, scale)
    up = _norm_mod(prev_ref[...], ng_ref[...], shift, scale)
    un = _norm_mod(next_ref[...], ng_ref[...], shift, scale)
    ext = jnp.concatenate([jnp.where(j > 0, up, 0.0), u, jnp.where(j < n_seq_tiles - 1, un, 0.0)], axis=0)
    ext_hi = ext.astype(BF16)
    ext_lo = (ext - ext_hi.astype(F32)).astype(BF16)
    off = (lax.broadcasted_iota(jnp.int32, (tm, tm + 2 * halo), 1)
           - lax.broadcasted_iota(jnp.int32, (tm, tm + 2 * halo), 0) - halo)
    pos = j * tm + lax.broadcasted_iota(jnp.int32, (tm, 1), 0)
    t_len = n_seq_tiles * tm
    css = [slice(g * POOL_C, (g + 1) * POOL_C) for g in range(len(POOL_WINDOWS))]
    bands = [((off >= -(win // 2)) & (off < win // 2)).astype(BF16) for win in POOL_WINDOWS]
    accs = [_dot(bd, ext_hi[:, cs]) + _dot(bd, ext_lo[:, cs]) for bd, cs in zip(bands, css)]
    pooled = []
    for win, cs, acc in zip(POOL_WINDOWS, css, accs):
        lo = jnp.maximum(pos - win // 2, 0)
        hi = jnp.minimum(pos - win // 2 + win, t_len)
        cnt = (hi - lo).astype(F32)
        pooled.append((acc / cnt - u[:, cs]).astype(BF16))
    outs = [_dot(pl_g, w_ref[g]) + b_ref[g:g + 1, :] for g, pl_g in enumerate(pooled)]
    y = jnp.concatenate(outs, axis=-1) * sc_ref[...]
    o_ref[...] = h + mod_ref[2:3, :] * y


def _pool(h, mod, ng, w, bias, scale, dims, n_rows):
    n, d = h.shape
    b, s, ctx = dims
    tm = ROW_TILE
    hb = tm // SUBLANES
    n_blk8 = n // SUBLANES
    row = lambda i: (i, 0)
    const = lambda i: (0, 0)
    kern = functools.partial(_pool_kernel, tm=tm, seq_tiles_lat=s // tm, n_lat_tiles=b * s // tm,
                             seq_tiles_ctx=ctx // tm)
    return pl.pallas_call(
        kern,
        grid=(n_rows // tm,),
        in_specs=[pl.BlockSpec((tm, d), row),
                  pl.BlockSpec((SUBLANES, d), lambda i: (jnp.maximum(i * hb - 1, 0), 0)),
                  pl.BlockSpec((SUBLANES, d), lambda i: (jnp.minimum((i + 1) * hb, n_blk8 - 1), 0)),
                  pl.BlockSpec((None, N_MOD, d), lambda i: (jnp.minimum(i * tm // s, b), 0, 0)),
                  pl.BlockSpec((1, d), const),
                  pl.BlockSpec(w.shape, lambda i: (0, 0, 0)),
                  pl.BlockSpec(bias.shape, const),
                  pl.BlockSpec((1, d), const)],
        out_specs=pl.BlockSpec((tm, d), row),
        out_shape=jax.ShapeDtypeStruct((n_rows, d), F32),
        compiler_params=_params("parallel"),
        name="pool",
    )(h, h, h, mod, ng, w, bias, scale)


def _route_kernel(h_ref, mod_ref, ng_ref, r_ref, v_ref, idx_ref, wt_ref, lst_ref, base_ref, run_ref):
    rt = ROW_TILE
    subs = [slice(k * rt, (k + 1) * rt) for k in range(h_ref.shape[0] // rt)]

    @pl.when(pl.program_id(0) == 0)
    def _():
        run_ref[...] = jnp.zeros_like(run_ref)

    lane = lax.broadcasted_iota(jnp.int32, (rt, LANES), 1)
    ti = lax.broadcasted_iota(jnp.int32, (rt, rt), 0)
    tj = lax.broadcasted_iota(jnp.int32, (rt, rt), 1)
    neg = jnp.float32(-jnp.inf)
    vs = [_norm_mod(h_ref[sl, :], ng_ref[...], mod_ref[3:4, :], mod_ref[4:5, :]) for sl in subs]
    v_his = [v.astype(BF16) for v in vs]
    for sl, v_hi in zip(subs, v_his):
        v_ref[sl, :] = v_hi
    logits = []
    for v, v_hi in zip(vs, v_his):
        v_lo = (v - v_hi.astype(F32)).astype(BF16)
        lg = _dot(v_hi, r_ref[0]) + (_dot(v_lo, r_ref[0]) + _dot(v_hi, r_ref[1]))
        logits.append(jnp.where(lane < N_EXPERTS, lg, neg))
    m1s = [jnp.max(lg, axis=-1, keepdims=True) for lg in logits]
    i1s = [jnp.min(jnp.where(lg == m1, lane, LANES), axis=-1, keepdims=True) for lg, m1 in zip(logits, m1s)]
    rests = [jnp.where(lane == i1, neg, lg) for lg, i1 in zip(logits, i1s)]
    m2s = [jnp.max(r, axis=-1, keepdims=True) for r in rests]
    i2s = [jnp.min(jnp.where(r == m2, lane, LANES), axis=-1, keepdims=True) for r, m2 in zip(rests, m2s)]
    for sl, m1, m2 in zip(subs, m1s, m2s):
        e2 = jnp.exp(m2 - m1)
        wt_ref[sl, :] = jnp.where(lane == 0, 1.0 / (1.0 + e2), jnp.where(lane == 1, e2 / (1.0 + e2), 0.0))
    ohs = [((lane == i1) | (lane == i2)).astype(BF16) for i1, i2 in zip(i1s, i2s)]
    befores = [_dot((tj < ti).astype(BF16), oh) for oh in ohs]
    befores_t = [_dot_tn(oh, (ti < tj).astype(BF16)) for oh in ohs]
    choses_t = [_dot_tn(oh, (ti == tj).astype(BF16)) for oh in ohs]
    run = run_ref[...]
    for k, sl in enumerate(subs):
        pos = run + befores[k]
        r1 = jnp.sum(jnp.where(lane == i1s[k], pos, 0.0), axis=-1, keepdims=True)
        r2 = jnp.sum(jnp.where(lane == i2s[k], pos, 0.0), axis=-1, keepdims=True)
        idx_ref[sl, :] = jnp.where(lane == 0, i1s[k], jnp.where(lane == 1, i2s[k], jnp.where(
            lane == 2, r1.astype(jnp.int32), jnp.where(lane == 3, r2.astype(jnp.int32), 0))))
        base_ref[k] = run.astype(jnp.int32)
        run = run + jnp.sum(ohs[k].astype(F32), axis=0, keepdims=True)
    run_ref[...] = run
    ids = lax.broadcasted_iota(jnp.int32, (rt, LANES), 0).astype(BF16)
    lsts = [jnp.zeros((rt, LANES), F32) for _ in subs]
    for e in range(N_EXPERTS):
        for k in range(len(subs)):
            sel = (befores_t[k][e:e + 1, :] == ti.astype(F32)) & (choses_t[k][e:e + 1, :] > 0.5)
            lsts[k] = jnp.where(lane == e, _dot(sel.astype(BF16), ids), lsts[k])
    for sl, lst in zip(subs, lsts):
        lst_ref[sl, :] = lst.astype(jnp.int32)


def _route(h, mod, ng, router, dims, n_rows):
    n, d = h.shape
    b, s, ctx = dims
    assert ROW_TILE <= 256
    nsub = ROUTE_SUBTILES
    while s % (nsub * ROW_TILE) or (n_rows - b * s) % (nsub * ROW_TILE):
        nsub //= 2
    tm = nsub * ROW_TILE
    nt = n_rows // ROW_TILE
    row = lambda i: (i, 0)
    const = lambda i: (0, 0)
    return pl.pallas_call(
        _route_kernel,
        grid=(n_rows // tm,),
        in_specs=[pl.BlockSpec((tm, d), row),
                  pl.BlockSpec((None, N_MOD, d), lambda i: (jnp.minimum(i * tm // s, b), 0, 0)),
                  pl.BlockSpec((1, d), const),
                  pl.BlockSpec((2, d, LANES), lambda i: (0, 0, 0))],
        out_specs=[pl.BlockSpec((tm, d), row), pl.BlockSpec((tm, LANES), row),
                   pl.BlockSpec((tm, LANES), row), pl.BlockSpec((tm, LANES), row),
                   pl.BlockSpec((nsub, 1, LANES), lambda i: (i, 0, 0))],
        out_shape=[jax.ShapeDtypeStruct((n_rows, d), BF16),
                   jax.ShapeDtypeStruct((n_rows, LANES), jnp.int32),
                   jax.ShapeDtypeStruct((n_rows, LANES), F32),
                   jax.ShapeDtypeStruct((n_rows, LANES), jnp.int32),
                   jax.ShapeDtypeStruct((nt, 1, LANES), jnp.int32)],
        scratch_shapes=[pltpu.VMEM((1, LANES), F32)],
        compiler_params=_params("arbitrary"),
        name="route",
    )(h, mod, ng, router)


def _moe_kernel(te_ref, nu_ref, x_ref, wg_ref, wu_ref, wd_ref, o_ref, acc_ref):
    i = pl.program_id(0)
    f = pl.program_id(1)

    @pl.when((i < nu_ref[0]) & (f == 0))
    def _():
        acc_ref[...] = jnp.zeros_like(acc_ref)

    @pl.when((i >= nu_ref[0]) & (f == 0))
    def _():
        o_ref[...] = jnp.zeros_like(o_ref)

    @pl.when(i < nu_ref[0])
    def _():
        x = x_ref[...]
        w = wg_ref.shape[-1] // MOE_FF_SPLIT
        hids = []
        for c in range(MOE_FF_SPLIT):
            cs = slice(c * w, (c + 1) * w)
            hids.append((_silu(_dot(x, wg_ref[:, cs].astype(BF16))) *
                         _dot(x, wu_ref[:, cs].astype(BF16))).astype(BF16))
        part = None
        for c in range(MOE_FF_SPLIT):
            p = _dot(hids[c], wd_ref[c * w:(c + 1) * w, :].astype(BF16))
            part = p if part is None else part + p
        acc = acc_ref[...] + part
        acc_ref[...] = acc
        o_ref[...] = acc.astype(o_ref.dtype)


def _moe_experts(x_sorted, tile_expert, n_used, wg, wu, wd, layer):
    npad, d = x_sorted.shape
    tm, tf = MOE_TILE, MOE_FF_TILE
    dff = wg.shape[-1]

    def ff(i, f, nu):
        return jnp.where(i < nu[0], f, 0)

    grid_spec = pltpu.PrefetchScalarGridSpec(
        num_scalar_prefetch=2,
        grid=(npad // tm, dff // tf),
        in_specs=[pl.BlockSpec((tm, d), lambda i, f, te, nu: (i, 0)),
                  pl.BlockSpec((None, None, d, tf), lambda i, f, te, nu: (layer, te[i], 0, ff(i, f, nu))),
                  pl.BlockSpec((None, None, d, tf), lambda i, f, te, nu: (layer, te[i], 0, ff(i, f, nu))),
                  pl.BlockSpec((None, None, tf, d), lambda i, f, te, nu: (layer, te[i], ff(i, f, nu), 0))],
        out_specs=pl.BlockSpec((tm, d), lambda i, f, te, nu: (i, 0)),
        scratch_shapes=[pltpu.VMEM((tm, d), F32)],
    )
    return pl.pallas_call(
        _moe_kernel,
        grid_spec=grid_spec,
        out_shape=jax.ShapeDtypeStruct((npad, d), BF16),
        compiler_params=_params("parallel", "arbitrary"),
        name="moe_experts",
    )(tile_expert, n_used, x_sorted, wg, wu, wd)


def _combine_kernel(h_ref, y0_ref, y1_ref, wt_ref, mod_ref, *rest):
    y = wt_ref[:, 0:1] * y0_ref[...] + wt_ref[:, 1:2] * y1_ref[...]
    h = h_ref[...] + mod_ref[5:6, :] * y
    if len(rest) == 2:
        fg_ref, o_ref = rest
        o_ref[...] = _rms(h, fg_ref[...])
    else:
        rest[0][...] = h


def _combine(h, y0, y1, wt, mod, dims, n_rows, final_g=None):
    n, d = h.shape
    b, s, ctx = dims
    tm = ROW_TILE
    row = lambda i: (i, 0)
    in_specs = [pl.BlockSpec((tm, d), row), pl.BlockSpec((tm, d), row), pl.BlockSpec((tm, d), lambda i: (i, 1)),
                pl.BlockSpec((tm, LANES), row),
                pl.BlockSpec((None, N_MOD, d), lambda i: (jnp.minimum(i * tm // s, b), 0, 0))]
    args = [h, y0, y1, wt, mod]
    if final_g is not None:
        in_specs.append(pl.BlockSpec((1, d), lambda i: (0, 0)))
        args.append(final_g)
    return pl.pallas_call(
        _combine_kernel,
        grid=(n_rows // tm,),
        in_specs=in_specs,
        out_specs=pl.BlockSpec((tm, d), row),
        out_shape=jax.ShapeDtypeStruct((n_rows if final_g is not None else n, d), F32),
        input_output_aliases={} if final_g is not None else {0: 0},
        compiler_params=_params("parallel"),
        name="moe_combine",
    )(*args)


def _moe(h, mod, ng, router, wg, wu, wd, layer, dims, n_rows, final_g=None, defer_combine=False):
    tm, rt = MOE_TILE, ROW_TILE
    v, idx, wt, lst, base = _route(h, mod, ng, router, dims, n_rows)
    base = base[:, 0, :N_EXPERTS]
    e_sel, pos = idx[:, 0:2], idx[:, 2:4]
    last = e_sel[-rt:]
    counts = base[-1] + jnp.sum(last[:, :, None] == jnp.arange(N_EXPERTS), axis=(0, 1))
    tiles_per = (counts + tm - 1) // tm
    tile_end = jnp.cumsum(tiles_per)
    group_start = (tile_end - tiles_per) * tm
    slot = group_start[e_sel] + pos
    n_tiles = 2 * n_rows // tm + N_EXPERTS
    tile_expert = jnp.minimum(jnp.sum(jnp.arange(n_tiles)[:, None] >= tile_end[None, :], axis=1),
                              N_EXPERTS - 1).astype(jnp.int32)
    n_used = tile_end[-1:].astype(jnp.int32)
    rank = (jnp.arange(n_tiles * tm, dtype=jnp.int32).reshape(n_tiles, tm)
            - group_start[tile_expert][:, None])
    base_te = base.T[tile_expert]
    owns = base_te[:, None, :] <= rank[:, :, None]
    rtile = jnp.sum(owns, axis=-1) - 1
    local = rank - jnp.max(jnp.where(owns, base_te[:, None, :], 0), axis=-1)
    flat = jnp.clip((rtile * rt + local) * N_EXPERTS + tile_expert[:, None], 0, n_rows * N_EXPERTS - 1)
    src = rtile * rt + jnp.take(lst[:, :N_EXPERTS].reshape(-1), flat, mode="clip")
    spread = jnp.arange(n_tiles * tm, dtype=jnp.int32).reshape(n_tiles, tm) % n_rows
    src = jnp.where(rank < counts[tile_expert][:, None], src, spread).reshape(-1)
    x_sorted = jnp.take(v, src, axis=0, mode="clip")
    out_sorted = _moe_experts(x_sorted, tile_expert, n_used, wg, wu, wd, layer)
    y0 = y1 = jnp.take(out_sorted, slot.reshape(-1), axis=0, mode="clip").reshape(n_rows, -1)
    if defer_combine:
        return h, (y0, y1, wt, mod)
    return _combine(h, y0, y1, wt, mod, dims, n_rows, final_g), None


def _final_kernel(h_ref, g_ref, o_ref):
    o_ref[...] = _rms(h_ref[...], g_ref[...])


def _final_norm(h, g, n_rows):
    n, d = h.shape
    tm = ROW_TILE
    row = lambda i: (i, 0)
    return pl.pallas_call(
        _final_kernel,
        grid=(n_rows // tm,),
        in_specs=[pl.BlockSpec((tm, d), row), pl.BlockSpec((1, d), lambda i: (0, 0))],
        out_specs=pl.BlockSpec((tm, d), row),
        out_shape=jax.ShapeDtypeStruct((n_rows, d), F32),
        compiler_params=_params("parallel"),
        name="final_norm",
    )(h, g)


def _pad_cols(w, width):
    return jnp.pad(w, ((0, 0), (0, width - w.shape[-1])))


def _pack_in(w_in):
    cq = w_in[:, :Q_LORA]
    ckv = w_in[:, Q_LORA:KR_OFF]
    kr = w_in[:, KR_OFF:KR_OFF + QK_ROPE]
    rest = w_in[:, KR_OFF + QK_ROPE:]
    return jnp.concatenate([cq, ckv, _pad_cols(kr, LANES), rest], axis=-1).astype(BF16)


def _pack_q(w_uq):
    w = w_uq.reshape(Q_LORA, N_HEADS, QK_DIM)
    z = jnp.zeros((Q_LORA, N_HEADS, HEAD_PAD - QK_DIM), w.dtype)
    return jnp.concatenate([w, z], axis=-1).reshape(Q_LORA, QKV_W).astype(BF16)


def _pack_kv(w_ukv):
    w = w_ukv.reshape(KV_LORA, N_HEADS, QK_NOPE + V_HEAD)
    z = jnp.zeros((KV_LORA, N_HEADS, HEAD_PAD - QK_NOPE), w.dtype)
    k_all = jnp.concatenate([w[..., :QK_NOPE], z], axis=-1).reshape(KV_LORA, QKV_W)
    v_all = jnp.concatenate([w[..., QK_NOPE:], z], axis=-1).reshape(KV_LORA, QKV_W)
    return jnp.concatenate([k_all, v_all], axis=-1).astype(BF16)


def _rope_tabs(s, tm):
    rows = s // GRID_W
    row = jnp.repeat(jnp.arange(rows, dtype=F32), GRID_W)
    col = jnp.tile(jnp.arange(GRID_W, dtype=F32), rows)
    half = QK_ROPE // 2
    inv = 1.0 / (ROPE_BASE ** (jnp.arange(0, half, 2, dtype=F32) / half))
    ar = row[:, None] * inv[None, :]
    ac = col[:, None] * inv[None, :]
    ang = jnp.concatenate([ar, ar, ac, ac], axis=-1)
    cos = jnp.concatenate([jnp.cos(ang), jnp.ones((tm, QK_ROPE), F32)], axis=0)
    sin = jnp.concatenate([jnp.sin(ang), jnp.zeros((tm, QK_ROPE), F32)], axis=0)
    t = cos.shape[0]
    first = (jnp.arange(QK_ROPE) % (QK_ROPE // 2)) < QK_ROPE // 4
    sin1 = jnp.where(first, -sin, 0.0)
    sin2 = jnp.where(first, 0.0, sin)
    scale = QK_DIM ** -0.5 * LOG2E

    def q_tab(rope_part, nope_val):
        return jnp.concatenate([jnp.full((t, QK_NOPE), nope_val, F32), rope_part * scale,
                                jnp.zeros((t, HEAD_PAD - QK_DIM), F32)], axis=-1)

    return (q_tab(cos, scale), q_tab(sin1, 0.0), q_tab(sin2, 0.0),
            _pad_cols(cos, LANES), _pad_cols(sin1, LANES), _pad_cols(sin2, LANES))


def kernel(x, c, ctx, c_ctx, w_mod, b_mod, norm_g, final_g, ab_w_in, ab_g_cq, ab_g_ckv, ab_w_uq, ab_w_ukv, hgrn_lb_logits, hgrn_g_norm, ab_w_out, ffn_w_gate, ffn_w_up, ffn_w_down, pool_w, pool_b, pool_scale, moe_router, moe_w_gate, moe_w_up, moe_w_down):
    b, s, d = x.shape
    ctx_len = ctx.shape[1]
    depth = w_mod.shape[0]
    dims = (b, s, ctx_len)
    n_lat, n_ctx = b * s, b * ctx_len
    assert d == D_MODEL and s % ROW_TILE == 0 and ctx_len % ROW_TILE == 0
    assert ctx_len % GLA_BLOCK == 0 and s % GLA_BLOCK == 0 and n_lat % ctx_len == 0

    h = (x.reshape(n_lat, d), ctx.reshape(n_ctx, d))
    n_groups = ((b + 1 + SUBLANES - 1) // SUBLANES) * SUBLANES
    cvec = jnp.concatenate([c, c_ctx[None, :], jnp.zeros((n_groups - b - 1, d), F32)], axis=0)
    mod_all = _mod_table(cvec, w_mod, b_mod).reshape(depth, n_groups, N_MOD, d)

    lb_p = jax.nn.softmax(hgrn_lb_logits.astype(F32), axis=0)
    lb_all = jnp.cumsum(lb_p, axis=0) - lb_p[:1]
    tabs = _rope_tabs(s, ROW_TILE)

    pending = None
    for l in range(depth):
        j = l // 2
        even = l % 2 == 0
        ctx_later = any(m % 2 == 0 for m in range(l + 1, depth))
        n_rows = n_lat + n_ctx if ctx_later else n_lat
        mod = mod_all[l]
        ng1, ng2 = norm_g[l, 0][None, :], norm_g[l, 1][None, :]
        if even:
            outs = _inproj(
                h, mod, ng1, _pack_in(ab_w_in[j]), ab_g_cq[j][None, :], ab_g_ckv[j][None, :],
                _pack_q(ab_w_uq[j]), _pack_kv(ab_w_ukv[j]), lb_all[j], tabs, dims, pending)
            (q, k, v, qh, kf, lgf, kb, lgb, vh, hg) = outs[:10]
            if pending is not None:
                h, pending = outs[10], None
            a = _attention_latent(q, k, v, dims)
            if ctx_later:
                a = jnp.concatenate([a, _attention_ctx(q, k, v, dims)], axis=0)
            o_f, o_b = _gla(qh, kf, lgf, kb, lgb, vh, dims)
            h = _mix_ffn(a, o_f, o_b, hg, hgrn_g_norm[j][None, :], ab_w_out[j].astype(BF16), h, mod, ng2,
                         ffn_w_gate[j].astype(BF16), ffn_w_up[j].astype(BF16),
                         ffn_w_down[j].astype(BF16), dims, n_rows)
        else:
            h = _pool(h, mod, ng1, pool_w[j].astype(BF16), pool_b[j], pool_scale[j][None, :], dims, n_rows)
            r_full = _pad_cols(moe_router[j], LANES)
            r_hi = r_full.astype(BF16)
            router = jnp.stack([r_hi, (r_full - r_hi.astype(F32)).astype(BF16)])
            last = l == depth - 1
            defer = not last and n_rows == n_lat + n_ctx
            h, pending = _moe(h, mod, ng2, router, moe_w_gate, moe_w_up, moe_w_down, j, dims, n_rows,
                              final_g[None, :] if last else None, defer)
            if last:
                return h.reshape(b, s, d)
    return _final_norm(h, final_g[None, :], n_lat).reshape(b, s, d)
```
